```python
import jax, jax.numpy as jnp
from jax import lax
import numpy as np

D_MODEL = 2048
BATCH = 4
SEQ = 2048
DEPTH = 2
DEC_BATCH = 128
DEC_SEQ = 1
PAST_LEN = 16384
PAGE_SIZE = 128

MIX_WIDTH = D_MODEL
CONV_DIM = MIX_WIDTH // 2
CONV_GROUPS = 16
CONV_W = 3
HG_HEADS = 8
HG_DV = (MIX_WIDTH - CONV_DIM) // HG_HEADS
HG_DK = 128
HG_CHUNK = 64
PROJ_SPLITS = (CONV_DIM, CONV_DIM, CONV_DIM,
               HG_HEADS * HG_DK, HG_HEADS * HG_DK, HG_HEADS * HG_DV, HG_HEADS * HG_DV)
PROJ_WIDTH = sum(PROJ_SPLITS)
D_FF = ((8 * D_MODEL // 3 + 255) // 256) * 256
N_EXPERTS = 8
TOP_K = 2
D_FF_EXPERT = 7 * D_MODEL // 2
MOE_BLOCK = 256
N_DENSE = (DEPTH + 1) // 2
N_MOE = DEPTH // 2
EPS = 1e-6
F_MIN = 1e-6

kernel_name = "hymba_conv_hgrn2_adaln_moe_step"


def _rmsnorm(x, g):
    xf = x.astype(jnp.float32)
    y = xf * lax.rsqrt(jnp.mean(xf * xf, axis=-1, keepdims=True) + EPS)
    return (y * g.astype(jnp.float32)).astype(x.dtype)


def _hgrn2_recurrence(q, logf, k, v, s0):
    b, t, h, dk = q.shape
    dv = v.shape[-1]
    L = min(HG_CHUNK, t)
    n = -(-t // L)
    pad = n * L - t

    def chunks(a):
        a = jnp.pad(a.astype(jnp.float32), ((0, 0), (0, pad), (0, 0), (0, 0)))
        return a.reshape(b, n, L, h, a.shape[-1]).transpose(1, 0, 2, 3, 4)

    mask = jnp.tril(jnp.ones((L, L), dtype=bool))[None, :, :, None, None]

    def step(S, xs):
        qc, lfc, kc, vc = xs
        A = jnp.cumsum(lfc, axis=1)
        diff = A[:, :, None] - A[:, None, :]
        decay = jnp.where(mask, jnp.exp(jnp.where(mask, diff, 0.0)), 0.0)
        scores = jnp.einsum('bthd,btshd,bshd->bhts', qc, decay, kc)
        o = (jnp.einsum('bhts,bshv->bthv', scores, vc)
             + jnp.einsum('bthd,bhdv->bthv', qc * jnp.exp(A), S))
        A_last = A[:, -1]
        S = (jnp.exp(A_last)[..., None] * S
             + jnp.einsum('bshd,bshv->bhdv', kc * jnp.exp(A_last[:, None] - A), vc))
        return S, o

    S, o = lax.scan(step, s0.astype(jnp.float32), (chunks(q), chunks(logf), chunks(k), chunks(v)))
    o = o.transpose(1, 0, 2, 3, 4).reshape(b, n * L, h, dv)[:, :t]
    return o, S


def _token_mixer(h, conv_state, hgrn_state, lb, w_in, conv_w, conv_norm, hgrn_norm, w_out):
    b, t, _ = h.shape
    proj = h @ w_in
    idx = list(np.cumsum(PROJ_SPLITS)[:-1])
    hc, bg, cg, q, fz, iv, og = jnp.split(proj, idx, axis=-1)
    u = cg * hc
    ext = jnp.concatenate([conv_state.astype(u.dtype), u], axis=1)
    conv = sum(conv_w[j] * ext[:, j:j + t] for j in range(CONV_W))
    yc = (bg * conv).reshape(b, t, CONV_GROUPS, CONV_DIM // CONV_GROUPS)
    yc = _rmsnorm(yc, conv_norm.reshape(CONV_GROUPS, -1)).reshape(b, t, CONV_DIM)
    new_conv = ext[:, -(CONV_W - 1):]
    z = fz.astype(jnp.float32).reshape(b, t, HG_HEADS, HG_DK)
    lbh = lb.reshape(HG_HEADS, HG_DK)
    sg = jax.nn.sigmoid(z)
    f = lbh + (1.0 - lbh) * sg
    logf = jnp.log(jnp.maximum(f, F_MIN))
    k = (1.0 - lbh) * (1.0 - sg)
    o, new_s = _hgrn2_recurrence(q.reshape(b, t, HG_HEADS, HG_DK), logf, k,
                                 iv.reshape(b, t, HG_HEADS, HG_DV), hgrn_state)
    o = _rmsnorm(o, hgrn_norm.reshape(HG_HEADS, HG_DV)).astype(h.dtype)
    o = (o * jax.nn.silu(og.reshape(b, t, HG_HEADS, HG_DV))).reshape(b, t, HG_HEADS * HG_DV)
    out = jnp.concatenate([yc, o], axis=-1) @ w_out
    return out, new_conv, new_s


def _swiglu(h, w1, w3, w2):
    return (jax.nn.silu(h @ w1) * (h @ w3)) @ w2


def _moe_block(m):
    per = -(-m // N_EXPERTS)
    return min(MOE_BLOCK, max(8, -(-per // 8) * 8))


def _moe_ffn(h2, router_w, router_b, w1, w3, w2):
    n, d = h2.shape
    logits = h2.astype(jnp.float32) @ router_w.astype(jnp.float32) + router_b.astype(jnp.float32)
    top_val, top_idx = lax.top_k(logits, TOP_K)
    gate = jax.nn.softmax(top_val, axis=-1)
    m = n * TOP_K
    e_flat = top_idx.reshape(-1)
    tok_flat = jnp.repeat(jnp.arange(n, dtype=jnp.int32), TOP_K)
    g_flat = gate.reshape(-1)
    blk = _moe_block(m)
    n_blocks = -(-(m + N_EXPERTS * (blk - 1)) // blk)
    P = n_blocks * blk
    order = jnp.argsort(e_flat)
    se = e_flat[order]
    counts = jnp.bincount(e_flat, length=N_EXPERTS)
    starts = jnp.cumsum(counts) - counts
    padded = (counts + blk - 1) // blk * blk
    pends = jnp.cumsum(padded)
    pstarts = pends - padded
    dest = pstarts[se] + jnp.arange(m, dtype=jnp.int32) - starts[se]
    tok_buf = jnp.full((P,), n, dtype=jnp.int32).at[dest].set(tok_flat[order])
    g_buf = jnp.zeros((P,), jnp.float32).at[dest].set(g_flat[order])
    block_e = jnp.minimum(jnp.searchsorted(pends, jnp.arange(n_blocks, dtype=jnp.int32) * blk,
                                           side='right'), N_EXPERTS - 1)
    h_pad = jnp.concatenate([h2, jnp.zeros((1, d), h2.dtype)], axis=0)
    xb = h_pad[tok_buf].reshape(n_blocks, blk, d)

    def expert_block(args):
        xblk, e = args
        return _swiglu(xblk, w1[e], w3[e], w2[e])

    yb = lax.map(expert_block, (xb, block_e)).reshape(P, d)
    y = jnp.zeros((n + 1, d), yb.dtype).at[tok_buf].add(yb * g_buf[:, None].astype(yb.dtype))
    return y[:n]


def _trunk(x, c, state_conv, state_hgrn, lb_all, norm_pre, norm_post, w_mod, b_mod, w_in, conv_w,
           conv_norm, hgrn_norm, w_out, ffn_w1, ffn_w3, ffn_w2, router_w, router_b,
           moe_w1, moe_w3, moe_w2):
    b, t, d = x.shape
    cf = jax.nn.silu(c)
    new_convs, new_hgrns = [], []
    for l in range(DEPTH):
        mod = (cf @ w_mod[l] + b_mod[l]).reshape(b, 6, 1, d)
        sh_a, sc_a, ga_a, sh_f, sc_f, ga_f = [mod[:, i] for i in range(6)]
        h = _rmsnorm(x, norm_pre[l, 0]) * (1 + sc_a) + sh_a
        mixed, nc, ns = _token_mixer(h, state_conv[l], state_hgrn[l], lb_all[l], w_in[l], conv_w[l],
                                     conv_norm[l], hgrn_norm[l], w_out[l])
        x = x + ga_a * _rmsnorm(mixed, norm_post[l, 0])
        new_convs.append(nc)
        new_hgrns.append(ns)
        h = _rmsnorm(x, norm_pre[l, 1]) * (1 + sc_f) + sh_f
        if l % 2 == 0:
            f = _swiglu(h, ffn_w1[l // 2], ffn_w3[l // 2], ffn_w2[l // 2])
        else:
            j = l // 2
            f = _moe_ffn(h.reshape(b * t, d), router_w[j], router_b[j], moe_w1[j], moe_w3[j],
                         moe_w2[j]).reshape(b, t, d)
        x = x + ga_f * _rmsnorm(f, norm_post[l, 1])
    return x, jnp.stack(new_convs), jnp.stack(new_hgrns)


def setup_inputs(seed: int = 0) -> dict:
    key = jax.random.key(seed)
    ks = jax.random.split(key, 26)
    nrm = jax.random.normal
    f32 = jnp.float32
    D = D_MODEL
    return {
        'x_prompt': nrm(ks[0], (BATCH, SEQ, D), f32),
        'x_sample': nrm(ks[1], (DEC_BATCH, DEC_SEQ, D), f32),
        'state_conv': nrm(ks[2], (DEPTH, DEC_BATCH, CONV_W - 1, CONV_DIM), f32),
        'state_hgrn': nrm(ks[3], (DEPTH, DEC_BATCH, HG_HEADS, HG_DK, HG_DV), f32),
        'c_prompt': nrm(ks[4], (BATCH, D), f32),
        'c_sample': nrm(ks[5], (DEC_BATCH, D), f32),
        'norm_pre': 1.0 + 0.02 * nrm(ks[6], (DEPTH, 2, D), f32),
        'norm_post': 1.0 + 0.02 * nrm(ks[7], (DEPTH, 2, D), f32),
        'w_mod': 0.5 * D ** -0.5 * nrm(ks[8], (DEPTH, D, 6 * D), f32),
        'b_mod': 0.02 * nrm(ks[9], (DEPTH, 6 * D), f32),
        'w_in': D ** -0.5 * nrm(ks[10], (DEPTH, D, PROJ_WIDTH), f32),
        'conv_w': CONV_W ** -0.5 * nrm(ks[11], (DEPTH, CONV_W, CONV_DIM), f32),
        'conv_norm': 1.0 + 0.02 * nrm(ks[12], (DEPTH, CONV_DIM), f32),
        'lb_logits': 0.5 * nrm(ks[13], (DEPTH, HG_HEADS * HG_DK), f32),
        'hgrn_norm': 1.0 + 0.02 * nrm(ks[14], (DEPTH, HG_HEADS * HG_DV), f32),
        'w_out': MIX_WIDTH ** -0.5 * nrm(ks[15], (DEPTH, MIX_WIDTH, D), f32),
        'ffn_w1': D ** -0.5 * nrm(ks[16], (N_DENSE, D, D_FF), f32),
        'ffn_w3': D ** -0.5 * nrm(ks[17], (N_DENSE, D, D_FF), f32),
        'ffn_w2': D_FF ** -0.5 * nrm(ks[18], (N_DENSE, D_FF, D), f32),
        'router_w': D ** -0.5 * nrm(ks[19], (N_MOE, D, N_EXPERTS), f32),
        'router_b': 0.01 * nrm(ks[20], (N_MOE, N_EXPERTS), f32),
        'moe_w1': D ** -0.5 * nrm(ks[21], (N_MOE, N_EXPERTS, D, D_FF_EXPERT), f32),
        'moe_w3': D ** -0.5 * nrm(ks[22], (N_MOE, N_EXPERTS, D, D_FF_EXPERT), f32),
        'moe_w2': D_FF_EXPERT ** -0.5 * nrm(ks[23], (N_MOE, N_EXPERTS, D_FF_EXPERT, D), f32),
    }


def reference(x_prompt, x_sample, state_conv, state_hgrn, c_prompt, c_sample, norm_pre, norm_post,
              w_mod, b_mod, w_in, conv_w, conv_norm, lb_logits, hgrn_norm, w_out, ffn_w1, ffn_w3,
              ffn_w2, router_w, router_b, moe_w1, moe_w3, moe_w2):
    p = jax.nn.softmax(lb_logits.astype(jnp.float32), axis=0)
    lb_all = jnp.cumsum(p, axis=0) - p[0:1]
    weights = (lb_all, norm_pre, norm_post, w_mod, b_mod, w_in, conv_w, conv_norm, hgrn_norm, w_out,
               ffn_w1, ffn_w3, ffn_w2, router_w, router_b, moe_w1, moe_w3, moe_w2)
    zero_conv = jnp.zeros((DEPTH, x_prompt.shape[0], CONV_W - 1, CONV_DIM), x_prompt.dtype)
    zero_hgrn = jnp.zeros((DEPTH, x_prompt.shape[0], HG_HEADS, HG_DK, HG_DV), jnp.float32)
    y_prompt, new_conv_prompt, new_hgrn_prompt = _trunk(x_prompt, c_prompt, zero_conv, zero_hgrn, *weights)
    y_sample, new_conv_sample, new_hgrn_sample = _trunk(x_sample, c_sample, state_conv, state_hgrn, *weights)
    return (y_prompt, y_sample, new_conv_prompt, new_hgrn_prompt, new_conv_sample, new_hgrn_sample)
```

```python
import functools

import jax
import jax.numpy as jnp
from jax import lax
from jax.experimental import pallas as pl
from jax.experimental.pallas import tpu as pltpu

F32 = jnp.float32
BF16 = jnp.bfloat16
I32 = jnp.int32
HIGHEST = lax.Precision.HIGHEST

LANES = 128
SUBLANES = 8
VMEM_LIMIT_BYTES = 56 * 1024 * 1024

D_MODEL = 2048
BATCH = 4
SEQ = 2048
DEPTH = 2
DEC_BATCH = 128
N_PROMPT = BATCH * SEQ
N_ROWS = N_PROMPT + DEC_BATCH
CONV_DIM = 1024
CONV_GROUPS = 16
CONV_W = 3
HG_HEADS = 8
HG_DK = 128
HG_DV = 128
PROJ_WIDTH = 7168
N_EXPERTS = 8
EPS = 1e-6
F_MIN = 1e-6

ROW_TILE = 128
N_TILES = N_ROWS // ROW_TILE
PROMPT_TILES = N_PROMPT // ROW_TILE
TILES_PER_SEQ = SEQ // ROW_TILE
SUB = 16
MOE_BLOCK = 256
N_PAIRS = 2 * N_ROWS
N_BLOCKS = (N_PAIRS + N_EXPERTS * (MOE_BLOCK - 1) + MOE_BLOCK - 1) // MOE_BLOCK
P_ROWS = N_BLOCKS * MOE_BLOCK


def _cparams(sem):
    return pltpu.CompilerParams(dimension_semantics=sem, vmem_limit_bytes=VMEM_LIMIT_BYTES)


def _rms(x, g):
    return x * lax.rsqrt(jnp.mean(x * x, axis=-1, keepdims=True) + EPS) * g


def _silu(x):
    return x * jax.nn.sigmoid(x)


def _gmm_kernel(be_ref, nact_ref, a_ref, w_ref, *rest, lhs_silu, has_bias):
    if has_bias:
        b_ref, o_ref, wb_ref = rest
    else:
        o_ref, wb_ref = rest
    i = pl.program_id(1)
    prev = be_ref[jnp.maximum(i - 1, 0)]

    @pl.when((i == 0) | (be_ref[i] != prev))
    def _():
        wb_ref[...] = w_ref[0].astype(BF16)

    @pl.when(i < nact_ref[0])
    def _():
        a = a_ref[...]
        if lhs_silu:
            a = _silu(a)
        acc = jnp.dot(a.astype(BF16), wb_ref[...], preferred_element_type=F32)
        if has_bias:
            acc = acc + b_ref[0]
        o_ref[...] = acc.astype(o_ref.dtype)

    @pl.when(i >= nact_ref[0])
    def _():
        o_ref[...] = jnp.zeros_like(o_ref)


def _gmm(a, w, block_e, nact, *, tm, tn, out_dtype, bias=None, lhs_silu=False):
    m, k = a.shape
    _, _, n = w.shape
    grid = (n // tn, m // tm)
    in_specs = [
        pl.BlockSpec((tm, k), lambda j, i, be, na: (i, 0)),
        pl.BlockSpec((1, k, tn), lambda j, i, be, na: (be[i], 0, j)),
    ]
    args = [a, w]
    if bias is not None:
        in_specs.append(pl.BlockSpec((1, 1, tn), lambda j, i, be, na: (be[i], 0, j)))
        args.append(bias)
    return pl.pallas_call(
        functools.partial(_gmm_kernel, lhs_silu=lhs_silu, has_bias=bias is not None),
        grid_spec=pltpu.PrefetchScalarGridSpec(
            num_scalar_prefetch=2,
            grid=grid,
            in_specs=in_specs,
            out_specs=pl.BlockSpec((tm, tn), lambda j, i, be, na: (i, j)),
            scratch_shapes=[pltpu.VMEM((k, tn), BF16)],
        ),
        out_shape=jax.ShapeDtypeStruct((m, n), out_dtype),
        compiler_params=_cparams(("arbitrary", "arbitrary")),
    )(block_e, nact, *args)


def _gmm_swiglu_kernel(be_ref, nact_ref, a_ref, w1_ref, w3_ref, o_ref, w1b_ref, w3b_ref):
    i = pl.program_id(1)
    prev = be_ref[jnp.maximum(i - 1, 0)]

    @pl.when((i == 0) | (be_ref[i] != prev))
    def _():
        w1b_ref[...] = w1_ref[0].astype(BF16)
        w3b_ref[...] = w3_ref[0].astype(BF16)

    @pl.when(i < nact_ref[0])
    def _():
        a = a_ref[...]
        u = jnp.dot(a, w1b_ref[...], preferred_element_type=F32)
        g = jnp.dot(a, w3b_ref[...], preferred_element_type=F32)
        o_ref[...] = (_silu(u) * g).astype(o_ref.dtype)

    @pl.when(i >= nact_ref[0])
    def _():
        o_ref[...] = jnp.zeros_like(o_ref)


def _gmm_swiglu(a, w1, w3, block_e, nact, *, tm, tf):
    m, k = a.shape
    _, _, f = w1.shape
    grid = (f // tf, m // tm)
    wspec = pl.BlockSpec((1, k, tf), lambda j, i, be, na: (be[i], 0, j))
    return pl.pallas_call(
        _gmm_swiglu_kernel,
        grid_spec=pltpu.PrefetchScalarGridSpec(
            num_scalar_prefetch=2,
            grid=grid,
            in_specs=[pl.BlockSpec((tm, k), lambda j, i, be, na: (i, 0)), wspec, wspec],
            out_specs=pl.BlockSpec((tm, tf), lambda j, i, be, na: (i, j)),
            scratch_shapes=[pltpu.VMEM((k, tf), BF16), pltpu.VMEM((k, tf), BF16)],
        ),
        out_shape=jax.ShapeDtypeStruct((m, f), BF16),
        compiler_params=_cparams(("arbitrary", "arbitrary")),
    )(block_e, nact, a, w1, w3)


def _pick(i, p_ref, s_ref):
    return jnp.where(i < PROMPT_TILES, p_ref[0], s_ref[...])


def _prenorm_kernel(x_ref, g_ref, scp_ref, scs_ref, shp_ref, shs_ref, h_ref):
    i = pl.program_id(0)
    sc = _pick(i, scp_ref, scs_ref)
    sh = _pick(i, shp_ref, shs_ref)
    h_ref[...] = (_rms(x_ref[...], g_ref[...]) * (1.0 + sc) + sh).astype(h_ref.dtype)


def _resid_kernel(*refs, with_next, with_router, h_dtype):
    it = iter(refs)
    x_ref, f_ref, gpost_ref, gap_ref, gas_ref = [next(it) for _ in range(5)]
    if with_next:
        gpre_ref, scp_ref, scs_ref, shp_ref, shs_ref = [next(it) for _ in range(5)]
    if with_router:
        rw_ref, rb_ref = next(it), next(it)
    xo_ref = next(it)
    if with_next:
        h_ref = next(it)
    if with_router:
        lg_ref = next(it)
    i = pl.program_id(0)
    ga = _pick(i, gap_ref, gas_ref)
    x = x_ref[...] + ga * _rms(f_ref[...], gpost_ref[...])
    xo_ref[...] = x
    if with_next:
        sc = _pick(i, scp_ref, scs_ref)
        sh = _pick(i, shp_ref, shs_ref)
        h = _rms(x, gpre_ref[...]) * (1.0 + sc) + sh
        h_ref[...] = h.astype(h_dtype)
        if with_router:
            lg_ref[...] = jnp.dot(h, rw_ref[...], precision=HIGHEST,
                                  preferred_element_type=F32) + rb_ref[...]


def _row_spec(width):
    return pl.BlockSpec((ROW_TILE, width), lambda i: (i, 0))


def _vec_spec():
    return pl.BlockSpec((1, D_MODEL), lambda i: (0, 0))


def _modp_spec():
    return pl.BlockSpec((1, 1, D_MODEL), lambda i: (jnp.minimum(i // TILES_PER_SEQ, BATCH - 1), 0, 0))


def _mods_spec():
    return pl.BlockSpec((DEC_BATCH, D_MODEL), lambda i: (0, 0))


def _prenorm(x, g, sc, sh):
    return pl.pallas_call(
        _prenorm_kernel,
        grid=(N_TILES,),
        in_specs=[_row_spec(D_MODEL), _vec_spec(), _modp_spec(), _mods_spec(), _modp_spec(), _mods_spec()],
        out_specs=_row_spec(D_MODEL),
        out_shape=jax.ShapeDtypeStruct((N_ROWS, D_MODEL), BF16),
        compiler_params=_cparams(("arbitrary",)),
    )(x, g, sc[0], sc[1], sh[0], sh[1])


def _resid(x, f, gpost, ga, nxt=None, router=None, h_dtype=BF16):
    args = [x, f, gpost, ga[0], ga[1]]
    in_specs = [_row_spec(D_MODEL), _row_spec(D_MODEL), _vec_spec(), _modp_spec(), _mods_spec()]
    out_shape = [jax.ShapeDtypeStruct((N_ROWS, D_MODEL), F32)]
    out_specs = [_row_spec(D_MODEL)]
    if nxt is not None:
        gpre, sc, sh = nxt
        args += [gpre, sc[0], sc[1], sh[0], sh[1]]
        in_specs += [_vec_spec(), _modp_spec(), _mods_spec(), _modp_spec(), _mods_spec()]
        out_shape.append(jax.ShapeDtypeStruct((N_ROWS, D_MODEL), h_dtype))
        out_specs.append(_row_spec(D_MODEL))
    if router is not None:
        rw, rb = router
        args += [rw, rb]
        in_specs += [pl.BlockSpec((D_MODEL, LANES), lambda i: (0, 0)), pl.BlockSpec((1, LANES), lambda i: (0, 0))]
        out_shape.append(jax.ShapeDtypeStruct((N_ROWS, LANES), F32))
        out_specs.append(_row_spec(LANES))
    return pl.pallas_call(
        functools.partial(_resid_kernel, with_next=nxt is not None, with_router=router is not None,
                          h_dtype=h_dtype),
        grid=(N_TILES,),
        in_specs=in_specs,
        out_specs=out_specs,
        out_shape=out_shape,
        compiler_params=_cparams(("arbitrary",)),
    )(*args)


def _group_rms(y, gnorm):
    width = CONV_DIM // CONV_GROUPS
    c = lax.broadcasted_iota(I32, (CONV_DIM, LANES), 0) // width
    g = lax.broadcasted_iota(I32, (CONV_DIM, LANES), 1)
    sel = (c == g).astype(F32)
    ct = lax.broadcasted_iota(I32, (LANES, CONV_DIM), 1) // width
    gt = lax.broadcasted_iota(I32, (LANES, CONV_DIM), 0)
    sel_t = (ct == gt).astype(F32)
    ss = jnp.dot(y * y, sel, precision=HIGHEST, preferred_element_type=F32)
    inv = lax.rsqrt(ss * (1.0 / width) + EPS)
    return y * jnp.dot(inv, sel_t, precision=HIGHEST, preferred_element_type=F32) * gnorm


def _conv_prompt_kernel(hc_ref, bg_ref, cg_ref, hch_ref, cgh_ref, cw_ref, gn_ref, y_ref, nc_ref, *, tt):
    t = pl.program_id(1)
    u = cg_ref[...] * hc_ref[...]
    halo = jnp.where(t == 0, 0.0, cgh_ref[...] * hch_ref[...])
    h1 = halo[SUBLANES - 1:SUBLANES]
    h2 = halo[SUBLANES - 2:SUBLANES - 1]
    row = lax.broadcasted_iota(I32, u.shape, 0)
    u1 = jnp.where(row == 0, h1, pltpu.roll(u, 1, 0))
    u2 = jnp.where(row == 0, h2, jnp.where(row == 1, h1, pltpu.roll(u, 2, 0)))
    conv = cw_ref[0:1] * u2 + cw_ref[1:2] * u1 + cw_ref[2:3] * u
    y_ref[...] = _group_rms(bg_ref[...] * conv, gn_ref[...]).astype(y_ref.dtype)

    @pl.when(t == pl.num_programs(1) - 1)
    def _():
        nc_ref[0] = u[tt - (CONV_W - 1):]


def _conv_prompt(proj, conv_w, conv_norm, *, tt=256):
    nt = SEQ // tt
    cblk = lambda c: pl.BlockSpec((tt, CONV_DIM), lambda b, t: (b * nt + t, c))
    hblk = lambda c: pl.BlockSpec(
        (SUBLANES, CONV_DIM), lambda b, t: (jnp.maximum((b * nt + t) * (tt // SUBLANES) - 1, 0), c))
    return pl.pallas_call(
        functools.partial(_conv_prompt_kernel, tt=tt),
        grid=(BATCH, nt),
        in_specs=[cblk(0), cblk(1), cblk(2), hblk(0), hblk(2),
                  pl.BlockSpec((CONV_W, CONV_DIM), lambda b, t: (0, 0)),
                  pl.BlockSpec((1, CONV_DIM), lambda b, t: (0, 0))],
        out_specs=[pl.BlockSpec((tt, CONV_DIM), lambda b, t: (b * nt + t, 0)),
                   pl.BlockSpec((1, CONV_W - 1, CONV_DIM), lambda b, t: (b, 0, 0))],
        out_shape=[jax.ShapeDtypeStruct((N_PROMPT, CONV_DIM), BF16),
                   jax.ShapeDtypeStruct((BATCH, CONV_W - 1, CONV_DIM), F32)],
        compiler_params=_cparams(("arbitrary", "arbitrary")),
    )(proj, proj, proj, proj, proj, conv_w, conv_norm)


def _gates(z, lb):
    sg = jax.nn.sigmoid(z)
    f = lb + (1.0 - lb) * sg
    return jnp.log(jnp.maximum(f, F_MIN)), (1.0 - lb) * (1.0 - sg)


def _hgrn_tile(q, z, v, lb, st):
    n_sub = ROW_TILE // SUB
    logf, k = _gates(z, lb)
    r = lax.broadcasted_iota(I32, (ROW_TILE, ROW_TILE), 0)
    c = lax.broadcasted_iota(I32, (ROW_TILE, ROW_TILE), 1)
    tri = ((r // SUB == c // SUB) & (c <= r)).astype(F32)
    a = jnp.dot(tri, logf, precision=HIGHEST, preferred_element_type=F32)
    shape3 = (n_sub, SUB, HG_DK)
    a3, q3, k3, v3 = (x.reshape(shape3) for x in (a, q, k, v))
    tpos = lax.broadcasted_iota(I32, shape3, 1)
    ones = jnp.ones((HG_DK, HG_DV), F32)
    acc = jnp.zeros(shape3, F32)
    for s in range(SUB):
        a_s, k_s, v_s = a3[:, s:s + 1, :], k3[:, s:s + 1, :], v3[:, s:s + 1, :]
        m = q3 * jnp.exp(jnp.minimum(a3 - a_s, 0.0)) * k_s
        score = jnp.dot(m.reshape(ROW_TILE, HG_DK), ones, precision=HIGHEST,
                        preferred_element_type=F32)
        acc = acc + jnp.where(tpos >= s, score.reshape(shape3), 0.0) * v_s
    a_last = a3[:, SUB - 1:SUB, :]
    qt = (q * jnp.exp(a)).astype(BF16)
    kt = (k3 * jnp.exp(a_last - a3)).reshape(ROW_TILE, HG_DK).astype(BF16)
    vb = v.astype(BF16)
    decay = jnp.exp(a_last)
    inter = []
    for ci in range(n_sub):
        rows = slice(ci * SUB, (ci + 1) * SUB)
        inter.append(lax.dot_general(qt[rows], st.astype(BF16), (((1,), (1,)), ((), ())),
                                     preferred_element_type=F32))
        st = st * decay[ci] + lax.dot_general(vb[rows], kt[rows], (((0,), (0,)), ((), ())),
                                              preferred_element_type=F32)
    o = acc.reshape(ROW_TILE, HG_DV) + jnp.concatenate(inter, axis=0)
    return o, st


def _hgrn_prompt_kernel(q_ref, z_ref, v_ref, og_ref, lb_ref, gn_ref, o_ref, s_ref, st_ref, *, tt):
    t = pl.program_id(2)

    @pl.when(t == 0)
    def _():
        st_ref[...] = jnp.zeros_like(st_ref)

    lb = lb_ref[0]
    gn = gn_ref[0]

    def body(j, carry):
        rows = pl.ds(pl.multiple_of(j * ROW_TILE, ROW_TILE), ROW_TILE)
        o, st = _hgrn_tile(q_ref[rows, :], z_ref[rows, :], v_ref[rows, :], lb, st_ref[...])
        st_ref[...] = st
        o_ref[rows, :] = (_rms(o, gn) * _silu(og_ref[rows, :])).astype(o_ref.dtype)
        return carry

    lax.fori_loop(0, tt // ROW_TILE, body, 0)

    @pl.when(t == pl.num_programs(2) - 1)
    def _():
        s_ref[0, 0] = st_ref[...].T


def _hgrn_prompt(proj, lb, hgrn_norm, *, tt=512):
    nt = SEQ // tt
    col0 = 3 * CONV_DIM // LANES
    blk = lambda part: pl.BlockSpec((tt, LANES), lambda b, h, t: (b * nt + t, col0 + part * HG_HEADS + h))
    hvec = pl.BlockSpec((1, 1, LANES), lambda b, h, t: (h, 0, 0))
    return pl.pallas_call(
        functools.partial(_hgrn_prompt_kernel, tt=tt),
        grid=(BATCH, HG_HEADS, nt),
        in_specs=[blk(0), blk(1), blk(2), blk(3), hvec, hvec],
        out_specs=[pl.BlockSpec((tt, LANES), lambda b, h, t: (b * nt + t, h)),
                   pl.BlockSpec((1, 1, HG_DK, HG_DV), lambda b, h, t: (b, h, 0, 0))],
        out_shape=[jax.ShapeDtypeStruct((N_PROMPT, HG_HEADS * HG_DV), BF16),
                   jax.ShapeDtypeStruct((BATCH, HG_HEADS, HG_DK, HG_DV), F32)],
        scratch_shapes=[pltpu.VMEM((HG_DV, HG_DK), F32)],
        compiler_params=_cparams(("arbitrary", "arbitrary", "arbitrary")),
    )(proj, proj, proj, proj, lb.reshape(HG_HEADS, 1, HG_DK), hgrn_norm.reshape(HG_HEADS, 1, HG_DV))


def _conv_sample_kernel(hc_ref, bg_ref, cg_ref, cs_ref, cw_ref, gn_ref, y_ref, nc_ref):
    u = cg_ref[...] * hc_ref[...]
    s0 = cs_ref[:, 0, :]
    s1 = cs_ref[:, 1, :]
    conv = cw_ref[0:1] * s0 + cw_ref[1:2] * s1 + cw_ref[2:3] * u
    y_ref[...] = _group_rms(bg_ref[...] * conv, gn_ref[...]).astype(y_ref.dtype)
    nc_ref[:, 0, :] = s1
    nc_ref[:, 1, :] = u


def _conv_sample(proj, conv_state, conv_w, conv_norm):
    rb = PROMPT_TILES
    cblk = lambda c: pl.BlockSpec((DEC_BATCH, CONV_DIM), lambda i: (rb, c))
    return pl.pallas_call(
        _conv_sample_kernel,
        grid=(1,),
        in_specs=[cblk(0), cblk(1), cblk(2),
                  pl.BlockSpec((DEC_BATCH, CONV_W - 1, CONV_DIM), lambda i: (0, 0, 0)),
                  pl.BlockSpec((CONV_W, CONV_DIM), lambda i: (0, 0)),
                  pl.BlockSpec((1, CONV_DIM), lambda i: (0, 0))],
        out_specs=[pl.BlockSpec((DEC_BATCH, CONV_DIM), lambda i: (0, 0)),
                   pl.BlockSpec((DEC_BATCH, CONV_W - 1, CONV_DIM), lambda i: (0, 0, 0))],
        out_shape=[jax.ShapeDtypeStruct((DEC_BATCH, CONV_DIM), BF16),
                   jax.ShapeDtypeStruct((DEC_BATCH, CONV_W - 1, CONV_DIM), F32)],
        compiler_params=_cparams(("arbitrary",)),
    )(proj, proj, proj, conv_state, conv_w, conv_norm)


def _hgrn_sample_kernel(q_ref, z_ref, v_ref, og_ref, lb_ref, gn_ref, s_ref, o_ref, so_ref, osc_ref, *, bg):
    g = pl.program_id(1)
    lb = lb_ref[0]
    logf, k = _gates(z_ref[...], lb)
    f = jnp.exp(logf)
    shift = (DEC_BATCH - g * bg) % DEC_BATCH
    ft = pltpu.roll(f.T, shift, 1)
    kt = pltpu.roll(k.T, shift, 1)
    qt = pltpu.roll(q_ref[...].T, shift, 1)
    rows = pl.ds(pl.multiple_of(g * bg, bg), bg)
    v = v_ref[rows, :]
    for j in range(bg):
        s_new = ft[:, j:j + 1] * s_ref[j, 0] + kt[:, j:j + 1] * v[j:j + 1, :]
        so_ref[j, 0] = s_new
        osc_ref[j:j + 1, :] = jnp.sum(qt[:, j:j + 1] * s_new, axis=0, keepdims=True)
    o = osc_ref[...]
    o_ref[...] = (_rms(o, gn_ref[0]) * _silu(og_ref[rows, :])).astype(o_ref.dtype)


def _hgrn_sample(proj, state, lb, hgrn_norm, *, bg=16):
    rb = PROMPT_TILES
    col0 = 3 * CONV_DIM // LANES
    blk = lambda part: pl.BlockSpec((DEC_BATCH, LANES), lambda h, g: (rb, col0 + part * HG_HEADS + h))
    hvec = pl.BlockSpec((1, 1, LANES), lambda h, g: (h, 0, 0))
    sblk = pl.BlockSpec((bg, 1, HG_DK, HG_DV), lambda h, g: (g, h, 0, 0))
    return pl.pallas_call(
        functools.partial(_hgrn_sample_kernel, bg=bg),
        grid=(HG_HEADS, DEC_BATCH // bg),
        in_specs=[blk(0), blk(1), blk(2), blk(3), hvec, hvec, sblk],
        out_specs=[pl.BlockSpec((bg, LANES), lambda h, g: (g, h)), sblk],
        out_shape=[jax.ShapeDtypeStruct((DEC_BATCH, HG_HEADS * HG_DV), BF16),
                   jax.ShapeDtypeStruct(state.shape, F32)],
        scratch_shapes=[pltpu.VMEM((bg, HG_DV), F32)],
        compiler_params=_cparams(("arbitrary", "arbitrary")),
    )(proj, proj, proj, proj, lb.reshape(HG_HEADS, 1, HG_DK), hgrn_norm.reshape(HG_HEADS, 1, HG_DV), state)


def _route_kernel(lg_ref, ri_ref, gate_ref, cnt_ref, carry_ref):
    i = pl.program_id(0)

    @pl.when(i == 0)
    def _():
        carry_ref[...] = jnp.zeros_like(carry_ref)

    lane = lax.broadcasted_iota(I32, (ROW_TILE, LANES), 1)
    lanef = lane.astype(F32)
    lg = jnp.where(lane < N_EXPERTS, lg_ref[...], -jnp.inf)
    m1 = jnp.max(lg, axis=-1, keepdims=True)
    i1 = jnp.min(jnp.where(lg == m1, lanef, float(LANES)), axis=-1, keepdims=True).astype(I32)
    lg2 = jnp.where(lane == i1, -jnp.inf, lg)
    m2 = jnp.max(lg2, axis=-1, keepdims=True)
    i2 = jnp.min(jnp.where(lg2 == m2, lanef, float(LANES)), axis=-1, keepdims=True).astype(I32)
    e = jnp.exp(m2 - m1)
    g1 = 1.0 / (1.0 + e)
    g2 = e / (1.0 + e)
    hot1 = lane == i1
    hot2 = lane == i2
    hot = (hot1 | hot2).astype(BF16)
    r = lax.broadcasted_iota(I32, (ROW_TILE, ROW_TILE), 0)
    c = lax.broadcasted_iota(I32, (ROW_TILE, ROW_TILE), 1)
    before = (c < r).astype(BF16)
    tot = jnp.dot(before, hot, preferred_element_type=F32) + carry_ref[...]
    r1 = jnp.sum(jnp.where(hot1, tot, 0.0), axis=-1, keepdims=True).astype(I32)
    r2 = jnp.sum(jnp.where(hot2, tot, 0.0), axis=-1, keepdims=True).astype(I32)
    ri_ref[...] = jnp.where(lane == 0, i1, jnp.where(lane == 1, i2, jnp.where(lane == 2, r1,
                            jnp.where(lane == 3, r2, 0))))
    gate_ref[...] = jnp.where(lane == 0, g1, jnp.where(lane == 1, g2, 0.0))
    carry_ref[...] += jnp.sum(hot.astype(F32), axis=0, keepdims=True)
    cnt_ref[...] = carry_ref[...]


def _route(logits):
    return pl.pallas_call(
        _route_kernel,
        grid=(N_TILES,),
        in_specs=[_row_spec(LANES)],
        out_specs=[_row_spec(LANES), _row_spec(LANES), pl.BlockSpec((1, LANES), lambda i: (0, 0))],
        out_shape=[jax.ShapeDtypeStruct((N_ROWS, LANES), I32),
                   jax.ShapeDtypeStruct((N_ROWS, LANES), F32),
                   jax.ShapeDtypeStruct((1, LANES), F32)],
        scratch_shapes=[pltpu.VMEM((1, LANES), F32)],
        compiler_params=_cparams(("arbitrary",)),
    )(logits)


def _gather_kernel(d1_ref, d2_ref, nact_ref, h_ref, xb_ref, tok_ref, buf_ref, sem):
    i = pl.program_id(0)

    @pl.when(i == 0)
    def _():
        def clear(p, carry):
            tok_ref[p] = 0
            return carry
        lax.fori_loop(0, P_ROWS, clear, 0)

        def scatter(t, carry):
            tok_ref[d1_ref[t]] = t
            tok_ref[d2_ref[t]] = t
            return carry
        lax.fori_loop(0, N_ROWS, scatter, 0)

    def row_copy(r):
        tok = tok_ref[i * MOE_BLOCK + r]
        return pltpu.make_async_copy(h_ref.at[pl.ds(tok, 1), :], buf_ref.at[pl.ds(r, 1), :], sem)

    @pl.when(i < nact_ref[0])
    def _():
        def start(r, carry):
            row_copy(r).start()
            return carry
        lax.fori_loop(0, MOE_BLOCK, start, 0)

        def wait(r, carry):
            row_copy(r).wait()
            return carry
        lax.fori_loop(0, MOE_BLOCK, wait, 0)
        xb_ref[...] = buf_ref[...].astype(xb_ref.dtype)

    @pl.when(i >= nact_ref[0])
    def _():
        xb_ref[...] = jnp.zeros_like(xb_ref)


def _gather(h, dest1, dest2, nact):
    return pl.pallas_call(
        _gather_kernel,
        grid_spec=pltpu.PrefetchScalarGridSpec(
            num_scalar_prefetch=3,
            grid=(N_BLOCKS,),
            in_specs=[pl.BlockSpec(memory_space=pl.ANY)],
            out_specs=pl.BlockSpec((MOE_BLOCK, D_MODEL), lambda i, d1, d2, na: (i, 0)),
            scratch_shapes=[pltpu.SMEM((P_ROWS,), I32), pltpu.VMEM((MOE_BLOCK, D_MODEL), F32),
                            pltpu.SemaphoreType.DMA(())],
        ),
        out_shape=jax.ShapeDtypeStruct((P_ROWS, D_MODEL), BF16),
        compiler_params=_cparams(("arbitrary",)),
    )(dest1, dest2, nact, h)


def _combine_kernel(d1_ref, d2_ref, gate_ref, yb_ref, f_ref, b1_ref, b2_ref, sem):
    i = pl.program_id(0)

    def copies(r):
        t = i * ROW_TILE + r
        return (pltpu.make_async_copy(yb_ref.at[pl.ds(d1_ref[t], 1), :], b1_ref.at[pl.ds(r, 1), :], sem.at[0]),
                pltpu.make_async_copy(yb_ref.at[pl.ds(d2_ref[t], 1), :], b2_ref.at[pl.ds(r, 1), :], sem.at[1]))

    def start(r, carry):
        c1, c2 = copies(r)
        c1.start()
        c2.start()
        return carry
    lax.fori_loop(0, ROW_TILE, start, 0)

    def wait(r, carry):
        c1, c2 = copies(r)
        c1.wait()
        c2.wait()
        return carry
    lax.fori_loop(0, ROW_TILE, wait, 0)
    gate = gate_ref[...]
    f_ref[...] = gate[:, 0:1] * b1_ref[...] + gate[:, 1:2] * b2_ref[...]


def _combine(yb, gates, dest1, dest2):
    return pl.pallas_call(
        _combine_kernel,
        grid_spec=pltpu.PrefetchScalarGridSpec(
            num_scalar_prefetch=2,
            grid=(N_TILES,),
            in_specs=[pl.BlockSpec((ROW_TILE, LANES), lambda i, d1, d2: (i, 0)),
                      pl.BlockSpec(memory_space=pl.ANY)],
            out_specs=pl.BlockSpec((ROW_TILE, D_MODEL), lambda i, d1, d2: (i, 0)),
            scratch_shapes=[pltpu.VMEM((ROW_TILE, D_MODEL), F32), pltpu.VMEM((ROW_TILE, D_MODEL), F32),
                            pltpu.SemaphoreType.DMA((2,))],
        ),
        out_shape=jax.ShapeDtypeStruct((N_ROWS, D_MODEL), F32),
        compiler_params=_cparams(("arbitrary",)),
    )(dest1, dest2, gates, yb)


def _moe_ffn(h, logits, w1, w3, w2):
    ri, gates, counts = _route(logits)
    counts = counts[0, :N_EXPERTS].astype(I32)
    padded = (counts + MOE_BLOCK - 1) // MOE_BLOCK * MOE_BLOCK
    pends = jnp.cumsum(padded)
    pstarts = pends - padded
    dest1 = pstarts[ri[:, 0]] + ri[:, 2]
    dest2 = pstarts[ri[:, 1]] + ri[:, 3]
    nact = (pends[-1:] // MOE_BLOCK).astype(I32)
    block_e = jnp.minimum(
        jnp.searchsorted(pends, jnp.arange(N_BLOCKS, dtype=I32) * MOE_BLOCK, side='right'),
        N_EXPERTS - 1).astype(I32)
    xb = _gather(h, dest1, dest2, nact)
    gb = _gmm_swiglu(xb, w1, w3, block_e, nact, tm=MOE_BLOCK, tf=1024)
    yb = _gmm(gb, w2, block_e, nact, tm=MOE_BLOCK, tn=512, out_dtype=F32)
    return _combine(yb, gates, dest1, dest2)


def _dense_blocks(m, tm, e):
    nb = m // tm
    return jnp.full((nb,), e, I32), jnp.full((1,), nb, I32)


def kernel(x_prompt, x_sample, state_conv, state_hgrn, c_prompt, c_sample, norm_pre, norm_post, w_mod, b_mod, w_in, conv_w, conv_norm, lb_logits, hgrn_norm, w_out, ffn_w1, ffn_w3, ffn_w2, router_w, router_b, moe_w1, moe_w3, moe_w2):
    p = jax.nn.softmax(lb_logits.astype(F32), axis=0)
    lb_all = jnp.cumsum(p, axis=0) - p[0:1]

    n_cond = BATCH + DEC_BATCH
    cond_rows = (n_cond + SUBLANES - 1) // SUBLANES * SUBLANES
    c_all = jnp.concatenate([c_prompt, c_sample, jnp.zeros((cond_rows - n_cond, D_MODEL), F32)], axis=0)
    mod = _gmm(jnp.concatenate([c_all] * DEPTH, axis=0), w_mod, jnp.arange(DEPTH, dtype=I32),
               jnp.full((1,), DEPTH, I32), tm=cond_rows, tn=1024, out_dtype=F32,
               bias=b_mod.reshape(DEPTH, 1, 6 * D_MODEL), lhs_silu=True)
    mod = mod.reshape(DEPTH, cond_rows, 6, D_MODEL)

    def mod_vec(l, j):
        return mod[l, :BATCH, j].reshape(BATCH, 1, D_MODEL), mod[l, BATCH:n_cond, j]

    x = jnp.concatenate([x_prompt.reshape(N_PROMPT, D_MODEL), x_sample.reshape(DEC_BATCH, D_MODEL)], axis=0)
    vec = lambda a: a.reshape(1, -1)

    h = _prenorm(x, vec(norm_pre[0, 0]), mod_vec(0, 1), mod_vec(0, 0))
    new_conv_p, new_hgrn_p, new_conv_s, new_hgrn_s = [], [], [], []
    for l in range(DEPTH):
        sh_f, sc_f, ga_f = mod_vec(l, 3), mod_vec(l, 4), mod_vec(l, 5)
        ga_a = mod_vec(l, 2)
        be_l, na_dense = _dense_blocks(N_ROWS, 640, l)
        proj = _gmm(h, w_in, be_l, na_dense, tm=640, tn=1024, out_dtype=F32)
        yc_p, nc_p = _conv_prompt(proj, conv_w[l], vec(conv_norm[l]))
        o_p, ns_p = _hgrn_prompt(proj, lb_all[l], hgrn_norm[l])
        yc_s, nc_s = _conv_sample(proj, state_conv[l], conv_w[l], vec(conv_norm[l]))
        o_s, ns_s = _hgrn_sample(proj, state_hgrn[l], lb_all[l], hgrn_norm[l])
        mix = jnp.concatenate([jnp.concatenate([yc_p, o_p], axis=1),
                               jnp.concatenate([yc_s, o_s], axis=1)], axis=0)
        mixed = _gmm(mix, w_out, be_l, na_dense, tm=640, tn=1024, out_dtype=F32)
        new_conv_p.append(nc_p)
        new_hgrn_p.append(ns_p)
        new_conv_s.append(nc_s)
        new_hgrn_s.append(ns_s)
        nxt = (vec(norm_pre[l, 1]), sc_f, sh_f)
        if l % 2 == 0:
            j = l // 2
            x, h2 = _resid(x, mixed, vec(norm_post[l, 0]), ga_a, nxt=nxt)
            be_j, _ = _dense_blocks(N_ROWS, 640, j)
            g = _gmm_swiglu(h2, ffn_w1, ffn_w3, be_j, na_dense, tm=640, tf=512)
            f = _gmm(g, ffn_w2, be_j, na_dense, tm=640, tn=512, out_dtype=F32)
        else:
            j = l // 2
            rw = jnp.pad(router_w[j].astype(F32), ((0, 0), (0, LANES - N_EXPERTS)))
            rb = jnp.pad(router_b[j].astype(F32), (0, LANES - N_EXPERTS)).reshape(1, LANES)
            x, h2, logits = _resid(x, mixed, vec(norm_post[l, 0]), ga_a, nxt=nxt, router=(rw, rb), h_dtype=F32)
            f = _moe_ffn(h2, logits, moe_w1[j], moe_w3[j], moe_w2[j])
        if l + 1 < DEPTH:
            nxt = (vec(norm_pre[l + 1, 0]), mod_vec(l + 1, 1), mod_vec(l + 1, 0))
            x, h = _resid(x, f, vec(norm_post[l, 1]), ga_f, nxt=nxt)
        else:
            (x,) = _resid(x, f, vec(norm_post[l, 1]), ga_f)

    y_prompt = x[:N_PROMPT].reshape(BATCH, SEQ, D_MODEL)
    y_sample = x[N_PROMPT:].reshape(DEC_BATCH, 1, D_MODEL)
    return (y_prompt, y_sample, jnp.stack(new_conv_p), jnp.stack(new_hgrn_p),
            jnp.stack(new_conv_s), jnp.stack(new_hgrn_s))
```

```python
import functools

import jax
import jax.numpy as jnp
import numpy as np
from jax import lax
from jax.experimental import pallas as pl
from jax.experimental.pallas import tpu as pltpu

F32 = jnp.float32
BF16 = jnp.bfloat16
I32 = jnp.int32
HIGHEST = lax.Precision.HIGHEST

LANES = 128
SUBLANES = 8
VMEM_LIMIT_BYTES = 56 * 1024 * 1024

D_MODEL = 2048
BATCH = 4
SEQ = 2048
DEPTH = 2
DEC_BATCH = 128
N_PROMPT = BATCH * SEQ
N_ROWS = N_PROMPT + DEC_BATCH
CONV_DIM = 1024
CONV_GROUPS = 16
CONV_W = 3
HG_HEADS = 8
HG_DK = 128
HG_DV = 128
PROJ_WIDTH = 7168
N_EXPERTS = 8
EPS = 1e-6
F_MIN = 1e-6

ROW_TILE = 128
N_TILES = N_ROWS // ROW_TILE
PROMPT_TILES = N_PROMPT // ROW_TILE
TILES_PER_SEQ = SEQ // ROW_TILE
MOE_BLOCK = 256
N_PAIRS = 2 * N_ROWS
N_BLOCKS = (N_PAIRS + N_EXPERTS * (MOE_BLOCK - 1) + MOE_BLOCK - 1) // MOE_BLOCK
P_ROWS = N_BLOCKS * MOE_BLOCK


def _cparams(sem):
    return pltpu.CompilerParams(dimension_semantics=sem, vmem_limit_bytes=VMEM_LIMIT_BYTES)


def _rms(x, g):
    return x * lax.rsqrt(jnp.mean(x * x, axis=-1, keepdims=True) + EPS) * g


def _silu(x):
    return x * jax.nn.sigmoid(x)


def _gmm_kernel(be_ref, nact_ref, a_ref, w_ref, *rest, lhs_silu, has_bias):
    if has_bias:
        b_ref, o_ref, wb_ref = rest
    else:
        o_ref, wb_ref = rest
    i = pl.program_id(1)
    prev = be_ref[jnp.maximum(i - 1, 0)]

    @pl.when((i == 0) | (be_ref[i] != prev))
    def _():
        wb_ref[...] = w_ref[0].astype(BF16)

    @pl.when(i < nact_ref[0])
    def _():
        a = a_ref[...]
        if lhs_silu:
            a = _silu(a)
        acc = jnp.dot(a.astype(BF16), wb_ref[...], preferred_element_type=F32)
        if has_bias:
            acc = acc + b_ref[0]
        o_ref[...] = acc.astype(o_ref.dtype)

    @pl.when(i >= nact_ref[0])
    def _():
        o_ref[...] = jnp.zeros_like(o_ref)


def _gmm(a, w, block_e, nact, *, tm, tn, out_dtype, bias=None, lhs_silu=False):
    m, k = a.shape
    _, _, n = w.shape
    grid = (n // tn, m // tm)
    in_specs = [
        pl.BlockSpec((tm, k), lambda j, i, be, na: (i, 0)),
        pl.BlockSpec((1, k, tn), lambda j, i, be, na: (be[i], 0, j)),
    ]
    args = [a, w]
    if bias is not None:
        in_specs.append(pl.BlockSpec((1, 1, tn), lambda j, i, be, na: (be[i], 0, j)))
        args.append(bias)
    return pl.pallas_call(
        functools.partial(_gmm_kernel, lhs_silu=lhs_silu, has_bias=bias is not None),
        grid_spec=pltpu.PrefetchScalarGridSpec(
            num_scalar_prefetch=2,
            grid=grid,
            in_specs=in_specs,
            out_specs=pl.BlockSpec((tm, tn), lambda j, i, be, na: (i, j)),
            scratch_shapes=[pltpu.VMEM((k, tn), BF16)],
        ),
        out_shape=jax.ShapeDtypeStruct((m, n), out_dtype),
        compiler_params=_cparams(("arbitrary", "arbitrary")),
    )(block_e, nact, *args)


def _gmm_swiglu_kernel(be_ref, nact_ref, a_ref, w1_ref, w3_ref, o_ref, w1b_ref, w3b_ref):
    i = pl.program_id(1)
    prev = be_ref[jnp.maximum(i - 1, 0)]

    @pl.when((i == 0) | (be_ref[i] != prev))
    def _():
        w1b_ref[...] = w1_ref[0].astype(BF16)
        w3b_ref[...] = w3_ref[0].astype(BF16)

    @pl.when(i < nact_ref[0])
    def _():
        a = a_ref[...]
        u = jnp.dot(a, w1b_ref[...], preferred_element_type=F32)
        g = jnp.dot(a, w3b_ref[...], preferred_element_type=F32)
        o_ref[...] = (_silu(u) * g).astype(o_ref.dtype)

    @pl.when(i >= nact_ref[0])
    def _():
        o_ref[...] = jnp.zeros_like(o_ref)


def _gmm_swiglu(a, w1, w3, block_e, nact, *, tm, tf):
    m, k = a.shape
    _, _, f = w1.shape
    grid = (f // tf, m // tm)
    wspec = pl.BlockSpec((1, k, tf), lambda j, i, be, na: (be[i], 0, j))
    return pl.pallas_call(
        _gmm_swiglu_kernel,
        grid_spec=pltpu.PrefetchScalarGridSpec(
            num_scalar_prefetch=2,
            grid=grid,
            in_specs=[pl.BlockSpec((tm, k), lambda j, i, be, na: (i, 0)), wspec, wspec],
            out_specs=pl.BlockSpec((tm, tf), lambda j, i, be, na: (i, j)),
            scratch_shapes=[pltpu.VMEM((k, tf), BF16), pltpu.VMEM((k, tf), BF16)],
        ),
        out_shape=jax.ShapeDtypeStruct((m, f), BF16),
        compiler_params=_cparams(("arbitrary", "arbitrary")),
    )(block_e, nact, a, w1, w3)


def _pick(i, p_ref, s_ref):
    return jnp.where(i < PROMPT_TILES, p_ref[0], s_ref[...])


def _prenorm_kernel(xp_ref, xs_ref, g_ref, scp_ref, scs_ref, shp_ref, shs_ref, h_ref):
    i = pl.program_id(0)
    x = jnp.where(i < PROMPT_TILES, xp_ref[...], xs_ref[...])
    sc = _pick(i, scp_ref, scs_ref)
    sh = _pick(i, shp_ref, shs_ref)
    h_ref[...] = (_rms(x, g_ref[...]) * (1.0 + sc) + sh).astype(h_ref.dtype)


def _resid_kernel(*refs, x_split, y_split, with_next, with_router, h_dtype):
    it = iter(refs)
    if x_split:
        xp_ref, xs_ref = next(it), next(it)
    else:
        x_ref = next(it)
    f_ref, gpost_ref, gap_ref, gas_ref = [next(it) for _ in range(4)]
    if with_next:
        gpre_ref, scp_ref, scs_ref, shp_ref, shs_ref = [next(it) for _ in range(5)]
    if with_router:
        rw_ref, rb_ref = next(it), next(it)
    if y_split:
        yp_ref, ys_ref = next(it), next(it)
    else:
        xo_ref = next(it)
    if with_next:
        h_ref = next(it)
    if with_router:
        lg_ref = next(it)
    i = pl.program_id(0)
    ga = _pick(i, gap_ref, gas_ref)
    x = jnp.where(i < PROMPT_TILES, xp_ref[...], xs_ref[...]) if x_split else x_ref[...]
    x = x + ga * _rms(f_ref[...], gpost_ref[...])
    if y_split:
        @pl.when(i < PROMPT_TILES)
        def _():
            yp_ref[...] = x

        @pl.when(i == PROMPT_TILES)
        def _():
            ys_ref[...] = x
    else:
        xo_ref[...] = x
    if with_next:
        sc = _pick(i, scp_ref, scs_ref)
        sh = _pick(i, shp_ref, shs_ref)
        h = _rms(x, gpre_ref[...]) * (1.0 + sc) + sh
        h_ref[...] = h.astype(h_dtype)
        if with_router:
            lg_ref[...] = jnp.dot(h, rw_ref[...], precision=HIGHEST,
                                  preferred_element_type=F32) + rb_ref[...]


def _row_spec(width):
    return pl.BlockSpec((ROW_TILE, width), lambda i: (i, 0))


def _vec_spec():
    return pl.BlockSpec((1, D_MODEL), lambda i: (0, 0))


def _modp_spec():
    return pl.BlockSpec((1, 1, D_MODEL), lambda i: (jnp.minimum(i // TILES_PER_SEQ, BATCH - 1), 0, 0))


def _mods_spec():
    return pl.BlockSpec((DEC_BATCH, D_MODEL), lambda i: (0, 0))


def _prompt_rows_spec():
    return pl.BlockSpec((ROW_TILE, D_MODEL), lambda i: (jnp.minimum(i, PROMPT_TILES - 1), 0))


def _prenorm(xp, xs, g, sc, sh):
    return pl.pallas_call(
        _prenorm_kernel,
        grid=(N_TILES,),
        in_specs=[_prompt_rows_spec(), _mods_spec(), _vec_spec(), _modp_spec(), _mods_spec(), _modp_spec(),
                  _mods_spec()],
        out_specs=_row_spec(D_MODEL),
        out_shape=jax.ShapeDtypeStruct((N_ROWS, D_MODEL), BF16),
        compiler_params=_cparams(("arbitrary",)),
    )(xp, xs, g, sc[0], sc[1], sh[0], sh[1])


def _resid(x, f, gpost, ga, nxt=None, router=None, h_dtype=BF16, y_split=False):
    x_split = isinstance(x, tuple)
    if x_split:
        args = [x[0], x[1]]
        in_specs = [_prompt_rows_spec(), _mods_spec()]
    else:
        args = [x]
        in_specs = [_row_spec(D_MODEL)]
    args += [f, gpost, ga[0], ga[1]]
    in_specs += [_row_spec(D_MODEL), _vec_spec(), _modp_spec(), _mods_spec()]
    if y_split:
        out_shape = [jax.ShapeDtypeStruct((N_PROMPT, D_MODEL), F32), jax.ShapeDtypeStruct((DEC_BATCH, D_MODEL), F32)]
        out_specs = [_prompt_rows_spec(), _mods_spec()]
    else:
        out_shape = [jax.ShapeDtypeStruct((N_ROWS, D_MODEL), F32)]
        out_specs = [_row_spec(D_MODEL)]
    if nxt is not None:
        gpre, sc, sh = nxt
        args += [gpre, sc[0], sc[1], sh[0], sh[1]]
        in_specs += [_vec_spec(), _modp_spec(), _mods_spec(), _modp_spec(), _mods_spec()]
        out_shape.append(jax.ShapeDtypeStruct((N_ROWS, D_MODEL), h_dtype))
        out_specs.append(_row_spec(D_MODEL))
    if router is not None:
        rw, rb = router
        args += [rw, rb]
        in_specs += [pl.BlockSpec((D_MODEL, LANES), lambda i: (0, 0)), pl.BlockSpec((1, LANES), lambda i: (0, 0))]
        out_shape.append(jax.ShapeDtypeStruct((N_ROWS, LANES), F32))
        out_specs.append(_row_spec(LANES))
    return pl.pallas_call(
        functools.partial(_resid_kernel, x_split=x_split, y_split=y_split, with_next=nxt is not None,
                          with_router=router is not None, h_dtype=h_dtype),
        grid=(N_TILES,),
        in_specs=in_specs,
        out_specs=out_specs,
        out_shape=out_shape,
        compiler_params=_cparams(("arbitrary",)),
    )(*args)


def _group_rms(y, gnorm):
    width = CONV_DIM // CONV_GROUPS
    c = lax.broadcasted_iota(I32, (CONV_DIM, LANES), 0) // width
    g = lax.broadcasted_iota(I32, (CONV_DIM, LANES), 1)
    sel = (c == g).astype(F32)
    ct = lax.broadcasted_iota(I32, (LANES, CONV_DIM), 1) // width
    gt = lax.broadcasted_iota(I32, (LANES, CONV_DIM), 0)
    sel_t = (ct == gt).astype(F32)
    ss = jnp.dot(y * y, sel, precision=HIGHEST, preferred_element_type=F32)
    inv = lax.rsqrt(ss * (1.0 / width) + EPS)
    return y * jnp.dot(inv, sel_t, precision=HIGHEST, preferred_element_type=F32) * gnorm


def _conv_prompt_kernel(hc_ref, bg_ref, cg_ref, hch_ref, cgh_ref, cw_ref, gn_ref, mix_ref, y_ref, nc_ref, *, tt):
    del mix_ref
    t = pl.program_id(1)
    u = cg_ref[...] * hc_ref[...]
    halo = jnp.where(t == 0, 0.0, cgh_ref[...] * hch_ref[...])
    h1 = halo[SUBLANES - 1:SUBLANES]
    h2 = halo[SUBLANES - 2:SUBLANES - 1]
    row = lax.broadcasted_iota(I32, u.shape, 0)
    u1 = jnp.where(row == 0, h1, pltpu.roll(u, 1, 0))
    u2 = jnp.where(row == 0, h2, jnp.where(row == 1, h1, pltpu.roll(u, 2, 0)))
    conv = cw_ref[0:1] * u2 + cw_ref[1:2] * u1 + cw_ref[2:3] * u
    y_ref[...] = _group_rms(bg_ref[...] * conv, gn_ref[...]).astype(y_ref.dtype)

    @pl.when(t == pl.num_programs(1) - 1)
    def _():
        nc_ref[0] = u[tt - (CONV_W - 1):]


def _conv_prompt(proj, conv_w, conv_norm, mix, *, tt=256):
    nt = SEQ // tt
    cblk = lambda c: pl.BlockSpec((tt, CONV_DIM), lambda b, t: (b * nt + t, c))
    hblk = lambda c: pl.BlockSpec(
        (SUBLANES, CONV_DIM), lambda b, t: (jnp.maximum((b * nt + t) * (tt // SUBLANES) - 1, 0), c))
    return pl.pallas_call(
        functools.partial(_conv_prompt_kernel, tt=tt),
        grid=(BATCH, nt),
        in_specs=[cblk(0), cblk(1), cblk(2), hblk(0), hblk(2),
                  pl.BlockSpec((CONV_W, CONV_DIM), lambda b, t: (0, 0)),
                  pl.BlockSpec((1, CONV_DIM), lambda b, t: (0, 0)),
                  pl.BlockSpec(memory_space=pl.ANY)],
        out_specs=[pl.BlockSpec((tt, CONV_DIM), lambda b, t: (b * nt + t, 0)),
                   pl.BlockSpec((1, CONV_W - 1, CONV_DIM), lambda b, t: (b, 0, 0))],
        out_shape=[jax.ShapeDtypeStruct(mix.shape, mix.dtype),
                   jax.ShapeDtypeStruct((BATCH, CONV_W - 1, CONV_DIM), F32)],
        input_output_aliases={7: 0},
        compiler_params=_cparams(("arbitrary", "arbitrary")),
    )(proj, proj, proj, proj, proj, conv_w, conv_norm, mix)


HG_LEVELS = tuple(1 << i for i in range(ROW_TILE.bit_length() - 1))


def _gates(z, lb):
    sg = jax.nn.sigmoid(z)
    f = lb + (1.0 - lb) * sg
    return jnp.log(jnp.maximum(f, F_MIN)), (1.0 - lb) * (1.0 - sg)


def _level_table():
    t = np.arange(ROW_TILE)[:, None]
    s = np.arange(ROW_TILE)[None, :]
    x = np.maximum(t ^ s, 1)
    lvl = np.floor(np.log2(x)).astype(np.int32)
    lvl = np.where(t == s, len(HG_LEVELS), np.where(s < t, lvl, -1))
    return jnp.asarray(lvl, I32)


def _nt_dot(x, y):
    return lax.dot_general(x, y, (((1,), (1,)), ((), ())), preferred_element_type=F32)


def _hgrn_tile(q, z, v, lb, st, tri, lvl):
    logf, k = _gates(z, lb)
    hi = logf.astype(BF16)
    rem = logf - hi.astype(F32)
    mid = rem.astype(BF16)
    lo = (rem - mid.astype(F32)).astype(BF16)
    parts = jnp.dot(tri, jnp.concatenate([hi, mid, lo], axis=1), preferred_element_type=F32)
    a = parts[:, :HG_DK] + parts[:, HG_DK:2 * HG_DK] + parts[:, 2 * HG_DK:]
    row = lax.broadcasted_iota(I32, (ROW_TILE, HG_DK), 0)
    a8 = a.reshape(ROW_TILE // SUBLANES, SUBLANES, HG_DK)
    sub8 = lax.broadcasted_iota(I32, a8.shape, 1)
    scores = jnp.where(lvl == len(HG_LEVELS), _nt_dot(q.astype(BF16), k.astype(BF16)), 0.0)
    for li, c in enumerate(HG_LEVELS):
        if c == 1:
            d = jnp.where((row & 1) == 1, logf, 0.0)
        elif c == 2:
            anchor = jnp.where(sub8 < 4, a8[:, 1:2, :], a8[:, 5:6, :])
            d = (a8 - anchor).reshape(ROW_TILE, HG_DK)
        else:
            ab = a.reshape(ROW_TILE // (2 * c), 2 * c, HG_DK)
            d = (ab - ab[:, c - 1:c, :]).reshape(ROW_TILE, HG_DK)
        x = (jnp.where((row & c) != 0, q, k) * jnp.exp(-jnp.abs(d))).astype(BF16)
        scores = jnp.where(lvl == li, _nt_dot(x, x), scores)
    vb = v.astype(BF16)
    a_last = a[ROW_TILE - 1:ROW_TILE, :]
    o = jnp.dot(scores.astype(BF16), vb, preferred_element_type=F32)
    o = o + _nt_dot((q * jnp.exp(a)).astype(BF16), st.astype(BF16))
    kt = (k * jnp.exp(a_last - a)).astype(BF16)
    st = st * jnp.exp(a_last) + lax.dot_general(vb, kt, (((0,), (0,)), ((), ())),
                                                preferred_element_type=F32)
    return o, st


def _hgrn_prompt_kernel(q_ref, z_ref, v_ref, og_ref, lb_ref, gn_ref, tri_ref, lvl_ref, mix_ref, o_ref, s_ref,
                        st_ref, *, tt):
    del mix_ref
    t = pl.program_id(1)

    @pl.when(t == 0)
    def _():
        st_ref[...] = jnp.zeros_like(st_ref)

    def body(j, carry):
        rows = pl.ds(pl.multiple_of(j * ROW_TILE, ROW_TILE), ROW_TILE)
        for h in range(HG_HEADS):
            cols = slice(h * LANES, (h + 1) * LANES)
            o, st = _hgrn_tile(q_ref[rows, cols], z_ref[rows, cols], v_ref[rows, cols], lb_ref[:, cols],
                               st_ref[h], tri_ref[...], lvl_ref[...])
            st_ref[h] = st
            o_ref[rows, cols] = (_rms(o, gn_ref[:, cols]) * _silu(og_ref[rows, cols])).astype(o_ref.dtype)
        return carry

    lax.fori_loop(0, tt // ROW_TILE, body, 0)

    @pl.when(t == pl.num_programs(1) - 1)
    def _():
        for h in range(HG_HEADS):
            s_ref[0, h] = st_ref[h].T


def _hgrn_prompt(proj, lb, hgrn_norm, mix, *, tt=256):
    nt = SEQ // tt
    width = HG_HEADS * LANES
    col0 = 3 * CONV_DIM // width
    blk = lambda part: pl.BlockSpec((tt, width), lambda b, t: (b * nt + t, col0 + part))
    hvec = pl.BlockSpec((1, width), lambda b, t: (0, 0))
    const = pl.BlockSpec((ROW_TILE, ROW_TILE), lambda b, t: (0, 0))
    tri =jnp.asarray(np.tril(np.ones((ROW_TILE, ROW_TILE), np.float32)), BF16)
    return pl.pallas_call(
        functools.partial(_hgrn_prompt_kernel, tt=tt),
        grid=(BATCH, nt),
        in_specs=[blk(0), blk(1), blk(2), blk(3), hvec, hvec, const, const, pl.BlockSpec(memory_space=pl.ANY)],
        out_specs=[pl.BlockSpec((tt, width), lambda b, t: (b * nt + t, 1)),
                   pl.BlockSpec((1, HG_HEADS, HG_DK, HG_DV), lambda b, t: (b, 0, 0, 0))],
        out_shape=[jax.ShapeDtypeStruct(mix.shape, mix.dtype),
                   jax.ShapeDtypeStruct((BATCH, HG_HEADS, HG_DK, HG_DV), F32)],
        scratch_shapes=[pltpu.VMEM((HG_HEADS, HG_DV, HG_DK), F32)],
        input_output_aliases={8: 0},
        compiler_params=_cparams(("arbitrary", "arbitrary")),
    )(proj, proj, proj, proj, lb.reshape(1, width), hgrn_norm.reshape(1, width), tri, _level_table(), mix)


def _conv_sample_kernel(hc_ref, bg_ref, cg_ref, cs_ref, cw_ref, gn_ref, mix_ref, y_ref, nc_ref):
    del mix_ref
    u = cg_ref[...] * hc_ref[...]
    s0 = cs_ref[:, 0, :]
    s1 = cs_ref[:, 1, :]
    conv = cw_ref[0:1] * s0 + cw_ref[1:2] * s1 + cw_ref[2:3] * u
    y_ref[...] = _group_rms(bg_ref[...] * conv, gn_ref[...]).astype(y_ref.dtype)
    nc_ref[:, 0, :] = s1
    nc_ref[:, 1, :] = u


def _conv_sample(proj, conv_state, conv_w, conv_norm, mix):
    rb = PROMPT_TILES
    cblk = lambda c: pl.BlockSpec((DEC_BATCH, CONV_DIM), lambda i: (rb, c))
    return pl.pallas_call(
        _conv_sample_kernel,
        grid=(1,),
        in_specs=[cblk(0), cblk(1), cblk(2),
                  pl.BlockSpec((DEC_BATCH, CONV_W - 1, CONV_DIM), lambda i: (0, 0, 0)),
                  pl.BlockSpec((CONV_W, CONV_DIM), lambda i: (0, 0)),
                  pl.BlockSpec((1, CONV_DIM), lambda i: (0, 0)),
                  pl.BlockSpec(memory_space=pl.ANY)],
        out_specs=[pl.BlockSpec((DEC_BATCH, CONV_DIM), lambda i: (rb, 0)),
                   pl.BlockSpec((DEC_BATCH, CONV_W - 1, CONV_DIM), lambda i: (0, 0, 0))],
        out_shape=[jax.ShapeDtypeStruct(mix.shape, mix.dtype),
                   jax.ShapeDtypeStruct((DEC_BATCH, CONV_W - 1, CONV_DIM), F32)],
        input_output_aliases={6: 0},
        compiler_params=_cparams(("arbitrary",)),
    )(proj, proj, proj, conv_state, conv_w, conv_norm, mix)


def _hgrn_sample_kernel(q_ref, z_ref, v_ref, og_ref, lb_ref, gn_ref, s_ref, *rest, bg, slab):
    o_ref, so_ref, osc_ref = rest[-3:]
    s_ref = s_ref.at[0]
    for other in range(so_ref.shape[0]):
        if other != slab:
            so_ref[other] = jnp.zeros(so_ref.shape[1:], so_ref.dtype)
    so_ref = so_ref.at[slab]
    g = pl.program_id(1)
    lb = lb_ref[0]
    logf, k = _gates(z_ref[...], lb)
    f = jnp.exp(logf)
    shift = (DEC_BATCH - g * bg) % DEC_BATCH
    ft = pltpu.roll(f.T, shift, 1)
    kt = pltpu.roll(k.T, shift, 1)
    qt = pltpu.roll(q_ref[...].T, shift, 1)
    rows = pl.ds(pl.multiple_of(g * bg, bg), bg)
    v = v_ref[rows, :]
    for j in range(bg):
        s_new = ft[:, j:j + 1] * s_ref[j, 0] + kt[:, j:j + 1] * v[j:j + 1, :]
        so_ref[j, 0] = s_new
        osc_ref[j:j + 1, :] = jnp.sum(qt[:, j:j + 1] * s_new, axis=0, keepdims=True)
    o = osc_ref[...]
    o_ref[...] = (_rms(o, gn_ref[0]) * _silu(og_ref[rows, :])).astype(o_ref.dtype)


def _hgrn_sample(proj, state_all, l, lb, hgrn_norm, mix, new_state_all=None, *, bg=16):
    rb = PROMPT_TILES
    col0 = 3 * CONV_DIM // LANES
    blk = lambda part: pl.BlockSpec((DEC_BATCH, LANES), lambda h, g: (rb, col0 + part * HG_HEADS + h))
    hvec = pl.BlockSpec((1, 1, LANES), lambda h, g: (h, 0, 0))
    sblk = pl.BlockSpec((1, bg, 1, HG_DK, HG_DV), lambda h, g: (l, g, h, 0, 0))
    any_spec = pl.BlockSpec(memory_space=pl.ANY)
    args = [proj, proj, proj, proj, lb.reshape(HG_HEADS, 1, HG_DK), hgrn_norm.reshape(HG_HEADS, 1, HG_DV),
            state_all, mix]
    in_specs = [blk(0), blk(1), blk(2), blk(3), hvec, hvec, sblk, any_spec]
    aliases = {7: 0}
    if new_state_all is not None:
        args.append(new_state_all)
        in_specs.append(any_spec)
        aliases[8] = 1
        so_blk, slab = sblk, 0
    else:
        n_slabs = state_all.shape[0]
        so_blk = pl.BlockSpec((n_slabs, bg, 1, HG_DK, HG_DV), lambda h, g: (0, g, h, 0, 0))
        slab = l
    return pl.pallas_call(
        functools.partial(_hgrn_sample_kernel, bg=bg, slab=slab),
        grid=(HG_HEADS, DEC_BATCH // bg),
        in_specs=in_specs,
        out_specs=[pl.BlockSpec((bg, LANES), lambda h, g: (N_PROMPT // bg + g, CONV_DIM // LANES + h)), so_blk],
        out_shape=[jax.ShapeDtypeStruct(mix.shape, mix.dtype),
                   jax.ShapeDtypeStruct(state_all.shape, F32)],
        scratch_shapes=[pltpu.VMEM((bg, HG_DV), F32)],
        input_output_aliases=aliases,
        compiler_params=_cparams(("arbitrary", "arbitrary")),
    )(*args)


def _route_kernel(lg_ref, ri_ref, gate_ref, cnt_ref, carry_ref):
    i = pl.program_id(0)

    @pl.when(i == 0)
    def _():
        carry_ref[...] = jnp.zeros_like(carry_ref)

    lane = lax.broadcasted_iota(I32, (ROW_TILE, LANES), 1)
    lanef = lane.astype(F32)
    lg = jnp.where(lane < N_EXPERTS, lg_ref[...], -jnp.inf)
    m1 = jnp.max(lg, axis=-1, keepdims=True)
    i1 = jnp.min(jnp.where(lg == m1, lanef, float(LANES)), axis=-1, keepdims=True).astype(I32)
    lg2 = jnp.where(lane == i1, -jnp.inf, lg)
    m2 = jnp.max(lg2, axis=-1, keepdims=True)
    i2 = jnp.min(jnp.where(lg2 == m2, lanef, float(LANES)), axis=-1, keepdims=True).astype(I32)
    e = jnp.exp(m2 - m1)
    g1 = 1.0 / (1.0 + e)
    g2 = e / (1.0 + e)
    hot1 = lane == i1
    hot2 = lane == i2
    hot = (hot1 | hot2).astype(BF16)
    r = lax.broadcasted_iota(I32, (ROW_TILE, ROW_TILE), 0)
    c = lax.broadcasted_iota(I32, (ROW_TILE, ROW_TILE), 1)
    before = (c < r).astype(BF16)
    tot = jnp.dot(before, hot, preferred_element_type=F32) + carry_ref[...]
    r1 = jnp.sum(jnp.where(hot1, tot, 0.0), axis=-1, keepdims=True).astype(I32)
    r2 = jnp.sum(jnp.where(hot2, tot, 0.0), axis=-1, keepdims=True).astype(I32)
    ri_ref[...] = jnp.where(lane == 0, i1, jnp.where(lane == 1, i2, jnp.where(lane == 2, r1,
                            jnp.where(lane == 3, r2, 0))))
    gate_ref[...] = jnp.where(lane == 0, g1, jnp.where(lane == 1, g2, 0.0))
    carry_ref[...] += jnp.sum(hot.astype(F32), axis=0, keepdims=True)
    cnt_ref[...] = carry_ref[...]


def _route(logits):
    return pl.pallas_call(
        _route_kernel,
        grid=(N_TILES,),
        in_specs=[_row_spec(LANES)],
        out_specs=[_row_spec(LANES), _row_spec(LANES), pl.BlockSpec((1, LANES), lambda i: (0, 0))],
        out_shape=[jax.ShapeDtypeStruct((N_ROWS, LANES), I32),
                   jax.ShapeDtypeStruct((N_ROWS, LANES), F32),
                   jax.ShapeDtypeStruct((1, LANES), F32)],
        scratch_shapes=[pltpu.VMEM((1, LANES), F32)],
        compiler_params=_cparams(("arbitrary",)),
    )(logits)


def _gather_kernel(d1_ref, d2_ref, nact_ref, h_ref, xb_ref, tok_ref, buf_ref, sem):
    i = pl.program_id(0)

    @pl.when(i == 0)
    def _():
        def clear(p, carry):
            tok_ref[p] = 0
            return carry
        lax.fori_loop(0, P_ROWS, clear, 0)

        def scatter(t, carry):
            tok_ref[d1_ref[t]] = t
            tok_ref[d2_ref[t]] = t
            return carry
        lax.fori_loop(0, N_ROWS, scatter, 0)

    def row_copy(r):
        tok = tok_ref[i * MOE_BLOCK + r]
        return pltpu.make_async_copy(h_ref.at[pl.ds(tok, 1), :], buf_ref.at[pl.ds(r, 1), :], sem)

    @pl.when(i < nact_ref[0])
    def _():
        def start(r, carry):
            row_copy(r).start()
            return carry
        lax.fori_loop(0, MOE_BLOCK, start, 0)

        def wait(r, carry):
            row_copy(r).wait()
            return carry
        lax.fori_loop(0, MOE_BLOCK, wait, 0)
        xb_ref[...] = buf_ref[...].astype(xb_ref.dtype)

    @pl.when(i >= nact_ref[0])
    def _():
        xb_ref[...] = jnp.zeros_like(xb_ref)


def _gather(h, dest1, dest2, nact):
    return pl.pallas_call(
        _gather_kernel,
        grid_spec=pltpu.PrefetchScalarGridSpec(
            num_scalar_prefetch=3,
            grid=(N_BLOCKS,),
            in_specs=[pl.BlockSpec(memory_space=pl.ANY)],
            out_specs=pl.BlockSpec((MOE_BLOCK, D_MODEL), lambda i, d1, d2, na: (i, 0)),
            scratch_shapes=[pltpu.SMEM((P_ROWS,), I32), pltpu.VMEM((MOE_BLOCK, D_MODEL), F32),
                            pltpu.SemaphoreType.DMA(())],
        ),
        out_shape=jax.ShapeDtypeStruct((P_ROWS, D_MODEL), BF16),
        compiler_params=_cparams(("arbitrary",)),
    )(dest1, dest2, nact, h)


def _combine_kernel(d1_ref, d2_ref, gate_ref, yb_ref, f_ref, b1_ref, b2_ref, sem):
    i = pl.program_id(0)

    def copies(r):
        t = i * ROW_TILE + r
        return (pltpu.make_async_copy(yb_ref.at[pl.ds(d1_ref[t], 1), :], b1_ref.at[pl.ds(r, 1), :], sem.at[0]),
                pltpu.make_async_copy(yb_ref.at[pl.ds(d2_ref[t], 1), :], b2_ref.at[pl.ds(r, 1), :], sem.at[1]))

    def start(r, carry):
        c1, c2 = copies(r)
        c1.start()
        c2.start()
        return carry
    lax.fori_loop(0, ROW_TILE, start, 0)

    def wait(r, carry):
        c1, c2 = copies(r)
        c1.wait()
        c2.wait()
        return carry
    lax.fori_loop(0, ROW_TILE, wait, 0)
    gate = gate_ref[...]
    f_ref[...] = gate[:, 0:1] * b1_ref[...] + gate[:, 1:2] * b2_ref[...]


def _combine(yb, gates, dest1, dest2):
    return pl.pallas_call(
        _combine_kernel,
        grid_spec=pltpu.PrefetchScalarGridSpec(
            num_scalar_prefetch=2,
            grid=(N_TILES,),
            in_specs=[pl.BlockSpec((ROW_TILE, LANES), lambda i, d1, d2: (i, 0)),
                      pl.BlockSpec(memory_space=pl.ANY)],
            out_specs=pl.BlockSpec((ROW_TILE, D_MODEL), lambda i, d1, d2: (i, 0)),
            scratch_shapes=[pltpu.VMEM((ROW_TILE, D_MODEL), F32), pltpu.VMEM((ROW_TILE, D_MODEL), F32),
                            pltpu.SemaphoreType.DMA((2,))],
        ),
        out_shape=jax.ShapeDtypeStruct((N_ROWS, D_MODEL), F32),
        compiler_params=_cparams(("arbitrary",)),
    )(dest1, dest2, gates, yb)


def _moe_ffn(h, logits, w1, w3, w2):
    ri, gates, counts = _route(logits)
    counts = counts[0, :N_EXPERTS].astype(I32)
    padded = (counts + MOE_BLOCK - 1) // MOE_BLOCK * MOE_BLOCK
    pends = jnp.cumsum(padded)
    pstarts = pends - padded
    dest1 = pstarts[ri[:, 0]] + ri[:, 2]
    dest2 = pstarts[ri[:, 1]] + ri[:, 3]
    nact = (pends[-1:] // MOE_BLOCK).astype(I32)
    block_start = jnp.arange(N_BLOCKS, dtype=I32) * MOE_BLOCK
    block_e = jnp.minimum(jnp.sum((pends[None, :] <= block_start[:, None]).astype(I32), axis=1), N_EXPERTS - 1)
    xb = _gather(h, dest1, dest2, nact)
    gb = _gmm_swiglu(xb, w1, w3, block_e, nact, tm=MOE_BLOCK, tf=1024)
    yb = _gmm(gb, w2, block_e, nact, tm=MOE_BLOCK, tn=512, out_dtype=F32)
    return _combine(yb, gates, dest1, dest2)


def _dense_blocks(m, tm, e):
    nb = m // tm
    return jnp.full((nb,), e, I32), jnp.full((1,), nb, I32)


def kernel(x_prompt, x_sample, state_conv, state_hgrn, c_prompt, c_sample, norm_pre, norm_post, w_mod, b_mod, w_in, conv_w, conv_norm, lb_logits, hgrn_norm, w_out, ffn_w1, ffn_w3, ffn_w2, router_w, router_b, moe_w1, moe_w3, moe_w2):
    p = jax.nn.softmax(lb_logits.astype(F32), axis=0)
    lb_all = jnp.cumsum(p, axis=0) - p[0:1]

    n_cond = BATCH + DEC_BATCH
    cond_rows = (n_cond + SUBLANES - 1) // SUBLANES * SUBLANES
    c_all = jnp.concatenate([c_prompt, c_sample, jnp.zeros((cond_rows - n_cond, D_MODEL), F32)], axis=0)
    mod = _gmm(jnp.concatenate([c_all] * DEPTH, axis=0), w_mod, jnp.arange(DEPTH, dtype=I32),
               jnp.full((1,), DEPTH, I32), tm=cond_rows, tn=1024, out_dtype=F32,
               bias=b_mod.reshape(DEPTH, 1, 6 * D_MODEL), lhs_silu=True)

    def mod_vec(l, j):
        rows = mod[l * cond_rows:l * cond_rows + n_cond, j * D_MODEL:(j + 1) * D_MODEL]
        return rows[:BATCH].reshape(BATCH, 1, D_MODEL), rows[BATCH:]

    x = (x_prompt.reshape(N_PROMPT, D_MODEL), x_sample.reshape(DEC_BATCH, D_MODEL))
    vec = lambda a: a.reshape(1, -1)

    h = _prenorm(x[0], x[1], vec(norm_pre[0, 0]), mod_vec(0, 1), mod_vec(0, 0))
    new_conv_p, new_hgrn_p, new_conv_s, new_hgrn_s = [], [], [], None
    for l in range(DEPTH):
        sh_f, sc_f, ga_f = mod_vec(l, 3), mod_vec(l, 4), mod_vec(l, 5)
        ga_a = mod_vec(l, 2)
        be_l, na_dense = _dense_blocks(N_ROWS, 640, l)
        proj = _gmm(h, w_in, be_l, na_dense, tm=640, tn=1024, out_dtype=F32)
        mix = jnp.zeros((N_ROWS, 2 * CONV_DIM), BF16)
        mix, nc_p = _conv_prompt(proj, conv_w[l], vec(conv_norm[l]), mix)
        mix, ns_p = _hgrn_prompt(proj, lb_all[l], hgrn_norm[l], mix)
        mix, nc_s = _conv_sample(proj, state_conv[l], conv_w[l], vec(conv_norm[l]), mix)
        mix, new_hgrn_s = _hgrn_sample(proj, state_hgrn, l, lb_all[l], hgrn_norm[l], mix, new_hgrn_s)
        mixed = _gmm(mix, w_out, be_l, na_dense, tm=640, tn=1024, out_dtype=F32)
        new_conv_p.append(nc_p)
        new_hgrn_p.append(ns_p)
        new_conv_s.append(nc_s)
        nxt = (vec(norm_pre[l, 1]), sc_f, sh_f)
        if l % 2 == 0:
            j = l // 2
            x, h2 = _resid(x, mixed, vec(norm_post[l, 0]), ga_a, nxt=nxt)
            be_j, _ = _dense_blocks(N_ROWS, 640, j)
            g = _gmm_swiglu(h2, ffn_w1, ffn_w3, be_j, na_dense, tm=640, tf=512)
            f = _gmm(g, ffn_w2, be_j, na_dense, tm=640, tn=512, out_dtype=F32)
        else:
            j = l // 2
            rw = jnp.pad(router_w[j].astype(F32), ((0, 0), (0, LANES - N_EXPERTS)))
            rb = jnp.pad(router_b[j].astype(F32), (0, LANES - N_EXPERTS)).reshape(1, LANES)
            x, h2, logits = _resid(x, mixed, vec(norm_post[l, 0]), ga_a, nxt=nxt, router=(rw, rb), h_dtype=F32)
            f = _moe_ffn(h2, logits, moe_w1[j], moe_w3[j], moe_w2[j])
        if l + 1 < DEPTH:
            nxt = (vec(norm_pre[l + 1, 0]), mod_vec(l + 1, 1), mod_vec(l + 1, 0))
            x, h = _resid(x, f, vec(norm_post[l, 1]), ga_f, nxt=nxt)
        else:
            y_prompt, y_sample = _resid(x, f, vec(norm_post[l, 1]), ga_f, y_split=True)

    y_prompt = y_prompt.reshape(BATCH, SEQ, D_MODEL)
    y_sample = y_sample.reshape(DEC_BATCH, 1, D_MODEL)
    return (y_prompt, y_sample, jnp.stack(new_conv_p), jnp.stack(new_hgrn_p),
            jnp.stack(new_conv_s), new_hgrn_s)
```

```python
import functools

import jax
import jax.numpy as jnp
import numpy as np
from jax import lax
from jax.experimental import pallas as pl
from jax.experimental.pallas import tpu as pltpu

F32 = jnp.float32
BF16 = jnp.bfloat16
I32 = jnp.int32
HIGHEST = lax.Precision.HIGHEST

LANES = 128
SUBLANES = 8
VMEM_LIMIT_BYTES = 56 * 1024 * 1024

D_MODEL = 2048
BATCH = 4
SEQ = 2048
DEPTH = 2
DEC_BATCH = 128
N_PROMPT = BATCH * SEQ
N_ROWS = N_PROMPT + DEC_BATCH
CONV_DIM = 1024
CONV_GROUPS = 16
CONV_W = 3
HG_HEADS = 8
HG_DK = 128
HG_DV = 128
PROJ_WIDTH = 7168
N_EXPERTS = 8
EPS = 1e-6
F_MIN = 1e-6

ROW_TILE = 128
N_TILES = N_ROWS // ROW_TILE
PROMPT_TILES = N_PROMPT // ROW_TILE
TILES_PER_SEQ = SEQ // ROW_TILE
MOE_BLOCK = 256
N_PAIRS = 2 * N_ROWS
N_BLOCKS = (N_PAIRS + N_EXPERTS * (MOE_BLOCK - 1) + MOE_BLOCK - 1) // MOE_BLOCK
P_ROWS = N_BLOCKS * MOE_BLOCK


def _cparams(sem):
    return pltpu.CompilerParams(dimension_semantics=sem, vmem_limit_bytes=VMEM_LIMIT_BYTES)


def _rms(x, g):
    return x * lax.rsqrt(jnp.mean(x * x, axis=-1, keepdims=True) + EPS) * g


def _silu(x):
    return x * jax.nn.sigmoid(x)


def _gmm_kernel(be_ref, nact_ref, a_ref, w_ref, *rest, lhs_silu, has_bias):
    if has_bias:
        b_ref, o_ref, wb_ref = rest
    else:
        o_ref, wb_ref = rest
    i = pl.program_id(1)
    prev = be_ref[jnp.maximum(i - 1, 0)]

    @pl.when((i == 0) | (be_ref[i] != prev))
    def _():
        wb_ref[...] = w_ref[0].astype(BF16)

    @pl.when(i < nact_ref[0])
    def _():
        a = a_ref[...]
        if lhs_silu:
            a = _silu(a)
        acc = jnp.dot(a.astype(BF16), wb_ref[...], preferred_element_type=F32)
        if has_bias:
            acc = acc + b_ref[0]
        o_ref[...] = acc.astype(o_ref.dtype)

    @pl.when(i >= nact_ref[0])
    def _():
        o_ref[...] = jnp.zeros_like(o_ref)


def _gmm(a, w, block_e, nact, *, tm, tn, out_dtype, bias=None, lhs_silu=False):
    m, k = a.shape
    _, _, n = w.shape
    grid = (n // tn, m // tm)
    in_specs = [
        pl.BlockSpec((tm, k), lambda j, i, be, na: (i, 0)),
        pl.BlockSpec((1, k, tn), lambda j, i, be, na: (be[i], 0, j)),
    ]
    args = [a, w]
    if bias is not None:
        in_specs.append(pl.BlockSpec((1, 1, tn), lambda j, i, be, na: (be[i], 0, j)))
        args.append(bias)
    return pl.pallas_call(
        functools.partial(_gmm_kernel, lhs_silu=lhs_silu, has_bias=bias is not None),
        grid_spec=pltpu.PrefetchScalarGridSpec(
            num_scalar_prefetch=2,
            grid=grid,
            in_specs=in_specs,
            out_specs=pl.BlockSpec((tm, tn), lambda j, i, be, na: (i, j)),
            scratch_shapes=[pltpu.VMEM((k, tn), BF16)],
        ),
        out_shape=jax.ShapeDtypeStruct((m, n), out_dtype),
        compiler_params=_cparams(("arbitrary", "arbitrary")),
    )(block_e, nact, *args)


CAST_ROWS = 32


def _emm_kernel(be_ref, nact_ref, nexte_ref, a_ref, *rest, n_w):
    w_hbm = rest[:n_w]
    o_ref, stage_ref, wb_ref, sem = rest[n_w:]
    j = pl.program_id(0)
    i = pl.program_id(1)
    tn = o_ref.shape[1]
    e = be_ref[i]

    def slab_copies(expert, col_tile):
        cols = pl.ds(pl.multiple_of(col_tile * tn, LANES), tn)
        return [pltpu.make_async_copy(w_hbm[t].at[expert, :, cols], stage_ref.at[t], sem.at[t])
                for t in range(n_w)]

    @pl.when(i < nact_ref[0])
    def _():
        @pl.when((i == 0) | (e != be_ref[jnp.maximum(i - 1, 0)]))
        def _():
            @pl.when((j == 0) & (i == 0))
            def _():
                for c in slab_copies(e, j):
                    c.start()
            for c in slab_copies(e, j):
                c.wait()

            def round_rows(c, carry):
                rows = pl.ds(pl.multiple_of(c * CAST_ROWS, CAST_ROWS), CAST_ROWS)
                for t in range(n_w):
                    wb_ref[t, rows, :] = stage_ref[t, rows, :].astype(BF16)
                return carry
            lax.fori_loop(0, stage_ref.shape[1] // CAST_ROWS, round_rows, 0)
            nxt = nexte_ref[i]

            @pl.when(nxt >= 0)
            def _():
                for c in slab_copies(nxt, j):
                    c.start()

            @pl.when((nxt < 0) & (j + 1 < pl.num_programs(0)))
            def _():
                for c in slab_copies(be_ref[0], j + 1):
                    c.start()

        a = a_ref[...]
        acc = jnp.dot(a, wb_ref[0], preferred_element_type=F32)
        if n_w == 2:
            acc = _silu(acc) * jnp.dot(a, wb_ref[1], preferred_element_type=F32)
        o_ref[...] = acc.astype(o_ref.dtype)

    @pl.when(i >= nact_ref[0])
    def _():
        o_ref[...] = jnp.zeros_like(o_ref)


def _emm(a, ws, block_e, nact, next_e, *, tm, tn, out_dtype):
    m, k = a.shape
    n = ws[0].shape[2]
    n_w = len(ws)
    any_spec = pl.BlockSpec(memory_space=pl.ANY)
    return pl.pallas_call(
        functools.partial(_emm_kernel, n_w=n_w),
        grid_spec=pltpu.PrefetchScalarGridSpec(
            num_scalar_prefetch=3,
            grid=(n // tn, m // tm),
            in_specs=[pl.BlockSpec((tm, k), lambda j, i, *_: (i, 0))] + [any_spec] * n_w,
            out_specs=pl.BlockSpec((tm, tn), lambda j, i, *_: (i, j)),
            scratch_shapes=[pltpu.VMEM((n_w, k, tn), F32), pltpu.VMEM((n_w, k, tn), BF16),
                            pltpu.SemaphoreType.DMA((n_w,))],
        ),
        out_shape=jax.ShapeDtypeStruct((m, n), out_dtype),
        compiler_params=_cparams(("arbitrary", "arbitrary")),
    )(block_e, nact, next_e, a, *ws)


def _pick(i, p_ref, s_ref):
    return jnp.where(i < PROMPT_TILES, p_ref[0], s_ref[...])


def _prenorm_kernel(xp_ref, xs_ref, g_ref, scp_ref, scs_ref, shp_ref, shs_ref, h_ref):
    i = pl.program_id(0)
    x = jnp.where(i < PROMPT_TILES, xp_ref[...], xs_ref[...])
    sc = _pick(i, scp_ref, scs_ref)
    sh = _pick(i, shp_ref, shs_ref)
    h_ref[...] = (_rms(x, g_ref[...]) * (1.0 + sc) + sh).astype(h_ref.dtype)


def _resid_kernel(*refs, x_split, y_split, with_next, with_router, h_dtype):
    it = iter(refs)
    if x_split:
        xp_ref, xs_ref = next(it), next(it)
    else:
        x_ref = next(it)
    f_ref, gpost_ref, gap_ref, gas_ref = [next(it) for _ in range(4)]
    if with_next:
        gpre_ref, scp_ref, scs_ref, shp_ref, shs_ref = [next(it) for _ in range(5)]
    if with_router:
        rw_ref, rb_ref = next(it), next(it)
    if y_split:
        yp_ref, ys_ref = next(it), next(it)
    else:
        xo_ref = next(it)
    if with_next:
        h_ref = next(it)
    if with_router:
        lg_ref = next(it)
    i = pl.program_id(0)
    ga = _pick(i, gap_ref, gas_ref)
    x = jnp.where(i < PROMPT_TILES, xp_ref[...], xs_ref[...]) if x_split else x_ref[...]
    x = x + ga * _rms(f_ref[...], gpost_ref[...])
    if y_split:
        @pl.when(i < PROMPT_TILES)
        def _():
            yp_ref[...] = x

        @pl.when(i == PROMPT_TILES)
        def _():
            ys_ref[...] = x
    else:
        xo_ref[...] = x
    if with_next:
        sc = _pick(i, scp_ref, scs_ref)
        sh = _pick(i, shp_ref, shs_ref)
        h = _rms(x, gpre_ref[...]) * (1.0 + sc) + sh
        h_ref[...] = h.astype(h_dtype)
        if with_router:
            lg_ref[...] = jnp.dot(h, rw_ref[...], precision=HIGHEST,
                                  preferred_element_type=F32) + rb_ref[...]


def _row_spec(width):
    return pl.BlockSpec((ROW_TILE, width), lambda i, *_: (i, 0))


def _vec_spec():
    return pl.BlockSpec((1, D_MODEL), lambda i, *_: (0, 0))


def _modp_spec():
    return pl.BlockSpec((1, 1, D_MODEL), lambda i, *_: (jnp.minimum(i // TILES_PER_SEQ, BATCH - 1), 0, 0))


def _mods_spec():
    return pl.BlockSpec((DEC_BATCH, D_MODEL), lambda i, *_: (0, 0))


def _prompt_rows_spec():
    return pl.BlockSpec((ROW_TILE, D_MODEL), lambda i, *_: (jnp.minimum(i, PROMPT_TILES - 1), 0))


def _prenorm(xp, xs, g, sc, sh):
    return pl.pallas_call(
        _prenorm_kernel,
        grid=(N_TILES,),
        in_specs=[_prompt_rows_spec(), _mods_spec(), _vec_spec(), _modp_spec(), _mods_spec(), _modp_spec(),
                  _mods_spec()],
        out_specs=_row_spec(D_MODEL),
        out_shape=jax.ShapeDtypeStruct((N_ROWS, D_MODEL), BF16),
        compiler_params=_cparams(("arbitrary",)),
    )(xp, xs, g, sc[0], sc[1], sh[0], sh[1])


def _resid(x, f, gpost, ga, nxt=None, router=None, h_dtype=BF16, y_split=False):
    x_split = isinstance(x, tuple)
    if x_split:
        args = [x[0], x[1]]
        in_specs = [_prompt_rows_spec(), _mods_spec()]
    else:
        args = [x]
        in_specs = [_row_spec(D_MODEL)]
    args += [f, gpost, ga[0], ga[1]]
    in_specs += [_row_spec(D_MODEL), _vec_spec(), _modp_spec(), _mods_spec()]
    if y_split:
        out_shape = [jax.ShapeDtypeStruct((N_PROMPT, D_MODEL), F32), jax.ShapeDtypeStruct((DEC_BATCH, D_MODEL), F32)]
        out_specs = [_prompt_rows_spec(), _mods_spec()]
    else:
        out_shape = [jax.ShapeDtypeStruct((N_ROWS, D_MODEL), F32)]
        out_specs = [_row_spec(D_MODEL)]
    if nxt is not None:
        gpre, sc, sh = nxt
        args += [gpre, sc[0], sc[1], sh[0], sh[1]]
        in_specs += [_vec_spec(), _modp_spec(), _mods_spec(), _modp_spec(), _mods_spec()]
        out_shape.append(jax.ShapeDtypeStruct((N_ROWS, D_MODEL), h_dtype))
        out_specs.append(_row_spec(D_MODEL))
    if router is not None:
        rw, rb = router
        args += [rw, rb]
        in_specs += [pl.BlockSpec((D_MODEL, LANES), lambda i: (0, 0)), pl.BlockSpec((1, LANES), lambda i: (0, 0))]
        out_shape.append(jax.ShapeDtypeStruct((N_ROWS, LANES), F32))
        out_specs.append(_row_spec(LANES))
    return pl.pallas_call(
        functools.partial(_resid_kernel, x_split=x_split, y_split=y_split, with_next=nxt is not None,
                          with_router=router is not None, h_dtype=h_dtype),
        grid=(N_TILES,),
        in_specs=in_specs,
        out_specs=out_specs,
        out_shape=out_shape,
        compiler_params=_cparams(("arbitrary",)),
    )(*args)


def _group_tables():
    grp = np.arange(CONV_DIM)[:, None] // (CONV_DIM // CONV_GROUPS) == np.arange(LANES)[None, :]
    return jnp.asarray(grp, BF16), jnp.asarray(grp.T, BF16)


def _split2(x):
    hi = x.astype(BF16)
    return hi, (x - hi.astype(F32)).astype(BF16)


def _group_rms(y, gnorm, sel, sel_t):
    width = CONV_DIM // CONV_GROUPS
    hi, lo = _split2(y * y)
    ss = jnp.dot(hi, sel, preferred_element_type=F32) + jnp.dot(lo, sel, preferred_element_type=F32)
    hi, lo = _split2(lax.rsqrt(ss * (1.0 / width) + EPS))
    scale = jnp.dot(hi, sel_t, preferred_element_type=F32) + jnp.dot(lo, sel_t, preferred_element_type=F32)
    return y * scale * gnorm


def _conv_prompt_kernel(hc_ref, bg_ref, cg_ref, hch_ref, cgh_ref, cw_ref, gn_ref, sel_ref, selt_ref, mix_ref,
                        y_ref, nc_ref, *, tt):
    del mix_ref
    t = pl.program_id(1)
    u = cg_ref[...] * hc_ref[...]
    halo = jnp.where(t == 0, 0.0, cgh_ref[...] * hch_ref[...])
    h1 = halo[SUBLANES - 1:SUBLANES]
    h2 = halo[SUBLANES - 2:SUBLANES - 1]
    row = lax.broadcasted_iota(I32, u.shape, 0)
    u1 = jnp.where(row == 0, h1, pltpu.roll(u, 1, 0))
    u2 = jnp.where(row == 0, h2, jnp.where(row == 1, h1, pltpu.roll(u, 2, 0)))
    conv = cw_ref[0:1] * u2 + cw_ref[1:2] * u1 + cw_ref[2:3] * u
    y_ref[...] = _group_rms(bg_ref[...] * conv, gn_ref[...], sel_ref[...], selt_ref[...]).astype(y_ref.dtype)

    @pl.when(t == pl.num_programs(1) - 1)
    def _():
        nc_ref[0] = u[tt - (CONV_W - 1):]


def _conv_prompt(proj, conv_w, conv_norm, mix, *, tt=256):
    nt = SEQ // tt
    cblk = lambda c: pl.BlockSpec((tt, CONV_DIM), lambda b, t: (b * nt + t, c))
    hblk = lambda c: pl.BlockSpec(
        (SUBLANES, CONV_DIM), lambda b, t: (jnp.maximum((b * nt + t) * (tt // SUBLANES) - 1, 0), c))
    return pl.pallas_call(
        functools.partial(_conv_prompt_kernel, tt=tt),
        grid=(BATCH, nt),
        in_specs=[cblk(0), cblk(1), cblk(2), hblk(0), hblk(2),
                  pl.BlockSpec((CONV_W, CONV_DIM), lambda b, t: (0, 0)),
                  pl.BlockSpec((1, CONV_DIM), lambda b, t: (0, 0)),
                  pl.BlockSpec((CONV_DIM, LANES), lambda b, t: (0, 0)),
                  pl.BlockSpec((LANES, CONV_DIM), lambda b, t: (0, 0)),
                  pl.BlockSpec(memory_space=pl.ANY)],
        out_specs=[pl.BlockSpec((tt, CONV_DIM), lambda b, t: (b * nt + t, 0)),
                   pl.BlockSpec((1, CONV_W - 1, CONV_DIM), lambda b, t: (b, 0, 0))],
        out_shape=[jax.ShapeDtypeStruct(mix.shape, mix.dtype),
                   jax.ShapeDtypeStruct((BATCH, CONV_W - 1, CONV_DIM), F32)],
        input_output_aliases={9: 0},
        compiler_params=_cparams(("arbitrary", "arbitrary")),
    )(proj, proj, proj, proj, proj, conv_w, conv_norm, *_group_tables(), mix)


HG_LEVELS = tuple(1 << i for i in range(ROW_TILE.bit_length() - 1))


def _gates(z, lb):
    sg = jax.nn.sigmoid(z)
    f = lb + (1.0 - lb) * sg
    return jnp.log(jnp.maximum(f, F_MIN)), (1.0 - lb) * (1.0 - sg)


def _level_table():
    t = np.arange(ROW_TILE)[:, None]
    s = np.arange(ROW_TILE)[None, :]
    x = np.maximum(t ^ s, 1)
    lvl = np.floor(np.log2(x)).astype(np.int32)
    lvl = np.where(t == s, len(HG_LEVELS), np.where(s < t, lvl, -1))
    return jnp.asarray(lvl, I32)


def _nt_dot(x, y):
    return lax.dot_general(x, y, (((1,), (1,)), ((), ())), preferred_element_type=F32)


def _hgrn_tile(q, z, v, lb, st, tri, lvl):
    logf, k = _gates(z, lb)
    hi = logf.astype(BF16)
    rem = logf - hi.astype(F32)
    mid = rem.astype(BF16)
    lo = (rem - mid.astype(F32)).astype(BF16)
    parts = jnp.dot(tri, jnp.concatenate([hi, mid, lo], axis=1), preferred_element_type=F32)
    a = parts[:, :HG_DK] + parts[:, HG_DK:2 * HG_DK] + parts[:, 2 * HG_DK:]
    row = lax.broadcasted_iota(I32, (ROW_TILE, HG_DK), 0)
    a8 = a.reshape(ROW_TILE // SUBLANES, SUBLANES, HG_DK)
    sub8 = lax.broadcasted_iota(I32, a8.shape, 1)
    scores = jnp.where(lvl == len(HG_LEVELS), _nt_dot(q.astype(BF16), k.astype(BF16)), 0.0)
    for li, c in enumerate(HG_LEVELS):
        if c == 1:
            d = jnp.where((row & 1) == 1, logf, 0.0)
        elif c == 2:
            anchor = jnp.where(sub8 < 4, a8[:, 1:2, :], a8[:, 5:6, :])
            d = (a8 - anchor).reshape(ROW_TILE, HG_DK)
        else:
            ab = a.reshape(ROW_TILE // (2 * c), 2 * c, HG_DK)
            d = (ab - ab[:, c - 1:c, :]).reshape(ROW_TILE, HG_DK)
        x = (jnp.where((row & c) != 0, q, k) * jnp.exp(-jnp.abs(d))).astype(BF16)
        scores = jnp.where(lvl == li, _nt_dot(x, x), scores)
    vb = v.astype(BF16)
    a_last = a[ROW_TILE - 1:ROW_TILE, :]
    o = jnp.dot(scores.astype(BF16), vb, preferred_element_type=F32)
    o = o + _nt_dot((q * jnp.exp(a)).astype(BF16), st.astype(BF16))
    kt = (k * jnp.exp(a_last - a)).astype(BF16)
    st = st * jnp.exp(a_last) + lax.dot_general(vb, kt, (((0,), (0,)), ((), ())),
                                                preferred_element_type=F32)
    return o, st


def _hgrn_prompt_kernel(q_ref, z_ref, v_ref, og_ref, lb_ref, gn_ref, tri_ref, lvl_ref, mix_ref, o_ref, s_ref,
                        st_ref, *, tt):
    del mix_ref
    t = pl.program_id(1)

    @pl.when(t == 0)
    def _():
        st_ref[...] = jnp.zeros_like(st_ref)

    def body(j, carry):
        rows = pl.ds(pl.multiple_of(j * ROW_TILE, ROW_TILE), ROW_TILE)
        for h in range(HG_HEADS):
            cols = slice(h * LANES, (h + 1) * LANES)
            o, st = _hgrn_tile(q_ref[rows, cols], z_ref[rows, cols], v_ref[rows, cols], lb_ref[:, cols],
                               st_ref[h], tri_ref[...], lvl_ref[...])
            st_ref[h] = st
            o_ref[rows, cols] = (_rms(o, gn_ref[:, cols]) * _silu(og_ref[rows, cols])).astype(o_ref.dtype)
        return carry

    lax.fori_loop(0, tt // ROW_TILE, body, 0)

    @pl.when(t == pl.num_programs(1) - 1)
    def _():
        for h in range(HG_HEADS):
            s_ref[0, h] = st_ref[h].T


def _hgrn_prompt(proj, lb, hgrn_norm, mix, *, tt=256):
    nt = SEQ // tt
    width = HG_HEADS * LANES
    col0 = 3 * CONV_DIM // width
    blk = lambda part: pl.BlockSpec((tt, width), lambda b, t: (b * nt + t, col0 + part))
    hvec = pl.BlockSpec((1, width), lambda b, t: (0, 0))
    const = pl.BlockSpec((ROW_TILE, ROW_TILE), lambda b, t: (0, 0))
    tri =jnp.asarray(np.tril(np.ones((ROW_TILE, ROW_TILE), np.float32)), BF16)
    return pl.pallas_call(
        functools.partial(_hgrn_prompt_kernel, tt=tt),
        grid=(BATCH, nt),
        in_specs=[blk(0), blk(1), blk(2), blk(3), hvec, hvec, const, const, pl.BlockSpec(memory_space=pl.ANY)],
        out_specs=[pl.BlockSpec((tt, width), lambda b, t: (b * nt + t, 1)),
                   pl.BlockSpec((1, HG_HEADS, HG_DK, HG_DV), lambda b, t: (b, 0, 0, 0))],
        out_shape=[jax.ShapeDtypeStruct(mix.shape, mix.dtype),
                   jax.ShapeDtypeStruct((BATCH, HG_HEADS, HG_DK, HG_DV), F32)],
        scratch_shapes=[pltpu.VMEM((HG_HEADS, HG_DV, HG_DK), F32)],
        input_output_aliases={8: 0},
        compiler_params=_cparams(("arbitrary", "arbitrary")),
    )(proj, proj, proj, proj, lb.reshape(1, width), hgrn_norm.reshape(1, width), tri, _level_table(), mix)


def _conv_sample_kernel(hc_ref, bg_ref, cg_ref, cs_ref, cw_ref, gn_ref, sel_ref, selt_ref, mix_ref, y_ref, nc_ref):
    del mix_ref
    u = cg_ref[...] * hc_ref[...]
    s0 = cs_ref[:, 0, :]
    s1 = cs_ref[:, 1, :]
    conv = cw_ref[0:1] * s0 + cw_ref[1:2] * s1 + cw_ref[2:3] * u
    y_ref[...] = _group_rms(bg_ref[...] * conv, gn_ref[...], sel_ref[...], selt_ref[...]).astype(y_ref.dtype)
    nc_ref[:, 0, :] = s1
    nc_ref[:, 1, :] = u


def _conv_sample(proj, conv_state, conv_w, conv_norm, mix):
    rb = PROMPT_TILES
    cblk = lambda c: pl.BlockSpec((DEC_BATCH, CONV_DIM), lambda i: (rb, c))
    return pl.pallas_call(
        _conv_sample_kernel,
        grid=(1,),
        in_specs=[cblk(0), cblk(1), cblk(2),
                  pl.BlockSpec((DEC_BATCH, CONV_W - 1, CONV_DIM), lambda i: (0, 0, 0)),
                  pl.BlockSpec((CONV_W, CONV_DIM), lambda i: (0, 0)),
                  pl.BlockSpec((1, CONV_DIM), lambda i: (0, 0)),
                  pl.BlockSpec((CONV_DIM, LANES), lambda i: (0, 0)),
                  pl.BlockSpec((LANES, CONV_DIM), lambda i: (0, 0)),
                  pl.BlockSpec(memory_space=pl.ANY)],
        out_specs=[pl.BlockSpec((DEC_BATCH, CONV_DIM), lambda i: (rb, 0)),
                   pl.BlockSpec((DEC_BATCH, CONV_W - 1, CONV_DIM), lambda i: (0, 0, 0))],
        out_shape=[jax.ShapeDtypeStruct(mix.shape, mix.dtype),
                   jax.ShapeDtypeStruct((DEC_BATCH, CONV_W - 1, CONV_DIM), F32)],
        input_output_aliases={8: 0},
        compiler_params=_cparams(("arbitrary",)),
    )(proj, proj, proj, conv_state, conv_w, conv_norm, *_group_tables(), mix)


def _hgrn_sample_kernel(q_ref, z_ref, v_ref, og_ref, lb_ref, gn_ref, s_ref, *rest, bg, slab):
    o_ref, so_ref, osc_ref = rest[-3:]
    s_ref = s_ref.at[0]
    for other in range(so_ref.shape[0]):
        if other != slab:
            so_ref[other] = jnp.zeros(so_ref.shape[1:], so_ref.dtype)
    so_ref = so_ref.at[slab]
    g = pl.program_id(1)
    lb = lb_ref[0]
    logf, k = _gates(z_ref[...], lb)
    f = jnp.exp(logf)
    shift = (DEC_BATCH - g * bg) % DEC_BATCH
    ft = pltpu.roll(f.T, shift, 1)
    kt = pltpu.roll(k.T, shift, 1)
    qt = pltpu.roll(q_ref[...].T, shift, 1)
    rows = pl.ds(pl.multiple_of(g * bg, bg), bg)
    v = v_ref[rows, :]
    for j in range(bg):
        s_new = ft[:, j:j + 1] * s_ref[j, 0] + kt[:, j:j + 1] * v[j:j + 1, :]
        so_ref[j, 0] = s_new
        osc_ref[j:j + 1, :] = jnp.sum(qt[:, j:j + 1] * s_new, axis=0, keepdims=True)
    o = osc_ref[...]
    o_ref[...] = (_rms(o, gn_ref[0]) * _silu(og_ref[rows, :])).astype(o_ref.dtype)


def _hgrn_sample(proj, state_all, l, lb, hgrn_norm, mix, new_state_all=None, *, bg=16):
    rb = PROMPT_TILES
    col0 = 3 * CONV_DIM // LANES
    blk = lambda part: pl.BlockSpec((DEC_BATCH, LANES), lambda h, g: (rb, col0 + part * HG_HEADS + h))
    hvec = pl.BlockSpec((1, 1, LANES), lambda h, g: (h, 0, 0))
    sblk = pl.BlockSpec((1, bg, 1, HG_DK, HG_DV), lambda h, g: (l, g, h, 0, 0))
    any_spec = pl.BlockSpec(memory_space=pl.ANY)
    args = [proj, proj, proj, proj, lb.reshape(HG_HEADS, 1, HG_DK), hgrn_norm.reshape(HG_HEADS, 1, HG_DV),
            state_all, mix]
    in_specs = [blk(0), blk(1), blk(2), blk(3), hvec, hvec, sblk, any_spec]
    aliases = {7: 0}
    if new_state_all is not None:
        args.append(new_state_all)
        in_specs.append(any_spec)
        aliases[8] = 1
        so_blk, slab = sblk, 0
    else:
        n_slabs = state_all.shape[0]
        so_blk = pl.BlockSpec((n_slabs, bg, 1, HG_DK, HG_DV), lambda h, g: (0, g, h, 0, 0))
        slab = l
    return pl.pallas_call(
        functools.partial(_hgrn_sample_kernel, bg=bg, slab=slab),
        grid=(HG_HEADS, DEC_BATCH // bg),
        in_specs=in_specs,
        out_specs=[pl.BlockSpec((bg, LANES), lambda h, g: (N_PROMPT // bg + g, CONV_DIM // LANES + h)), so_blk],
        out_shape=[jax.ShapeDtypeStruct(mix.shape, mix.dtype),
                   jax.ShapeDtypeStruct(state_all.shape, F32)],
        scratch_shapes=[pltpu.VMEM((bg, HG_DV), F32)],
        input_output_aliases=aliases,
        compiler_params=_cparams(("arbitrary", "arbitrary")),
    )(*args)


def _route_kernel(lg_ref, ri_ref, gate_ref, cnt_ref, carry_ref):
    i = pl.program_id(0)

    @pl.when(i == 0)
    def _():
        carry_ref[...] = jnp.zeros_like(carry_ref)

    lane = lax.broadcasted_iota(I32, (ROW_TILE, LANES), 1)
    lanef = lane.astype(F32)
    lg = jnp.where(lane < N_EXPERTS, lg_ref[...], -jnp.inf)
    m1 = jnp.max(lg, axis=-1, keepdims=True)
    i1 = jnp.min(jnp.where(lg == m1, lanef, float(LANES)), axis=-1, keepdims=True).astype(I32)
    lg2 = jnp.where(lane == i1, -jnp.inf, lg)
    m2 = jnp.max(lg2, axis=-1, keepdims=True)
    i2 = jnp.min(jnp.where(lg2 == m2, lanef, float(LANES)), axis=-1, keepdims=True).astype(I32)
    e = jnp.exp(m2 - m1)
    g1 = 1.0 / (1.0 + e)
    g2 = e / (1.0 + e)
    hot1 = lane == i1
    hot2 = lane == i2
    hot = (hot1 | hot2).astype(BF16)
    r = lax.broadcasted_iota(I32, (ROW_TILE, ROW_TILE), 0)
    c = lax.broadcasted_iota(I32, (ROW_TILE, ROW_TILE), 1)
    before = (c < r).astype(BF16)
    tot = jnp.dot(before, hot, preferred_element_type=F32) + carry_ref[...]
    r1 = jnp.sum(jnp.where(hot1, tot, 0.0), axis=-1, keepdims=True).astype(I32)
    r2 = jnp.sum(jnp.where(hot2, tot, 0.0), axis=-1, keepdims=True).astype(I32)
    ri_ref[...] = jnp.where(lane == 0, i1, jnp.where(lane == 1, i2, jnp.where(lane == 2, r1,
                            jnp.where(lane == 3, r2, 0))))
    gate_ref[...] = jnp.where(lane == 0, g1, jnp.where(lane == 1, g2, 0.0))
    carry_ref[...] += jnp.sum(hot.astype(F32), axis=0, keepdims=True)
    cnt_ref[...] = carry_ref[...]


def _route(logits):
    return pl.pallas_call(
        _route_kernel,
        grid=(N_TILES,),
        in_specs=[_row_spec(LANES)],
        out_specs=[_row_spec(LANES), _row_spec(LANES), pl.BlockSpec((1, LANES), lambda i: (0, 0))],
        out_shape=[jax.ShapeDtypeStruct((N_ROWS, LANES), I32),
                   jax.ShapeDtypeStruct((N_ROWS, LANES), F32),
                   jax.ShapeDtypeStruct((1, LANES), F32)],
        scratch_shapes=[pltpu.VMEM((1, LANES), F32)],
        compiler_params=_cparams(("arbitrary",)),
    )(logits)


def _gather_kernel(d1_ref, d2_ref, nact_ref, h_ref, xb_ref, tok_ref, buf_ref, sem):
    i = pl.program_id(0)

    nact = nact_ref[0]

    @pl.when(i == 0)
    def _():
        def clear(p, carry):
            tok_ref[p] = 0
            return carry
        lax.fori_loop(0, P_ROWS, clear, 0, unroll=8)

        def scatter(t, carry):
            tok_ref[d1_ref[t]] = t
            tok_ref[d2_ref[t]] = t
            return carry
        lax.fori_loop(0, N_ROWS, scatter, 0, unroll=4)

    def start_block(b):
        slot = b % 2

        def start(r, carry):
            tok = tok_ref[b * MOE_BLOCK + r]
            pltpu.make_async_copy(h_ref.at[pl.ds(tok, 1), :], buf_ref.at[slot, pl.ds(r, 1), :],
                                  sem.at[slot]).start()
            return carry
        lax.fori_loop(0, MOE_BLOCK, start, 0, unroll=8)

    @pl.when(i == 0)
    def _():
        start_block(i)

    @pl.when(i + 1 < nact)
    def _():
        start_block(i + 1)

    @pl.when(i < nact)
    def _():
        slot = i % 2
        pltpu.make_async_copy(h_ref.at[pl.ds(0, MOE_BLOCK), :], buf_ref.at[slot], sem.at[slot]).wait()
        xb_ref[...] = buf_ref[slot].astype(xb_ref.dtype)

    @pl.when(i >= nact)
    def _():
        xb_ref[...] = jnp.zeros_like(xb_ref)


def _gather(h, dest1, dest2, nact):
    return pl.pallas_call(
        _gather_kernel,
        grid_spec=pltpu.PrefetchScalarGridSpec(
            num_scalar_prefetch=3,
            grid=(N_BLOCKS,),
            in_specs=[pl.BlockSpec(memory_space=pl.ANY)],
            out_specs=pl.BlockSpec((MOE_BLOCK, D_MODEL), lambda i, d1, d2, na: (i, 0)),
            scratch_shapes=[pltpu.SMEM((P_ROWS,), I32), pltpu.VMEM((2, MOE_BLOCK, D_MODEL), F32),
                            pltpu.SemaphoreType.DMA((2,))],
        ),
        out_shape=jax.ShapeDtypeStruct((P_ROWS, D_MODEL), BF16),
        compiler_params=_cparams(("arbitrary",)),
    )(dest1, dest2, nact, h)


def _combine_resid_kernel(d1_ref, d2_ref, x_ref, gate_ref, gpost_ref, gap_ref, gas_ref, yb_ref, yp_ref, ys_ref,
                          b1_ref, b2_ref, sem):
    i = pl.program_id(0)

    def start_tile(t):
        slot = t % 2

        def start(r, carry):
            tok = t * ROW_TILE + r
            pltpu.make_async_copy(yb_ref.at[pl.ds(d1_ref[tok], 1), :], b1_ref.at[slot, pl.ds(r, 1), :],
                                  sem.at[0, slot]).start()
            pltpu.make_async_copy(yb_ref.at[pl.ds(d2_ref[tok], 1), :], b2_ref.at[slot, pl.ds(r, 1), :],
                                  sem.at[1, slot]).start()
            return carry
        lax.fori_loop(0, ROW_TILE, start, 0, unroll=8)

    @pl.when(i == 0)
    def _():
        start_tile(i)

    @pl.when(i + 1 < pl.num_programs(0))
    def _():
        start_tile(i + 1)

    slot = i % 2
    pltpu.make_async_copy(yb_ref.at[pl.ds(0, ROW_TILE), :], b1_ref.at[slot], sem.at[0, slot]).wait()
    pltpu.make_async_copy(yb_ref.at[pl.ds(0, ROW_TILE), :], b2_ref.at[slot], sem.at[1, slot]).wait()
    gate = gate_ref[...]
    f = gate[:, 0:1] * b1_ref[slot] + gate[:, 1:2] * b2_ref[slot]
    x = x_ref[...] + _pick(i, gap_ref, gas_ref) * _rms(f, gpost_ref[...])

    @pl.when(i < PROMPT_TILES)
    def _():
        yp_ref[...] = x

    @pl.when(i == PROMPT_TILES)
    def _():
        ys_ref[...] = x


def _combine_resid(x, yb, gates, dest1, dest2, gpost, ga):
    return pl.pallas_call(
        _combine_resid_kernel,
        grid_spec=pltpu.PrefetchScalarGridSpec(
            num_scalar_prefetch=2,
            grid=(N_TILES,),
            in_specs=[_row_spec(D_MODEL), _row_spec(LANES), _vec_spec(), _modp_spec(), _mods_spec(),
                      pl.BlockSpec(memory_space=pl.ANY)],
            out_specs=[_prompt_rows_spec(), _mods_spec()],
            scratch_shapes=[pltpu.VMEM((2, ROW_TILE, D_MODEL), F32), pltpu.VMEM((2, ROW_TILE, D_MODEL), F32),
                            pltpu.SemaphoreType.DMA((2, 2))],
        ),
        out_shape=[jax.ShapeDtypeStruct((N_PROMPT, D_MODEL), F32), jax.ShapeDtypeStruct((DEC_BATCH, D_MODEL), F32)],
        compiler_params=_cparams(("arbitrary",)),
    )(dest1, dest2, x, gates, gpost, ga[0], ga[1], yb)


def _moe_experts(h, logits, w1, w3, w2):
    ri, gates, counts = _route(logits)
    counts = counts[0, :N_EXPERTS].astype(I32)
    padded = (counts + MOE_BLOCK - 1) // MOE_BLOCK * MOE_BLOCK
    pends = jnp.cumsum(padded)
    pstarts = pends - padded
    dest1 = pstarts[ri[:, 0]] + ri[:, 2]
    dest2 = pstarts[ri[:, 1]] + ri[:, 3]
    nact = (pends[-1:] // MOE_BLOCK).astype(I32)
    block_start = jnp.arange(N_BLOCKS, dtype=I32) * MOE_BLOCK
    block_e = jnp.minimum(jnp.sum((pends[None, :] <= block_start[:, None]).astype(I32), axis=1), N_EXPERTS - 1)
    run_end = (pends // MOE_BLOCK)[block_e]
    next_e = jnp.where(run_end < nact[0], block_e[jnp.minimum(run_end, N_BLOCKS - 1)], -1).astype(I32)
    xb = _gather(h, dest1, dest2, nact)
    gb = _emm(xb, (w1, w3), block_e, nact, next_e, tm=MOE_BLOCK, tn=1792, out_dtype=BF16)
    yb = _emm(gb, (w2,), block_e, nact, next_e, tm=MOE_BLOCK, tn=512, out_dtype=F32)
    return yb, gates, dest1, dest2


DENSE_TM = 640


def _dense(a, ws, e, *, tn, out_dtype):
    nb = a.shape[0] // DENSE_TM
    return _emm(a, ws, jnp.full((nb,), e, I32), jnp.full((1,), nb, I32), jnp.full((nb,), -1, I32),
                tm=DENSE_TM, tn=tn, out_dtype=out_dtype)


def kernel(x_prompt, x_sample, state_conv, state_hgrn, c_prompt, c_sample, norm_pre, norm_post, w_mod, b_mod, w_in, conv_w, conv_norm, lb_logits, hgrn_norm, w_out, ffn_w1, ffn_w3, ffn_w2, router_w, router_b, moe_w1, moe_w3, moe_w2):
    p = jax.nn.softmax(lb_logits.astype(F32), axis=0)
    lb_all = jnp.cumsum(p, axis=0) - p[0:1]

    n_cond = BATCH + DEC_BATCH
    cond_rows = (n_cond + SUBLANES - 1) // SUBLANES * SUBLANES
    c_all = jnp.concatenate([c_prompt, c_sample, jnp.zeros((cond_rows - n_cond, D_MODEL), F32)], axis=0)
    mod = _gmm(jnp.concatenate([c_all] * DEPTH, axis=0), w_mod, jnp.arange(DEPTH, dtype=I32),
               jnp.full((1,), DEPTH, I32), tm=cond_rows, tn=1024, out_dtype=F32,
               bias=b_mod.reshape(DEPTH, 1, 6 * D_MODEL), lhs_silu=True)

    def mod_vec(l, j):
        rows = mod[l * cond_rows:l * cond_rows + n_cond, j * D_MODEL:(j + 1) * D_MODEL]
        return rows[:BATCH].reshape(BATCH, 1, D_MODEL), rows[BATCH:]

    x = (x_prompt.reshape(N_PROMPT, D_MODEL), x_sample.reshape(DEC_BATCH, D_MODEL))
    vec = lambda a: a.reshape(1, -1)

    h = _prenorm(x[0], x[1], vec(norm_pre[0, 0]), mod_vec(0, 1), mod_vec(0, 0))
    new_conv_p, new_hgrn_p, new_conv_s, new_hgrn_s = [], [], [], None
    for l in range(DEPTH):
        sh_f, sc_f, ga_f = mod_vec(l, 3), mod_vec(l, 4), mod_vec(l, 5)
        ga_a = mod_vec(l, 2)
        proj = _dense(h, (w_in,), l, tn=1792, out_dtype=F32)
        mix = jnp.zeros((N_ROWS, 2 * CONV_DIM), BF16)
        mix, nc_p = _conv_prompt(proj, conv_w[l], vec(conv_norm[l]), mix)
        mix, ns_p = _hgrn_prompt(proj, lb_all[l], hgrn_norm[l], mix)
        mix, nc_s = _conv_sample(proj, state_conv[l], conv_w[l], vec(conv_norm[l]), mix)
        mix, new_hgrn_s = _hgrn_sample(proj, state_hgrn, l, lb_all[l], hgrn_norm[l], mix, new_hgrn_s)
        mixed = _dense(mix, (w_out,), l, tn=1024, out_dtype=F32)
        new_conv_p.append(nc_p)
        new_hgrn_p.append(ns_p)
        new_conv_s.append(nc_s)
        nxt = (vec(norm_pre[l, 1]), sc_f, sh_f)
        if l % 2 == 0:
            j = l // 2
            x, h2 = _resid(x, mixed, vec(norm_post[l, 0]), ga_a, nxt=nxt)
            g = _dense(h2, (ffn_w1, ffn_w3), j, tn=512, out_dtype=BF16)
            f = _dense(g, (ffn_w2,), j, tn=512, out_dtype=F32)
        else:
            j = l // 2
            rw = jnp.pad(router_w[j].astype(F32), ((0, 0), (0, LANES - N_EXPERTS)))
            rb = jnp.pad(router_b[j].astype(F32), (0, LANES - N_EXPERTS)).reshape(1, LANES)
            x, h2, logits = _resid(x, mixed, vec(norm_post[l, 0]), ga_a, nxt=nxt, router=(rw, rb), h_dtype=F32)
            f = _moe_experts(h2, logits, moe_w1[j], moe_w3[j], moe_w2[j])
        if l + 1 < DEPTH:
            assert l % 2 == 0, "the expert combine is fused with the trunk's last residual step only"
            nxt = (vec(norm_pre[l + 1, 0]), mod_vec(l + 1, 1), mod_vec(l + 1, 0))
            x, h = _resid(x, f, vec(norm_post[l, 1]), ga_f, nxt=nxt)
        elif l % 2 == 0:
            y_prompt, y_sample = _resid(x, f, vec(norm_post[l, 1]), ga_f, y_split=True)
        else:
            y_prompt, y_sample = _combine_resid(x, *f, vec(norm_post[l, 1]), ga_f)

    y_prompt = y_prompt.reshape(BATCH, SEQ, D_MODEL)
    y_sample = y_sample.reshape(DEC_BATCH, 1, D_MODEL)
    return (y_prompt, y_sample, jnp.stack(new_conv_p), jnp.stack(new_hgrn_p),
            jnp.stack(new_conv_s), new_hgrn_s)
```

```python
import functools

import jax
import jax.numpy as jnp
import numpy as np
from jax import lax
from jax.experimental import pallas as pl
from jax.experimental.pallas import tpu as pltpu

F32 = jnp.float32
BF16 = jnp.bfloat16
I32 = jnp.int32
HIGHEST = lax.Precision.HIGHEST

LANES = 128
SUBLANES = 8
VMEM_LIMIT_BYTES = 56 * 1024 * 1024

D_MODEL = 2048
BATCH = 4
SEQ = 2048
DEPTH = 2
DEC_BATCH = 128
N_PROMPT = BATCH * SEQ
N_ROWS = N_PROMPT + DEC_BATCH
CONV_DIM = 1024
CONV_GROUPS = 16
CONV_W = 3
HG_HEADS = 8
HG_DK = 128
HG_DV = 128
PROJ_WIDTH = 7168
N_EXPERTS = 8
EPS = 1e-6
F_MIN = 1e-6

ROW_TILE = 128
N_TILES = N_ROWS // ROW_TILE
PROMPT_TILES = N_PROMPT // ROW_TILE
TILES_PER_SEQ = SEQ // ROW_TILE
MOE_BLOCK = 256
N_PAIRS = 2 * N_ROWS
N_BLOCKS = (N_PAIRS + N_EXPERTS * (MOE_BLOCK - 1) + MOE_BLOCK - 1) // MOE_BLOCK
P_ROWS = N_BLOCKS * MOE_BLOCK


def _cparams(sem):
    return pltpu.CompilerParams(dimension_semantics=sem, vmem_limit_bytes=VMEM_LIMIT_BYTES)


def _rms(x, g):
    return x * lax.rsqrt(jnp.mean(x * x, axis=-1, keepdims=True) + EPS) * g


def _silu(x):
    return x * jax.nn.sigmoid(x)


def _gmm_kernel(be_ref, nact_ref, a_ref, w_ref, *rest, lhs_silu, has_bias):
    if has_bias:
        b_ref, o_ref, wb_ref = rest
    else:
        o_ref, wb_ref = rest
    i = pl.program_id(1)
    prev = be_ref[jnp.maximum(i - 1, 0)]

    @pl.when((i == 0) | (be_ref[i] != prev))
    def _():
        wb_ref[...] = w_ref[0].astype(BF16)

    @pl.when(i < nact_ref[0])
    def _():
        a = a_ref[...]
        if lhs_silu:
            a = _silu(a)
        acc = jnp.dot(a.astype(BF16), wb_ref[...], preferred_element_type=F32)
        if has_bias:
            acc = acc + b_ref[0]
        o_ref[...] = acc.astype(o_ref.dtype)

    @pl.when(i >= nact_ref[0])
    def _():
        o_ref[...] = jnp.zeros_like(o_ref)


def _gmm(a, w, block_e, nact, *, tm, tn, out_dtype, bias=None, lhs_silu=False):
    m, k = a.shape
    _, _, n = w.shape
    grid = (n // tn, m // tm)
    in_specs = [
        pl.BlockSpec((tm, k), lambda j, i, be, na: (i, 0)),
        pl.BlockSpec((1, k, tn), lambda j, i, be, na: (be[i], 0, j)),
    ]
    args = [a, w]
    if bias is not None:
        in_specs.append(pl.BlockSpec((1, 1, tn), lambda j, i, be, na: (be[i], 0, j)))
        args.append(bias)
    return pl.pallas_call(
        functools.partial(_gmm_kernel, lhs_silu=lhs_silu, has_bias=bias is not None),
        grid_spec=pltpu.PrefetchScalarGridSpec(
            num_scalar_prefetch=2,
            grid=grid,
            in_specs=in_specs,
            out_specs=pl.BlockSpec((tm, tn), lambda j, i, be, na: (i, j)),
            scratch_shapes=[pltpu.VMEM((k, tn), BF16)],
        ),
        out_shape=jax.ShapeDtypeStruct((m, n), out_dtype),
        compiler_params=_cparams(("arbitrary", "arbitrary")),
    )(block_e, nact, *args)


CAST_ROWS = 32


def _emm_kernel(be_ref, nact_ref, nexte_ref, a_hbm, *rest, n_w, tm, nb):
    w_hbm = rest[:n_w]
    o_hbm, abuf, obuf, stage_ref, wb_ref, asem, osem, wsem = rest[n_w:]
    j = pl.program_id(0)
    last_tile = j + 1 == pl.num_programs(0)
    tn = obuf.shape[2]
    nact = nact_ref[0]
    base = (j * nact) % 2

    def a_copy(i, slot):
        rows = pl.ds(pl.multiple_of(i * tm, tm), tm)
        return pltpu.make_async_copy(a_hbm.at[rows, :], abuf.at[slot], asem.at[slot])

    def o_copy(i, slot):
        rows = pl.ds(pl.multiple_of(i * tm, tm), tm)
        cols = pl.ds(pl.multiple_of(j * tn, LANES), tn)
        return pltpu.make_async_copy(obuf.at[slot], o_hbm.at[rows, cols], osem.at[slot])

    def slab_copies(expert, col_tile):
        cols = pl.ds(pl.multiple_of(col_tile * tn, LANES), tn)
        return [pltpu.make_async_copy(w_hbm[t].at[expert, :, cols], stage_ref.at[t], wsem.at[t])
                for t in range(n_w)]

    @pl.when(j == 0)
    def _():
        for c in slab_copies(be_ref[0], j):
            c.start()
        a_copy(0, base).start()

    def block(i, carry):
        slot = (base + i) % 2
        e = be_ref[i]

        @pl.when(i + 1 < nact)
        def _():
            a_copy(i + 1, 1 - slot).start()

        @pl.when((i + 1 == nact) & jnp.logical_not(last_tile))
        def _():
            a_copy(0, 1 - slot).start()

        @pl.when((i == 0) | (e != be_ref[jnp.maximum(i - 1, 0)]))
        def _():
            for c in slab_copies(e, j):
                c.wait()

            def round_rows(c, carry):
                rows = pl.ds(pl.multiple_of(c * CAST_ROWS, CAST_ROWS), CAST_ROWS)
                for t in range(n_w):
                    wb_ref[t, rows, :] = stage_ref[t, rows, :].astype(BF16)
                return carry
            lax.fori_loop(0, stage_ref.shape[1] // CAST_ROWS, round_rows, 0)
            nxt = nexte_ref[i]

            @pl.when(nxt >= 0)
            def _():
                for c in slab_copies(nxt, j):
                    c.start()

            @pl.when((nxt < 0) & jnp.logical_not(last_tile))
            def _():
                for c in slab_copies(be_ref[0], j + 1):
                    c.start()

        a_copy(i, slot).wait()

        @pl.when(i >= 2)
        def _():
            o_copy(i - 2, slot).wait()

        a = abuf[slot]
        acc = jnp.dot(a, wb_ref[0], preferred_element_type=F32)
        if n_w == 2:
            acc = _silu(acc) * jnp.dot(a, wb_ref[1], preferred_element_type=F32)
        obuf[slot] = acc.astype(obuf.dtype)
        o_copy(i, slot).start()
        return carry

    lax.fori_loop(0, nact, block, 0)

    @pl.when(nact >= 2)
    def _():
        o_copy(nact - 2, (base + nact) % 2).wait()
    o_copy(nact - 1, (base + nact - 1) % 2).wait()

    @pl.when(nact < nb)
    def _():
        obuf[0] = jnp.zeros(obuf.shape[1:], obuf.dtype)

        def zero_block(i, carry):
            c = o_copy(i, 0)
            c.start()
            c.wait()
            return carry
        lax.fori_loop(nact, nb, zero_block, 0)


def _emm(a, ws, block_e, nact, next_e, *, tm, tn, out_dtype):
    m, k = a.shape
    n = ws[0].shape[2]
    n_w = len(ws)
    any_spec = pl.BlockSpec(memory_space=pl.ANY)
    return pl.pallas_call(
        functools.partial(_emm_kernel, n_w=n_w, tm=tm, nb=m // tm),
        grid_spec=pltpu.PrefetchScalarGridSpec(
            num_scalar_prefetch=3,
            grid=(n // tn,),
            in_specs=[any_spec] * (1 + n_w),
            out_specs=any_spec,
            scratch_shapes=[pltpu.VMEM((2, tm, k), BF16), pltpu.VMEM((2, tm, tn), out_dtype),
                            pltpu.VMEM((n_w, k, tn), F32), pltpu.VMEM((n_w, k, tn), BF16),
                            pltpu.SemaphoreType.DMA((2,)), pltpu.SemaphoreType.DMA((2,)),
                            pltpu.SemaphoreType.DMA((n_w,))],
        ),
        out_shape=jax.ShapeDtypeStruct((m, n), out_dtype),
        compiler_params=_cparams(("arbitrary",)),
    )(block_e, nact, next_e, a, *ws)


def _pick(i, p_ref, s_ref):
    return jnp.where(i < PROMPT_TILES, p_ref[0], s_ref[...])


def _prenorm_kernel(xp_ref, xs_ref, g_ref, scp_ref, scs_ref, shp_ref, shs_ref, h_ref):
    i = pl.program_id(0)
    x = jnp.where(i < PROMPT_TILES, xp_ref[...], xs_ref[...])
    sc = _pick(i, scp_ref, scs_ref)
    sh = _pick(i, shp_ref, shs_ref)
    h_ref[...] = (_rms(x, g_ref[...]) * (1.0 + sc) + sh).astype(h_ref.dtype)


def _resid_kernel(*refs, x_split, y_split, with_next, with_router, h_dtype):
    it = iter(refs)
    if x_split:
        xp_ref, xs_ref = next(it), next(it)
    else:
        x_ref = next(it)
    f_ref, gpost_ref, gap_ref, gas_ref = [next(it) for _ in range(4)]
    if with_next:
        gpre_ref, scp_ref, scs_ref, shp_ref, shs_ref = [next(it) for _ in range(5)]
    if with_router:
        rw_ref, rb_ref = next(it), next(it)
    if y_split:
        yp_ref, ys_ref = next(it), next(it)
    else:
        xo_ref = next(it)
    if with_next:
        h_ref = next(it)
    if with_router:
        ri_ref, gate_ref, cnt_ref, carry_ref = [next(it) for _ in range(4)]
    i = pl.program_id(0)
    ga = _pick(i, gap_ref, gas_ref)
    x = jnp.where(i < PROMPT_TILES, xp_ref[...], xs_ref[...]) if x_split else x_ref[...]
    x = x + ga * _rms(f_ref[...], gpost_ref[...])
    if y_split:
        @pl.when(i < PROMPT_TILES)
        def _():
            yp_ref[...] = x

        @pl.when(i == PROMPT_TILES)
        def _():
            ys_ref[...] = x
    else:
        xo_ref[...] = x
    if with_next:
        sc = _pick(i, scp_ref, scs_ref)
        sh = _pick(i, shp_ref, shs_ref)
        h = _rms(x, gpre_ref[...]) * (1.0 + sc) + sh
        h_ref[...] = h.astype(h_dtype)
        if with_router:
            logits = jnp.dot(h, rw_ref[...], precision=HIGHEST, preferred_element_type=F32) + rb_ref[...]
            _route_tile(logits, ri_ref, gate_ref, cnt_ref, carry_ref)


def _row_spec(width):
    return pl.BlockSpec((ROW_TILE, width), lambda i, *_: (i, 0))


def _vec_spec():
    return pl.BlockSpec((1, D_MODEL), lambda i, *_: (0, 0))


def _modp_spec():
    return pl.BlockSpec((1, 1, D_MODEL), lambda i, *_: (jnp.minimum(i // TILES_PER_SEQ, BATCH - 1), 0, 0))


def _mods_spec():
    return pl.BlockSpec((DEC_BATCH, D_MODEL), lambda i, *_: (0, 0))


def _prompt_rows_spec():
    return pl.BlockSpec((ROW_TILE, D_MODEL), lambda i, *_: (jnp.minimum(i, PROMPT_TILES - 1), 0))


def _prenorm(xp, xs, g, sc, sh):
    return pl.pallas_call(
        _prenorm_kernel,
        grid=(N_TILES,),
        in_specs=[_prompt_rows_spec(), _mods_spec(), _vec_spec(), _modp_spec(), _mods_spec(), _modp_spec(),
                  _mods_spec()],
        out_specs=_row_spec(D_MODEL),
        out_shape=jax.ShapeDtypeStruct((N_ROWS, D_MODEL), BF16),
        compiler_params=_cparams(("arbitrary",)),
    )(xp, xs, g, sc[0], sc[1], sh[0], sh[1])


def _resid(x, f, gpost, ga, nxt=None, router=None, h_dtype=BF16, y_split=False):
    x_split = isinstance(x, tuple)
    if x_split:
        args = [x[0], x[1]]
        in_specs = [_prompt_rows_spec(), _mods_spec()]
    else:
        args = [x]
        in_specs = [_row_spec(D_MODEL)]
    args += [f, gpost, ga[0], ga[1]]
    in_specs += [_row_spec(D_MODEL), _vec_spec(), _modp_spec(), _mods_spec()]
    if y_split:
        out_shape = [jax.ShapeDtypeStruct((N_PROMPT, D_MODEL), F32), jax.ShapeDtypeStruct((DEC_BATCH, D_MODEL), F32)]
        out_specs = [_prompt_rows_spec(), _mods_spec()]
    else:
        out_shape = [jax.ShapeDtypeStruct((N_ROWS, D_MODEL), F32)]
        out_specs = [_row_spec(D_MODEL)]
    if nxt is not None:
        gpre, sc, sh = nxt
        args += [gpre, sc[0], sc[1], sh[0], sh[1]]
        in_specs += [_vec_spec(), _modp_spec(), _mods_spec(), _modp_spec(), _mods_spec()]
        out_shape.append(jax.ShapeDtypeStruct((N_ROWS, D_MODEL), h_dtype))
        out_specs.append(_row_spec(D_MODEL))
    if router is not None:
        rw, rb = router
        args += [rw, rb]
        in_specs += [pl.BlockSpec((D_MODEL, LANES), lambda i: (0, 0)), pl.BlockSpec((1, LANES), lambda i: (0, 0))]
        out_shape += [jax.ShapeDtypeStruct((N_ROWS, LANES), I32), jax.ShapeDtypeStruct((N_ROWS, LANES), F32),
                      jax.ShapeDtypeStruct((1, LANES), F32)]
        out_specs += [_row_spec(LANES), _row_spec(LANES), pl.BlockSpec((1, LANES), lambda i: (0, 0))]
        scratch = [pltpu.VMEM((1, LANES), F32)]
    else:
        scratch = []
    return pl.pallas_call(
        functools.partial(_resid_kernel, x_split=x_split, y_split=y_split, with_next=nxt is not None,
                          with_router=router is not None, h_dtype=h_dtype),
        grid=(N_TILES,),
        in_specs=in_specs,
        out_specs=out_specs,
        out_shape=out_shape,
        scratch_shapes=scratch,
        compiler_params=_cparams(("arbitrary",)),
    )(*args)


def _group_tables():
    grp = np.arange(CONV_DIM)[:, None] // (CONV_DIM // CONV_GROUPS) == np.arange(LANES)[None, :]
    return jnp.asarray(grp, BF16), jnp.asarray(grp.T, BF16)


def _split2(x):
    hi = x.astype(BF16)
    return hi, (x - hi.astype(F32)).astype(BF16)


def _group_rms(y, gnorm, sel, sel_t):
    width = CONV_DIM // CONV_GROUPS
    hi, lo = _split2(y * y)
    ss = jnp.dot(hi, sel, preferred_element_type=F32) + jnp.dot(lo, sel, preferred_element_type=F32)
    hi, lo = _split2(lax.rsqrt(ss * (1.0 / width) + EPS))
    scale = jnp.dot(hi, sel_t, preferred_element_type=F32) + jnp.dot(lo, sel_t, preferred_element_type=F32)
    return y * scale * gnorm


def _conv_prompt_kernel(hc_ref, bg_ref, cg_ref, hch_ref, cgh_ref, cw_ref, gn_ref, sel_ref, selt_ref, mix_ref,
                        y_ref, nc_ref, *, tt):
    del mix_ref
    t = pl.program_id(1)
    u = cg_ref[...] * hc_ref[...]
    halo = jnp.where(t == 0, 0.0, cgh_ref[...] * hch_ref[...])
    h1 = halo[SUBLANES - 1:SUBLANES]
    h2 = halo[SUBLANES - 2:SUBLANES - 1]
    row = lax.broadcasted_iota(I32, u.shape, 0)
    u1 = jnp.where(row == 0, h1, pltpu.roll(u, 1, 0))
    u2 = jnp.where(row == 0, h2, jnp.where(row == 1, h1, pltpu.roll(u, 2, 0)))
    conv = cw_ref[0:1] * u2 + cw_ref[1:2] * u1 + cw_ref[2:3] * u
    y_ref[...] = _group_rms(bg_ref[...] * conv, gn_ref[...], sel_ref[...], selt_ref[...]).astype(y_ref.dtype)

    @pl.when(t == pl.num_programs(1) - 1)
    def _():
        nc_ref[0] = u[tt - (CONV_W - 1):]


def _conv_prompt(proj, conv_w, conv_norm, mix, *, tt=256):
    nt = SEQ // tt
    cblk = lambda c: pl.BlockSpec((tt, CONV_DIM), lambda b, t: (b * nt + t, c))
    hblk = lambda c: pl.BlockSpec(
        (SUBLANES, CONV_DIM), lambda b, t: (jnp.maximum((b * nt + t) * (tt // SUBLANES) - 1, 0), c))
    return pl.pallas_call(
        functools.partial(_conv_prompt_kernel, tt=tt),
        grid=(BATCH, nt),
        in_specs=[cblk(0), cblk(1), cblk(2), hblk(0), hblk(2),
                  pl.BlockSpec((CONV_W, CONV_DIM), lambda b, t: (0, 0)),
                  pl.BlockSpec((1, CONV_DIM), lambda b, t: (0, 0)),
                  pl.BlockSpec((CONV_DIM, LANES), lambda b, t: (0, 0)),
                  pl.BlockSpec((LANES, CONV_DIM), lambda b, t: (0, 0)),
                  pl.BlockSpec(memory_space=pl.ANY)],
        out_specs=[pl.BlockSpec((tt, CONV_DIM), lambda b, t: (b * nt + t, 0)),
                   pl.BlockSpec((1, CONV_W - 1, CONV_DIM), lambda b, t: (b, 0, 0))],
        out_shape=[jax.ShapeDtypeStruct(mix.shape, mix.dtype),
                   jax.ShapeDtypeStruct((BATCH, CONV_W - 1, CONV_DIM), F32)],
        input_output_aliases={9: 0},
        compiler_params=_cparams(("arbitrary", "arbitrary")),
    )(proj, proj, proj, proj, proj, conv_w, conv_norm, *_group_tables(), mix)


HG_LEVELS = tuple(1 << i for i in range(ROW_TILE.bit_length() - 1))


def _gates(z, lb):
    sg = jax.nn.sigmoid(z)
    f = lb + (1.0 - lb) * sg
    return jnp.log(jnp.maximum(f, F_MIN)), (1.0 - lb) * (1.0 - sg)


def _level_table():
    t = np.arange(ROW_TILE)[:, None]
    s = np.arange(ROW_TILE)[None, :]
    x = np.maximum(t ^ s, 1)
    lvl = np.floor(np.log2(x)).astype(np.int32)
    lvl = np.where(t == s, len(HG_LEVELS), np.where(s < t, lvl, -1))
    return jnp.asarray(lvl, I32)


def _nt_dot(x, y):
    return lax.dot_general(x, y, (((1,), (1,)), ((), ())), preferred_element_type=F32)


def _hgrn_tile(q, z, v, lb, st, tri, lvl):
    logf, k = _gates(z, lb)
    hi = logf.astype(BF16)
    rem = logf - hi.astype(F32)
    mid = rem.astype(BF16)
    lo = (rem - mid.astype(F32)).astype(BF16)
    parts = jnp.dot(tri, jnp.concatenate([hi, mid, lo], axis=1), preferred_element_type=F32)
    a = parts[:, :HG_DK] + parts[:, HG_DK:2 * HG_DK] + parts[:, 2 * HG_DK:]
    row = lax.broadcasted_iota(I32, (ROW_TILE, HG_DK), 0)
    a8 = a.reshape(ROW_TILE // SUBLANES, SUBLANES, HG_DK)
    sub8 = lax.broadcasted_iota(I32, a8.shape, 1)
    scores = jnp.where(lvl == len(HG_LEVELS), _nt_dot(q.astype(BF16), k.astype(BF16)), 0.0)
    for li, c in enumerate(HG_LEVELS):
        if c == 1:
            d = jnp.where((row & 1) == 1, logf, 0.0)
        elif c == 2:
            anchor = jnp.where(sub8 < 4, a8[:, 1:2, :], a8[:, 5:6, :])
            d = (a8 - anchor).reshape(ROW_TILE, HG_DK)
        else:
            ab = a.reshape(ROW_TILE // (2 * c), 2 * c, HG_DK)
            d = (ab - ab[:, c - 1:c, :]).reshape(ROW_TILE, HG_DK)
        x = (jnp.where((row & c) != 0, q, k) * jnp.exp(-jnp.abs(d))).astype(BF16)
        scores = jnp.where(lvl == li, _nt_dot(x, x), scores)
    vb = v.astype(BF16)
    a_last = a[ROW_TILE - 1:ROW_TILE, :]
    o = jnp.dot(scores.astype(BF16), vb, preferred_element_type=F32)
    o = o + _nt_dot((q * jnp.exp(a)).astype(BF16), st.astype(BF16))
    kt = (k * jnp.exp(a_last - a)).astype(BF16)
    st = st * jnp.exp(a_last) + lax.dot_general(vb, kt, (((0,), (0,)), ((), ())),
                                                preferred_element_type=F32)
    return o, st


def _hgrn_prompt_kernel(q_ref, z_ref, v_ref, og_ref, lb_ref, gn_ref, tri_ref, lvl_ref, mix_ref, o_ref, s_ref,
                        st_ref, *, tt):
    del mix_ref
    t = pl.program_id(1)

    @pl.when(t == 0)
    def _():
        st_ref[...] = jnp.zeros_like(st_ref)

    def body(j, carry):
        rows = pl.ds(pl.multiple_of(j * ROW_TILE, ROW_TILE), ROW_TILE)
        for h in range(HG_HEADS):
            cols = slice(h * LANES, (h + 1) * LANES)
            o, st = _hgrn_tile(q_ref[rows, cols], z_ref[rows, cols], v_ref[rows, cols], lb_ref[:, cols],
                               st_ref[h], tri_ref[...], lvl_ref[...])
            st_ref[h] = st
            o_ref[rows, cols] = (_rms(o, gn_ref[:, cols]) * _silu(og_ref[rows, cols])).astype(o_ref.dtype)
        return carry

    lax.fori_loop(0, tt // ROW_TILE, body, 0)

    @pl.when(t == pl.num_programs(1) - 1)
    def _():
        for h in range(HG_HEADS):
            s_ref[0, h] = st_ref[h].T


def _hgrn_prompt(proj, lb, hgrn_norm, mix, *, tt=256):
    nt = SEQ // tt
    width = HG_HEADS * LANES
    col0 = 3 * CONV_DIM // width
    blk = lambda part: pl.BlockSpec((tt, width), lambda b, t: (b * nt + t, col0 + part))
    hvec = pl.BlockSpec((1, width), lambda b, t: (0, 0))
    const = pl.BlockSpec((ROW_TILE, ROW_TILE), lambda b, t: (0, 0))
    tri =jnp.asarray(np.tril(np.ones((ROW_TILE, ROW_TILE), np.float32)), BF16)
    return pl.pallas_call(
        functools.partial(_hgrn_prompt_kernel, tt=tt),
        grid=(BATCH, nt),
        in_specs=[blk(0), blk(1), blk(2), blk(3), hvec, hvec, const, const, pl.BlockSpec(memory_space=pl.ANY)],
        out_specs=[pl.BlockSpec((tt, width), lambda b, t: (b * nt + t, 1)),
                   pl.BlockSpec((1, HG_HEADS, HG_DK, HG_DV), lambda b, t: (b, 0, 0, 0))],
        out_shape=[jax.ShapeDtypeStruct(mix.shape, mix.dtype),
                   jax.ShapeDtypeStruct((BATCH, HG_HEADS, HG_DK, HG_DV), F32)],
        scratch_shapes=[pltpu.VMEM((HG_HEADS, HG_DV, HG_DK), F32)],
        input_output_aliases={8: 0},
        compiler_params=_cparams(("arbitrary", "arbitrary")),
    )(proj, proj, proj, proj, lb.reshape(1, width), hgrn_norm.reshape(1, width), tri, _level_table(), mix)


def _conv_sample_kernel(hc_ref, bg_ref, cg_ref, cs_ref, cw_ref, gn_ref, sel_ref, selt_ref, mix_ref, y_ref, nc_ref):
    del mix_ref
    u = cg_ref[...] * hc_ref[...]
    s0 = cs_ref[:, 0, :]
    s1 = cs_ref[:, 1, :]
    conv = cw_ref[0:1] * s0 + cw_ref[1:2] * s1 + cw_ref[2:3] * u
    y_ref[...] = _group_rms(bg_ref[...] * conv, gn_ref[...], sel_ref[...], selt_ref[...]).astype(y_ref.dtype)
    nc_ref[:, 0, :] = s1
    nc_ref[:, 1, :] = u


def _conv_sample(proj, conv_state, conv_w, conv_norm, mix):
    rb = PROMPT_TILES
    cblk = lambda c: pl.BlockSpec((DEC_BATCH, CONV_DIM), lambda i: (rb, c))
    return pl.pallas_call(
        _conv_sample_kernel,
        grid=(1,),
        in_specs=[cblk(0), cblk(1), cblk(2),
                  pl.BlockSpec((DEC_BATCH, CONV_W - 1, CONV_DIM), lambda i: (0, 0, 0)),
                  pl.BlockSpec((CONV_W, CONV_DIM), lambda i: (0, 0)),
                  pl.BlockSpec((1, CONV_DIM), lambda i: (0, 0)),
                  pl.BlockSpec((CONV_DIM, LANES), lambda i: (0, 0)),
                  pl.BlockSpec((LANES, CONV_DIM), lambda i: (0, 0)),
                  pl.BlockSpec(memory_space=pl.ANY)],
        out_specs=[pl.BlockSpec((DEC_BATCH, CONV_DIM), lambda i: (rb, 0)),
                   pl.BlockSpec((DEC_BATCH, CONV_W - 1, CONV_DIM), lambda i: (0, 0, 0))],
        out_shape=[jax.ShapeDtypeStruct(mix.shape, mix.dtype),
                   jax.ShapeDtypeStruct((DEC_BATCH, CONV_W - 1, CONV_DIM), F32)],
        input_output_aliases={8: 0},
        compiler_params=_cparams(("arbitrary",)),
    )(proj, proj, proj, conv_state, conv_w, conv_norm, *_group_tables(), mix)


def _hgrn_sample_kernel(q_ref, z_ref, v_ref, og_ref, lb_ref, gn_ref, s_ref, *rest, bg, slab):
    o_ref, so_ref, osc_ref = rest[-3:]
    s_ref = s_ref.at[0]
    for other in range(so_ref.shape[0]):
        if other != slab:
            so_ref[other] = jnp.zeros(so_ref.shape[1:], so_ref.dtype)
    so_ref = so_ref.at[slab]
    g = pl.program_id(1)
    lb = lb_ref[0]
    logf, k = _gates(z_ref[...], lb)
    f = jnp.exp(logf)
    shift = (DEC_BATCH - g * bg) % DEC_BATCH
    ft = pltpu.roll(f.T, shift, 1)
    kt = pltpu.roll(k.T, shift, 1)
    qt = pltpu.roll(q_ref[...].T, shift, 1)
    rows = pl.ds(pl.multiple_of(g * bg, bg), bg)
    v = v_ref[rows, :]
    for j in range(bg):
        s_new = ft[:, j:j + 1] * s_ref[j, 0] + kt[:, j:j + 1] * v[j:j + 1, :]
        so_ref[j, 0] = s_new
        osc_ref[j:j + 1, :] = jnp.sum(qt[:, j:j + 1] * s_new, axis=0, keepdims=True)
    o = osc_ref[...]
    o_ref[...] = (_rms(o, gn_ref[0]) * _silu(og_ref[rows, :])).astype(o_ref.dtype)


def _hgrn_sample(proj, state_all, l, lb, hgrn_norm, mix, new_state_all=None, *, bg=16):
    rb = PROMPT_TILES
    col0 = 3 * CONV_DIM // LANES
    blk = lambda part: pl.BlockSpec((DEC_BATCH, LANES), lambda h, g: (rb, col0 + part * HG_HEADS + h))
    hvec = pl.BlockSpec((1, 1, LANES), lambda h, g: (h, 0, 0))
    sblk = pl.BlockSpec((1, bg, 1, HG_DK, HG_DV), lambda h, g: (l, g, h, 0, 0))
    any_spec = pl.BlockSpec(memory_space=pl.ANY)
    args = [proj, proj, proj, proj, lb.reshape(HG_HEADS, 1, HG_DK), hgrn_norm.reshape(HG_HEADS, 1, HG_DV),
            state_all, mix]
    in_specs = [blk(0), blk(1), blk(2), blk(3), hvec, hvec, sblk, any_spec]
    aliases = {7: 0}
    if new_state_all is not None:
        args.append(new_state_all)
        in_specs.append(any_spec)
        aliases[8] = 1
        so_blk, slab = sblk, 0
    else:
        n_slabs = state_all.shape[0]
        so_blk = pl.BlockSpec((n_slabs, bg, 1, HG_DK, HG_DV), lambda h, g: (0, g, h, 0, 0))
        slab = l
    return pl.pallas_call(
        functools.partial(_hgrn_sample_kernel, bg=bg, slab=slab),
        grid=(HG_HEADS, DEC_BATCH // bg),
        in_specs=in_specs,
        out_specs=[pl.BlockSpec((bg, LANES), lambda h, g: (N_PROMPT // bg + g, CONV_DIM // LANES + h)), so_blk],
        out_shape=[jax.ShapeDtypeStruct(mix.shape, mix.dtype),
                   jax.ShapeDtypeStruct(state_all.shape, F32)],
        scratch_shapes=[pltpu.VMEM((bg, HG_DV), F32)],
        input_output_aliases=aliases,
        compiler_params=_cparams(("arbitrary", "arbitrary")),
    )(*args)


def _route_tile(logits, ri_ref, gate_ref, cnt_ref, carry_ref):
    i = pl.program_id(0)

    @pl.when(i == 0)
    def _():
        carry_ref[...] = jnp.zeros_like(carry_ref)

    lane = lax.broadcasted_iota(I32, (ROW_TILE, LANES), 1)
    lanef = lane.astype(F32)
    lg = jnp.where(lane < N_EXPERTS, logits, -jnp.inf)
    m1 = jnp.max(lg, axis=-1, keepdims=True)
    i1 = jnp.min(jnp.where(lg == m1, lanef, float(LANES)), axis=-1, keepdims=True).astype(I32)
    lg2 = jnp.where(lane == i1, -jnp.inf, lg)
    m2 = jnp.max(lg2, axis=-1, keepdims=True)
    i2 = jnp.min(jnp.where(lg2 == m2, lanef, float(LANES)), axis=-1, keepdims=True).astype(I32)
    e = jnp.exp(m2 - m1)
    g1 = 1.0 / (1.0 + e)
    g2 = e / (1.0 + e)
    hot1 = lane == i1
    hot2 = lane == i2
    hot = (hot1 | hot2).astype(BF16)
    r = lax.broadcasted_iota(I32, (ROW_TILE, ROW_TILE), 0)
    c = lax.broadcasted_iota(I32, (ROW_TILE, ROW_TILE), 1)
    before = (c < r).astype(BF16)
    tot = jnp.dot(before, hot, preferred_element_type=F32) + carry_ref[...]
    r1 = jnp.sum(jnp.where(hot1, tot, 0.0), axis=-1, keepdims=True).astype(I32)
    r2 = jnp.sum(jnp.where(hot2, tot, 0.0), axis=-1, keepdims=True).astype(I32)
    ri_ref[...] = jnp.where(lane == 0, i1, jnp.where(lane == 1, i2, jnp.where(lane == 2, r1,
                            jnp.where(lane == 3, r2, 0))))
    gate_ref[...] = jnp.where(lane == 0, g1, jnp.where(lane == 1, g2, 0.0))
    carry_ref[...] += jnp.sum(hot.astype(F32), axis=0, keepdims=True)
    cnt_ref[...] = carry_ref[...]


def _gather_kernel(d1_ref, d2_ref, nact_ref, h_ref, xb_ref, tok_ref, buf_ref, sem):
    i = pl.program_id(0)

    nact = nact_ref[0]

    @pl.when(i == 0)
    def _():
        def clear(p, carry):
            tok_ref[p] = 0
            return carry
        lax.fori_loop(0, P_ROWS, clear, 0, unroll=8)

        def scatter(t, carry):
            tok_ref[d1_ref[t]] = t
            tok_ref[d2_ref[t]] = t
            return carry
        lax.fori_loop(0, N_ROWS, scatter, 0, unroll=4)

    def start_block(b):
        slot = b % 2

        def start(r, carry):
            tok = tok_ref[b * MOE_BLOCK + r]
            pltpu.make_async_copy(h_ref.at[pl.ds(tok, 1), :], buf_ref.at[slot, pl.ds(r, 1), :],
                                  sem.at[slot]).start()
            return carry
        lax.fori_loop(0, MOE_BLOCK, start, 0, unroll=8)

    @pl.when(i == 0)
    def _():
        start_block(i)

    @pl.when(i + 1 < nact)
    def _():
        start_block(i + 1)

    @pl.when(i < nact)
    def _():
        slot = i % 2
        pltpu.make_async_copy(h_ref.at[pl.ds(0, MOE_BLOCK), :], buf_ref.at[slot], sem.at[slot]).wait()
        xb_ref[...] = buf_ref[slot].astype(xb_ref.dtype)

    @pl.when(i >= nact)
    def _():
        xb_ref[...] = jnp.zeros_like(xb_ref)


def _gather(h, dest1, dest2, nact):
    return pl.pallas_call(
        _gather_kernel,
        grid_spec=pltpu.PrefetchScalarGridSpec(
            num_scalar_prefetch=3,
            grid=(N_BLOCKS,),
            in_specs=[pl.BlockSpec(memory_space=pl.ANY)],
            out_specs=pl.BlockSpec((MOE_BLOCK, D_MODEL), lambda i, d1, d2, na: (i, 0)),
            scratch_shapes=[pltpu.SMEM((P_ROWS,), I32), pltpu.VMEM((2, MOE_BLOCK, D_MODEL), F32),
                            pltpu.SemaphoreType.DMA((2,))],
        ),
        out_shape=jax.ShapeDtypeStruct((P_ROWS, D_MODEL), BF16),
        compiler_params=_cparams(("arbitrary",)),
    )(dest1, dest2, nact, h)


def _combine_resid_kernel(d1_ref, d2_ref, x_ref, gate_ref, gpost_ref, gap_ref, gas_ref, yb_ref, yp_ref, ys_ref,
                          b1_ref, b2_ref, sem):
    i = pl.program_id(0)

    def start_tile(t):
        slot = t % 2

        def start(r, carry):
            tok = t * ROW_TILE + r
            pltpu.make_async_copy(yb_ref.at[pl.ds(d1_ref[tok], 1), :], b1_ref.at[slot, pl.ds(r, 1), :],
                                  sem.at[0, slot]).start()
            pltpu.make_async_copy(yb_ref.at[pl.ds(d2_ref[tok], 1), :], b2_ref.at[slot, pl.ds(r, 1), :],
                                  sem.at[1, slot]).start()
            return carry
        lax.fori_loop(0, ROW_TILE, start, 0, unroll=8)

    @pl.when(i == 0)
    def _():
        start_tile(i)

    @pl.when(i + 1 < pl.num_programs(0))
    def _():
        start_tile(i + 1)

    slot = i % 2
    pltpu.make_async_copy(yb_ref.at[pl.ds(0, ROW_TILE), :], b1_ref.at[slot], sem.at[0, slot]).wait()
    pltpu.make_async_copy(yb_ref.at[pl.ds(0, ROW_TILE), :], b2_ref.at[slot], sem.at[1, slot]).wait()
    gate = gate_ref[...]
    f = gate[:, 0:1] * b1_ref[slot] + gate[:, 1:2] * b2_ref[slot]
    x = x_ref[...] + _pick(i, gap_ref, gas_ref) * _rms(f, gpost_ref[...])

    @pl.when(i < PROMPT_TILES)
    def _():
        yp_ref[...] = x

    @pl.when(i == PROMPT_TILES)
    def _():
        ys_ref[...] = x


def _combine_resid(x, yb, gates, dest1, dest2, gpost, ga):
    return pl.pallas_call(
        _combine_resid_kernel,
        grid_spec=pltpu.PrefetchScalarGridSpec(
            num_scalar_prefetch=2,
            grid=(N_TILES,),
            in_specs=[_row_spec(D_MODEL), _row_spec(LANES), _vec_spec(), _modp_spec(), _mods_spec(),
                      pl.BlockSpec(memory_space=pl.ANY)],
            out_specs=[_prompt_rows_spec(), _mods_spec()],
            scratch_shapes=[pltpu.VMEM((2, ROW_TILE, D_MODEL), F32), pltpu.VMEM((2, ROW_TILE, D_MODEL), F32),
                            pltpu.SemaphoreType.DMA((2, 2))],
        ),
        out_shape=[jax.ShapeDtypeStruct((N_PROMPT, D_MODEL), F32), jax.ShapeDtypeStruct((DEC_BATCH, D_MODEL), F32)],
        compiler_params=_cparams(("arbitrary",)),
    )(dest1, dest2, x, gates, gpost, ga[0], ga[1], yb)


def _moe_experts(h, ri, gates, counts, w1, w3, w2):
    counts = counts[0, :N_EXPERTS].astype(I32)
    padded = (counts + MOE_BLOCK - 1) // MOE_BLOCK * MOE_BLOCK
    pends = jnp.cumsum(padded)
    pstarts = pends - padded
    dest1 = pstarts[ri[:, 0]] + ri[:, 2]
    dest2 = pstarts[ri[:, 1]] + ri[:, 3]
    nact = (pends[-1:] // MOE_BLOCK).astype(I32)
    block_start = jnp.arange(N_BLOCKS, dtype=I32) * MOE_BLOCK
    block_e = jnp.minimum(jnp.sum((pends[None, :] <= block_start[:, None]).astype(I32), axis=1), N_EXPERTS - 1)
    run_end = (pends // MOE_BLOCK)[block_e]
    next_e = jnp.where(run_end < nact[0], block_e[jnp.minimum(run_end, N_BLOCKS - 1)], -1).astype(I32)
    xb = _gather(h, dest1, dest2, nact)
    gb = _emm(xb, (w1, w3), block_e, nact, next_e, tm=MOE_BLOCK, tn=1792, out_dtype=BF16)
    yb = _emm(gb, (w2,), block_e, nact, next_e, tm=MOE_BLOCK, tn=512, out_dtype=F32)
    return yb, gates, dest1, dest2


def _dense(a, ws, e, *, tm, tn, out_dtype):
    nb = a.shape[0] // tm
    return _emm(a, ws, jnp.full((nb,), e, I32), jnp.full((1,), nb, I32), jnp.full((nb,), -1, I32),
                tm=tm, tn=tn, out_dtype=out_dtype)


def kernel(x_prompt, x_sample, state_conv, state_hgrn, c_prompt, c_sample, norm_pre, norm_post, w_mod, b_mod, w_in, conv_w, conv_norm, lb_logits, hgrn_norm, w_out, ffn_w1, ffn_w3, ffn_w2, router_w, router_b, moe_w1, moe_w3, moe_w2):
    p = jax.nn.softmax(lb_logits.astype(F32), axis=0)
    lb_all = jnp.cumsum(p, axis=0) - p[0:1]

    n_cond = BATCH + DEC_BATCH
    cond_rows = (n_cond + SUBLANES - 1) // SUBLANES * SUBLANES
    c_all = jnp.concatenate([c_prompt, c_sample, jnp.zeros((cond_rows - n_cond, D_MODEL), F32)], axis=0)
    mod = _gmm(jnp.concatenate([c_all] * DEPTH, axis=0), w_mod, jnp.arange(DEPTH, dtype=I32),
               jnp.full((1,), DEPTH, I32), tm=cond_rows, tn=1024, out_dtype=F32,
               bias=b_mod.reshape(DEPTH, 1, 6 * D_MODEL), lhs_silu=True)

    def mod_vec(l, j):
        rows = mod[l * cond_rows:l * cond_rows + n_cond, j * D_MODEL:(j + 1) * D_MODEL]
        return rows[:BATCH].reshape(BATCH, 1, D_MODEL), rows[BATCH:]

    x = (x_prompt.reshape(N_PROMPT, D_MODEL), x_sample.reshape(DEC_BATCH, D_MODEL))
    vec = lambda a: a.reshape(1, -1)

    h = _prenorm(x[0], x[1], vec(norm_pre[0, 0]), mod_vec(0, 1), mod_vec(0, 0))
    new_conv_p, new_hgrn_p, new_conv_s, new_hgrn_s = [], [], [], None
    for l in range(DEPTH):
        sh_f, sc_f, ga_f = mod_vec(l, 3), mod_vec(l, 4), mod_vec(l, 5)
        ga_a = mod_vec(l, 2)
        proj = _dense(h, (w_in,), l, tm=1664, tn=1024, out_dtype=F32)
        mix = jnp.zeros((N_ROWS, 2 * CONV_DIM), BF16)
        mix, nc_p = _conv_prompt(proj, conv_w[l], vec(conv_norm[l]), mix)
        mix, ns_p = _hgrn_prompt(proj, lb_all[l], hgrn_norm[l], mix)
        mix, nc_s = _conv_sample(proj, state_conv[l], conv_w[l], vec(conv_norm[l]), mix)
        mix, new_hgrn_s = _hgrn_sample(proj, state_hgrn, l, lb_all[l], hgrn_norm[l], mix, new_hgrn_s)
        mixed = _dense(mix, (w_out,), l, tm=1664, tn=1024, out_dtype=F32)
        new_conv_p.append(nc_p)
        new_hgrn_p.append(ns_p)
        new_conv_s.append(nc_s)
        nxt = (vec(norm_pre[l, 1]), sc_f, sh_f)
        if l % 2 == 0:
            j = l // 2
            x, h2 = _resid(x, mixed, vec(norm_post[l, 0]), ga_a, nxt=nxt)
            g = _dense(h2, (ffn_w1, ffn_w3), j, tm=1664, tn=512, out_dtype=BF16)
            f = _dense(g, (ffn_w2,), j, tm=1040, tn=512, out_dtype=F32)
        else:
            j = l // 2
            rw = jnp.pad(router_w[j].astype(F32), ((0, 0), (0, LANES - N_EXPERTS)))
            rb = jnp.pad(router_b[j].astype(F32), (0, LANES - N_EXPERTS)).reshape(1, LANES)
            x, h2, ri, gates, counts = _resid(x, mixed, vec(norm_post[l, 0]), ga_a, nxt=nxt, router=(rw, rb),
                                              h_dtype=F32)
            f = _moe_experts(h2, ri, gates, counts, moe_w1[j], moe_w3[j], moe_w2[j])
        if l + 1 < DEPTH:
            assert l % 2 == 0, "the expert combine is fused with the trunk's last residual step only"
            nxt = (vec(norm_pre[l + 1, 0]), mod_vec(l + 1, 1), mod_vec(l + 1, 0))
            x, h = _resid(x, f, vec(norm_post[l, 1]), ga_f, nxt=nxt)
        elif l % 2 == 0:
            y_prompt, y_sample = _resid(x, f, vec(norm_post[l, 1]), ga_f, y_split=True)
        else:
            y_prompt, y_sample = _combine_resid(x, *f, vec(norm_post[l, 1]), ga_f)

    y_prompt = y_prompt.reshape(BATCH, SEQ, D_MODEL)
    y_sample = y_sample.reshape(DEC_BATCH, 1, D_MODEL)
    return (y_prompt, y_sample, jnp.stack(new_conv_p), jnp.stack(new_hgrn_p),
            jnp.stack(new_conv_s), new_hgrn_s)
```

```python
import functools

import jax
import jax.numpy as jnp
import numpy as np
from jax import lax
from jax.experimental import pallas as pl
from jax.experimental.pallas import tpu as pltpu

F32 = jnp.float32
BF16 = jnp.bfloat16
I32 = jnp.int32
HIGHEST = lax.Precision.HIGHEST

LANES = 128
SUBLANES = 8
VMEM_LIMIT_BYTES = 56 * 1024 * 1024

D_MODEL = 2048
BATCH = 4
SEQ = 2048
DEPTH = 2
DEC_BATCH = 128
N_PROMPT = BATCH * SEQ
N_ROWS = N_PROMPT + DEC_BATCH
CONV_DIM = 1024
CONV_GROUPS = 16
CONV_W = 3
HG_HEADS = 8
HG_DK = 128
HG_DV = 128
PROJ_WIDTH = 7168
N_EXPERTS = 8
EPS = 1e-6
F_MIN = 1e-6

ROW_TILE = 128
N_TILES = N_ROWS // ROW_TILE
PROMPT_TILES = N_PROMPT // ROW_TILE
TILES_PER_SEQ = SEQ // ROW_TILE
MOE_BLOCK = 256
N_PAIRS = 2 * N_ROWS
N_BLOCKS = (N_PAIRS + N_EXPERTS * (MOE_BLOCK - 1) + MOE_BLOCK - 1) // MOE_BLOCK
P_ROWS = N_BLOCKS * MOE_BLOCK


def _cparams(sem):
    return pltpu.CompilerParams(dimension_semantics=sem, vmem_limit_bytes=VMEM_LIMIT_BYTES)


def _rms(x, g):
    return x * lax.rsqrt(jnp.mean(x * x, axis=-1, keepdims=True) + EPS) * g


def _silu(x):
    return x * jax.nn.sigmoid(x)


def _gmm_kernel(be_ref, nact_ref, a_ref, w_ref, *rest, lhs_silu, has_bias):
    if has_bias:
        b_ref, o_ref, wb_ref = rest
    else:
        o_ref, wb_ref = rest
    i = pl.program_id(1)
    prev = be_ref[jnp.maximum(i - 1, 0)]

    @pl.when((i == 0) | (be_ref[i] != prev))
    def _():
        wb_ref[...] = w_ref[0].astype(BF16)

    @pl.when(i < nact_ref[0])
    def _():
        a = a_ref[...]
        if lhs_silu:
            a = _silu(a)
        acc = jnp.dot(a.astype(BF16), wb_ref[...], preferred_element_type=F32)
        if has_bias:
            acc = acc + b_ref[0]
        o_ref[...] = acc.astype(o_ref.dtype)

    @pl.when(i >= nact_ref[0])
    def _():
        o_ref[...] = jnp.zeros_like(o_ref)


def _gmm(a, w, block_e, nact, *, tm, tn, out_dtype, bias=None, lhs_silu=False):
    m, k = a.shape
    _, _, n = w.shape
    grid = (n // tn, m // tm)
    in_specs = [
        pl.BlockSpec((tm, k), lambda j, i, be, na: (i, 0)),
        pl.BlockSpec((1, k, tn), lambda j, i, be, na: (be[i], 0, j)),
    ]
    args = [a, w]
    if bias is not None:
        in_specs.append(pl.BlockSpec((1, 1, tn), lambda j, i, be, na: (be[i], 0, j)))
        args.append(bias)
    return pl.pallas_call(
        functools.partial(_gmm_kernel, lhs_silu=lhs_silu, has_bias=bias is not None),
        grid_spec=pltpu.PrefetchScalarGridSpec(
            num_scalar_prefetch=2,
            grid=grid,
            in_specs=in_specs,
            out_specs=pl.BlockSpec((tm, tn), lambda j, i, be, na: (i, j)),
            scratch_shapes=[pltpu.VMEM((k, tn), BF16)],
        ),
        out_shape=jax.ShapeDtypeStruct((m, n), out_dtype),
        compiler_params=_cparams(("arbitrary", "arbitrary")),
    )(block_e, nact, *args)


CAST_ROWS = 32


def _emm_kernel(be_ref, nact_ref, nexte_ref, a_hbm, *rest, n_w, tm, nb):
    w_hbm = rest[:n_w]
    o_hbm, abuf, obuf, stage_ref, wb_ref, asem, osem, wsem = rest[n_w:]
    j = pl.program_id(0)
    last_tile = j + 1 == pl.num_programs(0)
    tn = obuf.shape[2]
    nact = nact_ref[0]
    base = (j * nact) % 2

    def a_copy(i, slot):
        rows = pl.ds(pl.multiple_of(i * tm, tm), tm)
        return pltpu.make_async_copy(a_hbm.at[rows, :], abuf.at[slot], asem.at[slot])

    def o_copy(i, slot):
        rows = pl.ds(pl.multiple_of(i * tm, tm), tm)
        cols = pl.ds(pl.multiple_of(j * tn, LANES), tn)
        return pltpu.make_async_copy(obuf.at[slot], o_hbm.at[rows, cols], osem.at[slot])

    def slab_copies(expert, col_tile):
        cols = pl.ds(pl.multiple_of(col_tile * tn, LANES), tn)
        return [pltpu.make_async_copy(w_hbm[t].at[expert, :, cols], stage_ref.at[t], wsem.at[t])
                for t in range(n_w)]

    @pl.when(j == 0)
    def _():
        for c in slab_copies(be_ref[0], j):
            c.start()
        a_copy(0, base).start()

    def block(i, carry):
        slot = (base + i) % 2
        e = be_ref[i]

        @pl.when(i + 1 < nact)
        def _():
            a_copy(i + 1, 1 - slot).start()

        @pl.when((i + 1 == nact) & jnp.logical_not(last_tile))
        def _():
            a_copy(0, 1 - slot).start()

        @pl.when((i == 0) | (e != be_ref[jnp.maximum(i - 1, 0)]))
        def _():
            for c in slab_copies(e, j):
                c.wait()

            def round_rows(c, carry):
                rows = pl.ds(pl.multiple_of(c * CAST_ROWS, CAST_ROWS), CAST_ROWS)
                for t in range(n_w):
                    wb_ref[t, rows, :] = stage_ref[t, rows, :].astype(BF16)
                return carry
            lax.fori_loop(0, stage_ref.shape[1] // CAST_ROWS, round_rows, 0)
            nxt = nexte_ref[i]

            @pl.when(nxt >= 0)
            def _():
                for c in slab_copies(nxt, j):
                    c.start()

            @pl.when((nxt < 0) & jnp.logical_not(last_tile))
            def _():
                for c in slab_copies(be_ref[0], j + 1):
                    c.start()

        a_copy(i, slot).wait()

        @pl.when(i >= 2)
        def _():
            o_copy(i - 2, slot).wait()

        a = abuf[slot]
        acc = jnp.dot(a, wb_ref[0], preferred_element_type=F32)
        if n_w == 2:
            acc = _silu(acc) * jnp.dot(a, wb_ref[1], preferred_element_type=F32)
        obuf[slot] = acc.astype(obuf.dtype)
        o_copy(i, slot).start()
        return carry

    lax.fori_loop(0, nact, block, 0)

    @pl.when(nact >= 2)
    def _():
        o_copy(nact - 2, (base + nact) % 2).wait()
    o_copy(nact - 1, (base + nact - 1) % 2).wait()

    @pl.when(nact < nb)
    def _():
        obuf[0] = jnp.zeros(obuf.shape[1:], obuf.dtype)

        def zero_block(i, carry):
            c = o_copy(i, 0)
            c.start()
            c.wait()
            return carry
        lax.fori_loop(nact, nb, zero_block, 0)


def _emm(a, ws, block_e, nact, next_e, *, tm, tn, out_dtype):
    m, k = a.shape
    n = ws[0].shape[2]
    n_w = len(ws)
    any_spec = pl.BlockSpec(memory_space=pl.ANY)
    return pl.pallas_call(
        functools.partial(_emm_kernel, n_w=n_w, tm=tm, nb=m // tm),
        grid_spec=pltpu.PrefetchScalarGridSpec(
            num_scalar_prefetch=3,
            grid=(n // tn,),
            in_specs=[any_spec] * (1 + n_w),
            out_specs=any_spec,
            scratch_shapes=[pltpu.VMEM((2, tm, k), BF16), pltpu.VMEM((2, tm, tn), out_dtype),
                            pltpu.VMEM((n_w, k, tn), F32), pltpu.VMEM((n_w, k, tn), BF16),
                            pltpu.SemaphoreType.DMA((2,)), pltpu.SemaphoreType.DMA((2,)),
                            pltpu.SemaphoreType.DMA((n_w,))],
        ),
        out_shape=jax.ShapeDtypeStruct((m, n), out_dtype),
        compiler_params=_cparams(("arbitrary",)),
    )(block_e, nact, next_e, a, *ws)


ELT_TILE = 256


def _rep_rows(s, tile):
    return s if tile == DEC_BATCH else jnp.concatenate([s] * (tile // DEC_BATCH), axis=0)


def _pick(i, p_ref, s_ref, tile=ROW_TILE):
    return jnp.where(i < N_PROMPT // tile, p_ref[0], _rep_rows(s_ref[...], tile))


def _prenorm_kernel(xp_ref, xs_ref, g_ref, scp_ref, scs_ref, shp_ref, shs_ref, h_ref):
    i = pl.program_id(0)
    tile = h_ref.shape[0]
    x = jnp.where(i < N_PROMPT // tile, xp_ref[...], _rep_rows(xs_ref[...], tile))
    sc = _pick(i, scp_ref, scs_ref, tile)
    sh = _pick(i, shp_ref, shs_ref, tile)
    h_ref[...] = (_rms(x, g_ref[...]) * (1.0 + sc) + sh).astype(h_ref.dtype)


def _resid_kernel(*refs, x_split, y_split, with_next, with_router, h_dtype):
    it = iter(refs)
    if x_split:
        xp_ref, xs_ref = next(it), next(it)
    else:
        x_ref = next(it)
    f_ref, gpost_ref, gap_ref, gas_ref = [next(it) for _ in range(4)]
    if with_next:
        gpre_ref, scp_ref, scs_ref, shp_ref, shs_ref = [next(it) for _ in range(5)]
    if with_router:
        rw_ref, rb_ref = next(it), next(it)
    if y_split:
        yp_ref, ys_ref = next(it), next(it)
    else:
        xo_ref = next(it)
    if with_next:
        h_ref = next(it)
    if with_router:
        ri_ref, gate_ref, cnt_ref, carry_ref = [next(it) for _ in range(4)]
    i = pl.program_id(0)
    tile = f_ref.shape[0]
    prompt_tiles = N_PROMPT // tile
    ga = _pick(i, gap_ref, gas_ref, tile)
    if x_split:
        x = jnp.where(i < prompt_tiles, xp_ref[...], _rep_rows(xs_ref[...], tile))
    else:
        x = x_ref[...]
    x = x + ga * _rms(f_ref[...], gpost_ref[...])
    if y_split:
        @pl.when(i < prompt_tiles)
        def _():
            yp_ref[...] = x

        @pl.when(i == prompt_tiles)
        def _():
            ys_ref[...] = x[:DEC_BATCH]
    else:
        xo_ref[...] = x
    if with_next:
        sc = _pick(i, scp_ref, scs_ref, tile)
        sh = _pick(i, shp_ref, shs_ref, tile)
        h = _rms(x, gpre_ref[...]) * (1.0 + sc) + sh
        h_ref[...] = h.astype(h_dtype)
        if with_router:
            logits = jnp.dot(h, rw_ref[...], precision=HIGHEST, preferred_element_type=F32) + rb_ref[...]
            _route_tile(logits, ri_ref, gate_ref, cnt_ref, carry_ref)


def _n_tiles(tile):
    return N_PROMPT // tile + 1


def _row_spec(width, tile=ROW_TILE):
    return pl.BlockSpec((tile, width), lambda i, *_: (i, 0))


def _vec_spec():
    return pl.BlockSpec((1, D_MODEL), lambda i, *_: (0, 0))


def _modp_spec(tile=ROW_TILE):
    return pl.BlockSpec((1, 1, D_MODEL), lambda i, *_: (jnp.minimum(i // (SEQ // tile), BATCH - 1), 0, 0))


def _mods_spec():
    return pl.BlockSpec((DEC_BATCH, D_MODEL), lambda i, *_: (0, 0))


def _prompt_rows_spec(tile=ROW_TILE):
    return pl.BlockSpec((tile, D_MODEL), lambda i, *_: (jnp.minimum(i, N_PROMPT // tile - 1), 0))


def _prenorm(xp, xs, g, sc, sh):
    t = ELT_TILE
    return pl.pallas_call(
        _prenorm_kernel,
        grid=(_n_tiles(t),),
        in_specs=[_prompt_rows_spec(t), _mods_spec(), _vec_spec(), _modp_spec(t), _mods_spec(), _modp_spec(t),
                  _mods_spec()],
        out_specs=_row_spec(D_MODEL, t),
        out_shape=jax.ShapeDtypeStruct((N_ROWS, D_MODEL), BF16),
        compiler_params=_cparams(("arbitrary",)),
    )(xp, xs, g, sc[0], sc[1], sh[0], sh[1])


def _resid(x, f, gpost, ga, nxt=None, router=None, h_dtype=BF16, y_split=False):
    t = ELT_TILE
    x_split = isinstance(x, tuple)
    if x_split:
        args = [x[0], x[1]]
        in_specs = [_prompt_rows_spec(t), _mods_spec()]
    else:
        args = [x]
        in_specs = [_row_spec(D_MODEL, t)]
    args += [f, gpost, ga[0], ga[1]]
    in_specs += [_row_spec(D_MODEL, t), _vec_spec(), _modp_spec(t), _mods_spec()]
    if y_split:
        out_shape = [jax.ShapeDtypeStruct((N_PROMPT, D_MODEL), F32), jax.ShapeDtypeStruct((DEC_BATCH, D_MODEL), F32)]
        out_specs = [_prompt_rows_spec(t), _mods_spec()]
    else:
        out_shape = [jax.ShapeDtypeStruct((N_ROWS, D_MODEL), F32)]
        out_specs = [_row_spec(D_MODEL, t)]
    if nxt is not None:
        gpre, sc, sh = nxt
        args += [gpre, sc[0], sc[1], sh[0], sh[1]]
        in_specs += [_vec_spec(), _modp_spec(t), _mods_spec(), _modp_spec(t), _mods_spec()]
        out_shape.append(jax.ShapeDtypeStruct((N_ROWS, D_MODEL), h_dtype))
        out_specs.append(_row_spec(D_MODEL, t))
    if router is not None:
        rw, rb = router
        args += [rw, rb]
        in_specs += [pl.BlockSpec((D_MODEL, LANES), lambda i: (0, 0)), pl.BlockSpec((1, LANES), lambda i: (0, 0))]
        out_shape += [jax.ShapeDtypeStruct((N_ROWS, LANES), I32), jax.ShapeDtypeStruct((N_ROWS, LANES), F32),
                      jax.ShapeDtypeStruct((1, LANES), F32)]
        out_specs += [_row_spec(LANES, t), _row_spec(LANES, t), pl.BlockSpec((1, LANES), lambda i: (0, 0))]
        scratch = [pltpu.VMEM((1, LANES), F32)]
    else:
        scratch = []
    return pl.pallas_call(
        functools.partial(_resid_kernel, x_split=x_split, y_split=y_split, with_next=nxt is not None,
                          with_router=router is not None, h_dtype=h_dtype),
        grid=(_n_tiles(t),),
        in_specs=in_specs,
        out_specs=out_specs,
        out_shape=out_shape,
        scratch_shapes=scratch,
        compiler_params=_cparams(("arbitrary",)),
    )(*args)


def _group_tables():
    grp = np.arange(CONV_DIM)[:, None] // (CONV_DIM // CONV_GROUPS) == np.arange(LANES)[None, :]
    return jnp.asarray(grp, BF16), jnp.asarray(grp.T, BF16)


def _split2(x):
    hi = x.astype(BF16)
    return hi, (x - hi.astype(F32)).astype(BF16)


def _group_rms(y, gnorm, sel, sel_t):
    width = CONV_DIM // CONV_GROUPS
    hi, lo = _split2(y * y)
    ss = jnp.dot(hi, sel, preferred_element_type=F32) + jnp.dot(lo, sel, preferred_element_type=F32)
    hi, lo = _split2(lax.rsqrt(ss * (1.0 / width) + EPS))
    scale = jnp.dot(hi, sel_t, preferred_element_type=F32) + jnp.dot(lo, sel_t, preferred_element_type=F32)
    return y * scale * gnorm


def _conv_prompt_kernel(hc_ref, bg_ref, cg_ref, hch_ref, cgh_ref, cw_ref, gn_ref, sel_ref, selt_ref, mix_ref,
                        y_ref, nc_ref, *, tt):
    del mix_ref
    t = pl.program_id(1)
    u = cg_ref[...] * hc_ref[...]
    halo = jnp.where(t == 0, 0.0, cgh_ref[...] * hch_ref[...])
    h1 = halo[SUBLANES - 1:SUBLANES]
    h2 = halo[SUBLANES - 2:SUBLANES - 1]
    row = lax.broadcasted_iota(I32, u.shape, 0)
    u1 = jnp.where(row == 0, h1, pltpu.roll(u, 1, 0))
    u2 = jnp.where(row == 0, h2, jnp.where(row == 1, h1, pltpu.roll(u, 2, 0)))
    conv = cw_ref[0:1] * u2 + cw_ref[1:2] * u1 + cw_ref[2:3] * u
    y_ref[...] = _group_rms(bg_ref[...] * conv, gn_ref[...], sel_ref[...], selt_ref[...]).astype(y_ref.dtype)

    @pl.when(t == pl.num_programs(1) - 1)
    def _():
        nc_ref[0] = u[tt - (CONV_W - 1):]


def _conv_prompt(proj, conv_w, conv_norm, mix, *, tt=256):
    nt = SEQ // tt
    cblk = lambda c: pl.BlockSpec((tt, CONV_DIM), lambda b, t: (b * nt + t, c))
    hblk = lambda c: pl.BlockSpec(
        (SUBLANES, CONV_DIM), lambda b, t: (jnp.maximum((b * nt + t) * (tt // SUBLANES) - 1, 0), c))
    return pl.pallas_call(
        functools.partial(_conv_prompt_kernel, tt=tt),
        grid=(BATCH, nt),
        in_specs=[cblk(0), cblk(1), cblk(2), hblk(0), hblk(2),
                  pl.BlockSpec((CONV_W, CONV_DIM), lambda b, t: (0, 0)),
                  pl.BlockSpec((1, CONV_DIM), lambda b, t: (0, 0)),
                  pl.BlockSpec((CONV_DIM, LANES), lambda b, t: (0, 0)),
                  pl.BlockSpec((LANES, CONV_DIM), lambda b, t: (0, 0)),
                  pl.BlockSpec(memory_space=pl.ANY)],
        out_specs=[pl.BlockSpec((tt, CONV_DIM), lambda b, t: (b * nt + t, 0)),
                   pl.BlockSpec((1, CONV_W - 1, CONV_DIM), lambda b, t: (b, 0, 0))],
        out_shape=[jax.ShapeDtypeStruct(mix.shape, mix.dtype),
                   jax.ShapeDtypeStruct((BATCH, CONV_W - 1, CONV_DIM), F32)],
        input_output_aliases={9: 0},
        compiler_params=_cparams(("arbitrary", "arbitrary")),
    )(proj, proj, proj, proj, proj, conv_w, conv_norm, *_group_tables(), mix)


HG_LEVELS = tuple(1 << i for i in range(ROW_TILE.bit_length() - 1))
LOG2_E = 1.4426950408889634


def _gates(z, lb):
    sg = jax.nn.sigmoid(z)
    f = lb + (1.0 - lb) * sg
    return jnp.log(jnp.maximum(f, F_MIN)), (1.0 - lb) * (1.0 - sg)


def _level_table():
    t = np.arange(ROW_TILE)[:, None]
    s = np.arange(ROW_TILE)[None, :]
    x = np.maximum(t ^ s, 1)
    lvl = np.floor(np.log2(x)).astype(np.int32)
    lvl = np.where(t == s, len(HG_LEVELS), np.where(s < t, lvl, -1))
    return jnp.asarray(lvl, I32)


def _nt_dot(x, y):
    return lax.dot_general(x, y, (((1,), (1,)), ((), ())), preferred_element_type=F32)


def _hgrn_tile(q, z, v, lb, st, tri, lvl):
    logf, k = _gates(z, lb)
    lf2 = logf * LOG2_E
    hi = lf2.astype(BF16)
    rem = lf2 - hi.astype(F32)
    mid = rem.astype(BF16)
    lo = (rem - mid.astype(F32)).astype(BF16)
    parts = jnp.dot(tri, jnp.concatenate([hi, mid, lo], axis=1), preferred_element_type=F32)
    a = parts[:, :HG_DK] + parts[:, HG_DK:2 * HG_DK] + parts[:, 2 * HG_DK:]
    row = lax.broadcasted_iota(I32, (ROW_TILE, HG_DK), 0)
    a8 = a.reshape(ROW_TILE // SUBLANES, SUBLANES, HG_DK)
    sub8 = lax.broadcasted_iota(I32, a8.shape, 1)
    scores = jnp.where(lvl == len(HG_LEVELS), _nt_dot(q.astype(BF16), k.astype(BF16)), 0.0)
    for li, c in enumerate(HG_LEVELS):
        if c < SUBLANES:
            if c == 1:
                d = jnp.where((row & 1) == 1, lf2, 0.0)
            elif c == 2:
                anchor = jnp.where(sub8 < 4, a8[:, 1:2, :], a8[:, 5:6, :])
                d = (a8 - anchor).reshape(ROW_TILE, HG_DK)
            else:
                d = (a8 - a8[:, c - 1:c, :]).reshape(ROW_TILE, HG_DK)
            src = jnp.where((row & c) != 0, q, k)
            neg = -jnp.abs(d)
        else:
            shape3 = (ROW_TILE // (2 * c), 2 * c, HG_DK)
            ab, qb, kb = a.reshape(shape3), q.reshape(shape3), k.reshape(shape3)
            anchor = ab[:, c - 1:c, :]
            neg = jnp.concatenate([anchor - ab[:, :c, :], ab[:, c:, :] - anchor], axis=1).reshape(ROW_TILE, HG_DK)
            src = jnp.concatenate([kb[:, :c, :], qb[:, c:, :]], axis=1).reshape(ROW_TILE, HG_DK)
        x = (src * jnp.exp2(neg)).astype(BF16)
        scores = jnp.where(lvl == li, _nt_dot(x, x), scores)
    vb = v.astype(BF16)
    a_last = a[ROW_TILE - 1:ROW_TILE, :]
    o = jnp.dot(scores.astype(BF16), vb, preferred_element_type=F32)
    o = o + _nt_dot((q * jnp.exp2(a)).astype(BF16), st.astype(BF16))
    kt = (k * jnp.exp2(a_last - a)).astype(BF16)
    st = st * jnp.exp2(a_last) + lax.dot_general(vb, kt, (((0,), (0,)), ((), ())),
                                                 preferred_element_type=F32)
    return o, st


def _hgrn_prompt_kernel(q_ref, z_ref, v_ref, og_ref, lb_ref, gn_ref, tri_ref, lvl_ref, mix_ref, o_ref, s_ref,
                        st_ref, *, tt):
    del mix_ref
    t = pl.program_id(1)

    @pl.when(t == 0)
    def _():
        st_ref[...] = jnp.zeros_like(st_ref)

    def body(j, carry):
        rows = pl.ds(pl.multiple_of(j * ROW_TILE, ROW_TILE), ROW_TILE)
        for h in range(HG_HEADS):
            cols = slice(h * LANES, (h + 1) * LANES)
            o, st = _hgrn_tile(q_ref[rows, cols], z_ref[rows, cols], v_ref[rows, cols], lb_ref[:, cols],
                               st_ref[h], tri_ref[...], lvl_ref[...])
            st_ref[h] = st
            o_ref[rows, cols] = (_rms(o, gn_ref[:, cols]) * _silu(og_ref[rows, cols])).astype(o_ref.dtype)
        return carry

    lax.fori_loop(0, tt // ROW_TILE, body, 0)

    @pl.when(t == pl.num_programs(1) - 1)
    def _():
        for h in range(HG_HEADS):
            s_ref[0, h] = st_ref[h].T


def _hgrn_prompt(proj, lb, hgrn_norm, mix, *, tt=256):
    nt = SEQ // tt
    width = HG_HEADS * LANES
    col0 = 3 * CONV_DIM // width
    blk = lambda part: pl.BlockSpec((tt, width), lambda b, t: (b * nt + t, col0 + part))
    hvec = pl.BlockSpec((1, width), lambda b, t: (0, 0))
    const = pl.BlockSpec((ROW_TILE, ROW_TILE), lambda b, t: (0, 0))
    tri =jnp.asarray(np.tril(np.ones((ROW_TILE, ROW_TILE), np.float32)), BF16)
    return pl.pallas_call(
        functools.partial(_hgrn_prompt_kernel, tt=tt),
        grid=(BATCH, nt),
        in_specs=[blk(0), blk(1), blk(2), blk(3), hvec, hvec, const, const, pl.BlockSpec(memory_space=pl.ANY)],
        out_specs=[pl.BlockSpec((tt, width), lambda b, t: (b * nt + t, 1)),
                   pl.BlockSpec((1, HG_HEADS, HG_DK, HG_DV), lambda b, t: (b, 0, 0, 0))],
        out_shape=[jax.ShapeDtypeStruct(mix.shape, mix.dtype),
                   jax.ShapeDtypeStruct((BATCH, HG_HEADS, HG_DK, HG_DV), F32)],
        scratch_shapes=[pltpu.VMEM((HG_HEADS, HG_DV, HG_DK), F32)],
        input_output_aliases={8: 0},
        compiler_params=_cparams(("arbitrary", "arbitrary")),
    )(proj, proj, proj, proj, lb.reshape(1, width), hgrn_norm.reshape(1, width), tri, _level_table(), mix)


def _conv_sample_kernel(hc_ref, bg_ref, cg_ref, cs_ref, cw_ref, gn_ref, sel_ref, selt_ref, mix_ref, y_ref, nc_ref):
    del mix_ref
    u = cg_ref[...] * hc_ref[...]
    s0 = cs_ref[:, 0, :]
    s1 = cs_ref[:, 1, :]
    conv = cw_ref[0:1] * s0 + cw_ref[1:2] * s1 + cw_ref[2:3] * u
    y_ref[...] = _group_rms(bg_ref[...] * conv, gn_ref[...], sel_ref[...], selt_ref[...]).astype(y_ref.dtype)
    nc_ref[:, 0, :] = s1
    nc_ref[:, 1, :] = u


def _conv_sample(proj, conv_state, conv_w, conv_norm, mix):
    rb = PROMPT_TILES
    cblk = lambda c: pl.BlockSpec((DEC_BATCH, CONV_DIM), lambda i: (rb, c))
    return pl.pallas_call(
        _conv_sample_kernel,
        grid=(1,),
        in_specs=[cblk(0), cblk(1), cblk(2),
                  pl.BlockSpec((DEC_BATCH, CONV_W - 1, CONV_DIM), lambda i: (0, 0, 0)),
                  pl.BlockSpec((CONV_W, CONV_DIM), lambda i: (0, 0)),
                  pl.BlockSpec((1, CONV_DIM), lambda i: (0, 0)),
                  pl.BlockSpec((CONV_DIM, LANES), lambda i: (0, 0)),
                  pl.BlockSpec((LANES, CONV_DIM), lambda i: (0, 0)),
                  pl.BlockSpec(memory_space=pl.ANY)],
        out_specs=[pl.BlockSpec((DEC_BATCH, CONV_DIM), lambda i: (rb, 0)),
                   pl.BlockSpec((DEC_BATCH, CONV_W - 1, CONV_DIM), lambda i: (0, 0, 0))],
        out_shape=[jax.ShapeDtypeStruct(mix.shape, mix.dtype),
                   jax.ShapeDtypeStruct((DEC_BATCH, CONV_W - 1, CONV_DIM), F32)],
        input_output_aliases={8: 0},
        compiler_params=_cparams(("arbitrary",)),
    )(proj, proj, proj, conv_state, conv_w, conv_norm, *_group_tables(), mix)


def _hgrn_sample_kernel(q_ref, z_ref, v_ref, og_ref, lb_ref, gn_ref, s_ref, *rest, bg, slab):
    o_ref, so_ref, osc_ref = rest[-3:]
    s_ref = s_ref.at[0]
    for other in range(so_ref.shape[0]):
        if other != slab:
            so_ref[other] = jnp.zeros(so_ref.shape[1:], so_ref.dtype)
    so_ref = so_ref.at[slab]
    g = pl.program_id(1)
    lb = lb_ref[0]
    logf, k = _gates(z_ref[...], lb)
    f = jnp.exp(logf)
    shift = (DEC_BATCH - g * bg) % DEC_BATCH
    ft = pltpu.roll(f.T, shift, 1)
    kt = pltpu.roll(k.T, shift, 1)
    rows = pl.ds(pl.multiple_of(g * bg, bg), bg)
    v = v_ref[rows, :]
    q = q_ref[rows, :]
    for j in range(bg):
        s_new = ft[:, j:j + 1] * s_ref[j, 0] + kt[:, j:j + 1] * v[j:j + 1, :]
        so_ref[j, 0] = s_new
        osc_ref[j:j + 1, :] = jnp.dot(q[j:j + 1, :].astype(BF16), s_new.astype(BF16),
                                      preferred_element_type=F32)
    o = osc_ref[...]
    o_ref[...] = (_rms(o, gn_ref[0]) * _silu(og_ref[rows, :])).astype(o_ref.dtype)


def _hgrn_sample(proj, state_all, l, lb, hgrn_norm, mix, new_state_all=None, *, bg=16):
    rb = PROMPT_TILES
    col0 = 3 * CONV_DIM // LANES
    blk = lambda part: pl.BlockSpec((DEC_BATCH, LANES), lambda h, g: (rb, col0 + part * HG_HEADS + h))
    hvec = pl.BlockSpec((1, 1, LANES), lambda h, g: (h, 0, 0))
    sblk = pl.BlockSpec((1, bg, 1, HG_DK, HG_DV), lambda h, g: (l, g, h, 0, 0))
    any_spec = pl.BlockSpec(memory_space=pl.ANY)
    args = [proj, proj, proj, proj, lb.reshape(HG_HEADS, 1, HG_DK), hgrn_norm.reshape(HG_HEADS, 1, HG_DV),
            state_all, mix]
    in_specs = [blk(0), blk(1), blk(2), blk(3), hvec, hvec, sblk, any_spec]
    aliases = {7: 0}
    if new_state_all is not None:
        args.append(new_state_all)
        in_specs.append(any_spec)
        aliases[8] = 1
        so_blk, slab = sblk, 0
    else:
        n_slabs = state_all.shape[0]
        so_blk = pl.BlockSpec((n_slabs, bg, 1, HG_DK, HG_DV), lambda h, g: (0, g, h, 0, 0))
        slab = l
    return pl.pallas_call(
        functools.partial(_hgrn_sample_kernel, bg=bg, slab=slab),
        grid=(HG_HEADS, DEC_BATCH // bg),
        in_specs=in_specs,
        out_specs=[pl.BlockSpec((bg, LANES), lambda h, g: (N_PROMPT // bg + g, CONV_DIM // LANES + h)), so_blk],
        out_shape=[jax.ShapeDtypeStruct(mix.shape, mix.dtype),
                   jax.ShapeDtypeStruct(state_all.shape, F32)],
        scratch_shapes=[pltpu.VMEM((bg, HG_DV), F32)],
        input_output_aliases=aliases,
        compiler_params=_cparams(("arbitrary", "arbitrary")),
    )(*args)


def _route_tile(logits, ri_ref, gate_ref, cnt_ref, carry_ref):
    i = pl.program_id(0)

    @pl.when(i == 0)
    def _():
        carry_ref[...] = jnp.zeros_like(carry_ref)

    tile = logits.shape[0]
    lane = lax.broadcasted_iota(I32, (tile, LANES), 1)
    lanef = lane.astype(F32)
    valid = i * tile + lax.broadcasted_iota(I32, (tile, LANES), 0) < N_ROWS
    lg = jnp.where(valid & (lane < N_EXPERTS), logits, jnp.where(valid, -jnp.inf, 0.0))
    m1 = jnp.max(lg, axis=-1, keepdims=True)
    i1 = jnp.min(jnp.where(lg == m1, lanef, float(LANES)), axis=-1, keepdims=True).astype(I32)
    lg2 = jnp.where(lane == i1, -jnp.inf, lg)
    m2 = jnp.max(lg2, axis=-1, keepdims=True)
    i2 = jnp.min(jnp.where(lg2 == m2, lanef, float(LANES)), axis=-1, keepdims=True).astype(I32)
    e = jnp.exp(m2 - m1)
    g1 = 1.0 / (1.0 + e)
    g2 = e / (1.0 + e)
    hot1 = lane == i1
    hot2 = lane == i2
    hot = jnp.where(valid, (hot1 | hot2).astype(F32), 0.0).astype(BF16)
    r = lax.broadcasted_iota(I32, (tile, tile), 0)
    c = lax.broadcasted_iota(I32, (tile, tile), 1)
    before = (c < r).astype(BF16)
    tot = jnp.dot(before, hot, preferred_element_type=F32) + carry_ref[...]
    r1 = jnp.sum(jnp.where(hot1, tot, 0.0), axis=-1, keepdims=True).astype(I32)
    r2 = jnp.sum(jnp.where(hot2, tot, 0.0), axis=-1, keepdims=True).astype(I32)
    ri_ref[...] = jnp.where(lane == 0, i1, jnp.where(lane == 1, i2, jnp.where(lane == 2, r1,
                            jnp.where(lane == 3, r2, 0))))
    gate_ref[...] = jnp.where(lane == 0, g1, jnp.where(lane == 1, g2, 0.0))
    carry_ref[...] += jnp.sum(hot.astype(F32), axis=0, keepdims=True)
    cnt_ref[...] = carry_ref[...]


def _gather_kernel(d1_ref, d2_ref, nact_ref, h_ref, xb_ref, tok_ref, buf_ref, sem):
    i = pl.program_id(0)

    nact = nact_ref[0]

    @pl.when(i == 0)
    def _():
        def clear(p, carry):
            tok_ref[p] = 0
            return carry
        lax.fori_loop(0, P_ROWS, clear, 0, unroll=8)

        def scatter(t, carry):
            tok_ref[d1_ref[t]] = t
            tok_ref[d2_ref[t]] = t
            return carry
        lax.fori_loop(0, N_ROWS, scatter, 0, unroll=4)

    def start_block(b):
        slot = b % 2

        def start(r, carry):
            tok = tok_ref[b * MOE_BLOCK + r]
            pltpu.make_async_copy(h_ref.at[pl.ds(tok, 1), :], buf_ref.at[slot, pl.ds(r, 1), :],
                                  sem.at[slot]).start()
            return carry
        lax.fori_loop(0, MOE_BLOCK, start, 0, unroll=8)

    @pl.when(i == 0)
    def _():
        start_block(i)

    @pl.when(i + 1 < nact)
    def _():
        start_block(i + 1)

    @pl.when(i < nact)
    def _():
        slot = i % 2
        pltpu.make_async_copy(h_ref.at[pl.ds(0, MOE_BLOCK), :], buf_ref.at[slot], sem.at[slot]).wait()
        xb_ref[...] = buf_ref[slot].astype(xb_ref.dtype)

    @pl.when(i >= nact)
    def _():
        xb_ref[...] = jnp.zeros_like(xb_ref)


def _gather(h, dest1, dest2, nact):
    return pl.pallas_call(
        _gather_kernel,
        grid_spec=pltpu.PrefetchScalarGridSpec(
            num_scalar_prefetch=3,
            grid=(N_BLOCKS,),
            in_specs=[pl.BlockSpec(memory_space=pl.ANY)],
            out_specs=pl.BlockSpec((MOE_BLOCK, D_MODEL), lambda i, d1, d2, na: (i, 0)),
            scratch_shapes=[pltpu.SMEM((P_ROWS,), I32), pltpu.VMEM((2, MOE_BLOCK, D_MODEL), F32),
                            pltpu.SemaphoreType.DMA((2,))],
        ),
        out_shape=jax.ShapeDtypeStruct((P_ROWS, D_MODEL), BF16),
        compiler_params=_cparams(("arbitrary",)),
    )(dest1, dest2, nact, h)


def _combine_resid_kernel(d1_ref, d2_ref, x_ref, gate_ref, gpost_ref, gap_ref, gas_ref, yb_ref, yp_ref, ys_ref,
                          b1_ref, b2_ref, sem):
    i = pl.program_id(0)

    def start_tile(t):
        slot = t % 2

        def start(r, carry):
            tok = t * ROW_TILE + r
            pltpu.make_async_copy(yb_ref.at[pl.ds(d1_ref[tok], 1), :], b1_ref.at[slot, pl.ds(r, 1), :],
                                  sem.at[0, slot]).start()
            pltpu.make_async_copy(yb_ref.at[pl.ds(d2_ref[tok], 1), :], b2_ref.at[slot, pl.ds(r, 1), :],
                                  sem.at[1, slot]).start()
            return carry
        lax.fori_loop(0, ROW_TILE, start, 0, unroll=8)

    @pl.when(i == 0)
    def _():
        start_tile(i)

    @pl.when(i + 1 < pl.num_programs(0))
    def _():
        start_tile(i + 1)

    slot = i % 2
    pltpu.make_async_copy(yb_ref.at[pl.ds(0, ROW_TILE), :], b1_ref.at[slot], sem.at[0, slot]).wait()
    pltpu.make_async_copy(yb_ref.at[pl.ds(0, ROW_TILE), :], b2_ref.at[slot], sem.at[1, slot]).wait()
    gate = gate_ref[...]
    f = gate[:, 0:1] * b1_ref[slot] + gate[:, 1:2] * b2_ref[slot]
    x = x_ref[...] + _pick(i, gap_ref, gas_ref) * _rms(f, gpost_ref[...])

    @pl.when(i < PROMPT_TILES)
    def _():
        yp_ref[...] = x

    @pl.when(i == PROMPT_TILES)
    def _():
        ys_ref[...] = x


def _combine_resid(x, yb, gates, dest1, dest2, gpost, ga):
    return pl.pallas_call(
        _combine_resid_kernel,
        grid_spec=pltpu.PrefetchScalarGridSpec(
            num_scalar_prefetch=2,
            grid=(N_TILES,),
            in_specs=[_row_spec(D_MODEL), _row_spec(LANES), _vec_spec(), _modp_spec(), _mods_spec(),
                      pl.BlockSpec(memory_space=pl.ANY)],
            out_specs=[_prompt_rows_spec(), _mods_spec()],
            scratch_shapes=[pltpu.VMEM((2, ROW_TILE, D_MODEL), F32), pltpu.VMEM((2, ROW_TILE, D_MODEL), F32),
                            pltpu.SemaphoreType.DMA((2, 2))],
        ),
        out_shape=[jax.ShapeDtypeStruct((N_PROMPT, D_MODEL), F32), jax.ShapeDtypeStruct((DEC_BATCH, D_MODEL), F32)],
        compiler_params=_cparams(("arbitrary",)),
    )(dest1, dest2, x, gates, gpost, ga[0], ga[1], yb)


def _moe_experts(h, ri, gates, counts, w1, w3, w2):
    counts = counts[0, :N_EXPERTS].astype(I32)
    padded = (counts + MOE_BLOCK - 1) // MOE_BLOCK * MOE_BLOCK
    pends = jnp.cumsum(padded)
    pstarts = pends - padded
    dest1 = pstarts[ri[:, 0]] + ri[:, 2]
    dest2 = pstarts[ri[:, 1]] + ri[:, 3]
    nact = (pends[-1:] // MOE_BLOCK).astype(I32)
    block_start = jnp.arange(N_BLOCKS, dtype=I32) * MOE_BLOCK
    block_e = jnp.minimum(jnp.sum((pends[None, :] <= block_start[:, None]).astype(I32), axis=1), N_EXPERTS - 1)
    run_end = (pends // MOE_BLOCK)[block_e]
    next_e = jnp.where(run_end < nact[0], block_e[jnp.minimum(run_end, N_BLOCKS - 1)], -1).astype(I32)
    xb = _gather(h, dest1, dest2, nact)
    gb = _emm(xb, (w1, w3), block_e, nact, next_e, tm=MOE_BLOCK, tn=1792, out_dtype=BF16)
    yb = _emm(gb, (w2,), block_e, nact, next_e, tm=MOE_BLOCK, tn=512, out_dtype=F32)
    return yb, gates, dest1, dest2


def _dense(a, ws, e, *, tm, tn, out_dtype):
    nb = a.shape[0] // tm
    return _emm(a, ws, jnp.full((nb,), e, I32), jnp.full((1,), nb, I32), jnp.full((nb,), -1, I32),
                tm=tm, tn=tn, out_dtype=out_dtype)


def kernel(x_prompt, x_sample, state_conv, state_hgrn, c_prompt, c_sample, norm_pre, norm_post, w_mod, b_mod, w_in, conv_w, conv_norm, lb_logits, hgrn_norm, w_out, ffn_w1, ffn_w3, ffn_w2, router_w, router_b, moe_w1, moe_w3, moe_w2):
    p = jax.nn.softmax(lb_logits.astype(F32), axis=0)
    lb_all = jnp.cumsum(p, axis=0) - p[0:1]

    n_cond = BATCH + DEC_BATCH
    cond_rows = (n_cond + SUBLANES - 1) // SUBLANES * SUBLANES
    c_all = jnp.concatenate([c_prompt, c_sample, jnp.zeros((cond_rows - n_cond, D_MODEL), F32)], axis=0)
    mod = _gmm(jnp.concatenate([c_all] * DEPTH, axis=0), w_mod, jnp.arange(DEPTH, dtype=I32),
               jnp.full((1,), DEPTH, I32), tm=cond_rows, tn=1024, out_dtype=F32,
               bias=b_mod.reshape(DEPTH, 1, 6 * D_MODEL), lhs_silu=True)

    def mod_vec(l, j):
        rows = mod[l * cond_rows:l * cond_rows + n_cond, j * D_MODEL:(j + 1) * D_MODEL]
        return rows[:BATCH].reshape(BATCH, 1, D_MODEL), rows[BATCH:]

    x = (x_prompt.reshape(N_PROMPT, D_MODEL), x_sample.reshape(DEC_BATCH, D_MODEL))
    vec = lambda a: a.reshape(1, -1)

    h = _prenorm(x[0], x[1], vec(norm_pre[0, 0]), mod_vec(0, 1), mod_vec(0, 0))
    new_conv_p, new_hgrn_p, new_conv_s, new_hgrn_s = [], [], [], None
    for l in range(DEPTH):
        sh_f, sc_f, ga_f = mod_vec(l, 3), mod_vec(l, 4), mod_vec(l, 5)
        ga_a = mod_vec(l, 2)
        proj = _dense(h, (w_in,), l, tm=1664, tn=1024, out_dtype=F32)
        mix = jnp.zeros((N_ROWS, 2 * CONV_DIM), BF16)
        mix, nc_p = _conv_prompt(proj, conv_w[l], vec(conv_norm[l]), mix)
        mix, ns_p = _hgrn_prompt(proj, lb_all[l], hgrn_norm[l], mix)
        mix, nc_s = _conv_sample(proj, state_conv[l], conv_w[l], vec(conv_norm[l]), mix)
        mix, new_hgrn_s = _hgrn_sample(proj, state_hgrn, l, lb_all[l], hgrn_norm[l], mix, new_hgrn_s)
        mixed = _dense(mix, (w_out,), l, tm=1664, tn=1024, out_dtype=F32)
        new_conv_p.append(nc_p)
        new_hgrn_p.append(ns_p)
        new_conv_s.append(nc_s)
        nxt = (vec(norm_pre[l, 1]), sc_f, sh_f)
        if l % 2 == 0:
            j = l // 2
            x, h2 = _resid(x, mixed, vec(norm_post[l, 0]), ga_a, nxt=nxt)
            g = _dense(h2, (ffn_w1, ffn_w3), j, tm=1664, tn=512, out_dtype=BF16)
            f = _dense(g, (ffn_w2,), j, tm=1040, tn=512, out_dtype=F32)
        else:
            j = l // 2
            rw = jnp.pad(router_w[j].astype(F32), ((0, 0), (0, LANES - N_EXPERTS)))
            rb = jnp.pad(router_b[j].astype(F32), (0, LANES - N_EXPERTS)).reshape(1, LANES)
            x, h2, ri, gates, counts = _resid(x, mixed, vec(norm_post[l, 0]), ga_a, nxt=nxt, router=(rw, rb),
                                              h_dtype=F32)
            f = _moe_experts(h2, ri, gates, counts, moe_w1[j], moe_w3[j], moe_w2[j])
        if l + 1 < DEPTH:
            assert l % 2 == 0, "the expert combine is fused with the trunk's last residual step only"
            nxt = (vec(norm_pre[l + 1, 0]), mod_vec(l + 1, 1), mod_vec(l + 1, 0))
            x, h = _resid(x, f, vec(norm_post[l, 1]), ga_f, nxt=nxt)
        elif l % 2 == 0:
            y_prompt, y_sample = _resid(x, f, vec(norm_post[l, 1]), ga_f, y_split=True)
        else:
            y_prompt, y_sample = _combine_resid(x, *f, vec(norm_post[l, 1]), ga_f)

    y_prompt = y_prompt.reshape(BATCH, SEQ, D_MODEL)
    y_sample = y_sample.reshape(DEC_BATCH, 1, D_MODEL)
    return (y_prompt, y_sample, jnp.stack(new_conv_p), jnp.stack(new_hgrn_p),
            jnp.stack(new_conv_s), new_hgrn_s)
```

```python
import functools

import jax
import jax.numpy as jnp
import numpy as np
from jax import lax
from jax.experimental import pallas as pl
from jax.experimental.pallas import tpu as pltpu

F32 = jnp.float32
BF16 = jnp.bfloat16
I32 = jnp.int32
HIGHEST = lax.Precision.HIGHEST

LANES = 128
SUBLANES = 8
VMEM_LIMIT_BYTES = 56 * 1024 * 1024

D_MODEL = 2048
BATCH = 4
SEQ = 2048
DEPTH = 2
DEC_BATCH = 128
N_PROMPT = BATCH * SEQ
N_ROWS = N_PROMPT + DEC_BATCH
CONV_DIM = 1024
CONV_GROUPS = 16
CONV_W = 3
HG_HEADS = 8
HG_DK = 128
HG_DV = 128
PROJ_WIDTH = 7168
N_EXPERTS = 8
EPS = 1e-6
F_MIN = 1e-6

ROW_TILE = 128
N_TILES = N_ROWS // ROW_TILE
PROMPT_TILES = N_PROMPT // ROW_TILE
TILES_PER_SEQ = SEQ // ROW_TILE
MOE_BLOCK = 256
N_PAIRS = 2 * N_ROWS
N_BLOCKS = (N_PAIRS + N_EXPERTS * (MOE_BLOCK - 1) + MOE_BLOCK - 1) // MOE_BLOCK
P_ROWS = N_BLOCKS * MOE_BLOCK


def _cparams(sem):
    return pltpu.CompilerParams(dimension_semantics=sem, vmem_limit_bytes=VMEM_LIMIT_BYTES)


def _rms(x, g):
    return x * lax.rsqrt(jnp.mean(x * x, axis=-1, keepdims=True) + EPS) * g


def _silu(x):
    return x * jax.nn.sigmoid(x)


def _gmm_kernel(be_ref, nact_ref, a_ref, w_ref, *rest, lhs_silu, has_bias):
    if has_bias:
        b_ref, o_ref, wb_ref = rest
    else:
        o_ref, wb_ref = rest
    i = pl.program_id(1)
    prev = be_ref[jnp.maximum(i - 1, 0)]

    @pl.when((i == 0) | (be_ref[i] != prev))
    def _():
        wb_ref[...] = w_ref[0].astype(BF16)

    @pl.when(i < nact_ref[0])
    def _():
        a = a_ref[...]
        if lhs_silu:
            a = _silu(a)
        acc = jnp.dot(a.astype(BF16), wb_ref[...], preferred_element_type=F32)
        if has_bias:
            acc = acc + b_ref[0]
        o_ref[...] = acc.astype(o_ref.dtype)

    @pl.when(i >= nact_ref[0])
    def _():
        o_ref[...] = jnp.zeros_like(o_ref)


def _gmm(a, w, block_e, nact, *, tm, tn, out_dtype, bias=None, lhs_silu=False):
    m, k = a.shape
    _, _, n = w.shape
    grid = (n // tn, m // tm)
    in_specs = [
        pl.BlockSpec((tm, k), lambda j, i, be, na: (i, 0)),
        pl.BlockSpec((1, k, tn), lambda j, i, be, na: (be[i], 0, j)),
    ]
    args = [a, w]
    if bias is not None:
        in_specs.append(pl.BlockSpec((1, 1, tn), lambda j, i, be, na: (be[i], 0, j)))
        args.append(bias)
    return pl.pallas_call(
        functools.partial(_gmm_kernel, lhs_silu=lhs_silu, has_bias=bias is not None),
        grid_spec=pltpu.PrefetchScalarGridSpec(
            num_scalar_prefetch=2,
            grid=grid,
            in_specs=in_specs,
            out_specs=pl.BlockSpec((tm, tn), lambda j, i, be, na: (i, j)),
            scratch_shapes=[pltpu.VMEM((k, tn), BF16)],
        ),
        out_shape=jax.ShapeDtypeStruct((m, n), out_dtype),
        compiler_params=_cparams(("arbitrary", "arbitrary")),
    )(block_e, nact, *args)


CAST_ROWS = 32


def _emm_kernel(be_ref, nact_ref, nexte_ref, a_hbm, *rest, n_w, tm, nb):
    w_hbm = rest[:n_w]
    o_hbm, abuf, obuf, stage_ref, wb_ref, asem, osem, wsem = rest[n_w:]
    j = pl.program_id(0)
    last_tile = j + 1 == pl.num_programs(0)
    tn = obuf.shape[2]
    nact = nact_ref[0]
    base = (j * nact) % 2

    def a_copy(i, slot):
        rows = pl.ds(pl.multiple_of(i * tm, tm), tm)
        return pltpu.make_async_copy(a_hbm.at[rows, :], abuf.at[slot], asem.at[slot])

    def o_copy(i, slot):
        rows = pl.ds(pl.multiple_of(i * tm, tm), tm)
        cols = pl.ds(pl.multiple_of(j * tn, LANES), tn)
        return pltpu.make_async_copy(obuf.at[slot], o_hbm.at[rows, cols], osem.at[slot])

    def slab_copies(expert, col_tile):
        cols = pl.ds(pl.multiple_of(col_tile * tn, LANES), tn)
        return [pltpu.make_async_copy(w_hbm[t].at[expert, :, cols], stage_ref.at[t], wsem.at[t])
                for t in range(n_w)]

    @pl.when(j == 0)
    def _():
        for c in slab_copies(be_ref[0], j):
            c.start()
        a_copy(0, base).start()

    def block(i, carry):
        slot = (base + i) % 2
        e = be_ref[i]

        @pl.when(i + 1 < nact)
        def _():
            a_copy(i + 1, 1 - slot).start()

        @pl.when((i + 1 == nact) & jnp.logical_not(last_tile))
        def _():
            a_copy(0, 1 - slot).start()

        @pl.when((i == 0) | (e != be_ref[jnp.maximum(i - 1, 0)]))
        def _():
            for c in slab_copies(e, j):
                c.wait()

            def round_rows(c, carry):
                rows = pl.ds(pl.multiple_of(c * CAST_ROWS, CAST_ROWS), CAST_ROWS)
                for t in range(n_w):
                    wb_ref[t, rows, :] = stage_ref[t, rows, :].astype(BF16)
                return carry
            lax.fori_loop(0, stage_ref.shape[1] // CAST_ROWS, round_rows, 0)
            nxt = nexte_ref[i]

            @pl.when(nxt >= 0)
            def _():
                for c in slab_copies(nxt, j):
                    c.start()

            @pl.when((nxt < 0) & jnp.logical_not(last_tile))
            def _():
                for c in slab_copies(be_ref[0], j + 1):
                    c.start()

        a_copy(i, slot).wait()

        @pl.when(i >= 2)
        def _():
            o_copy(i - 2, slot).wait()

        a = abuf[slot]
        if a.dtype != BF16:
            a = a.astype(BF16)
        acc = jnp.dot(a, wb_ref[0], preferred_element_type=F32)
        if n_w == 2:
            acc = _silu(acc) * jnp.dot(a, wb_ref[1], preferred_element_type=F32)
        obuf[slot] = acc.astype(obuf.dtype)
        o_copy(i, slot).start()
        return carry

    lax.fori_loop(0, nact, block, 0)

    @pl.when(nact >= 2)
    def _():
        o_copy(nact - 2, (base + nact) % 2).wait()
    o_copy(nact - 1, (base + nact - 1) % 2).wait()

    @pl.when(nact < nb)
    def _():
        obuf[0] = jnp.zeros(obuf.shape[1:], obuf.dtype)

        def zero_block(i, carry):
            c = o_copy(i, 0)
            c.start()
            c.wait()
            return carry
        lax.fori_loop(nact, nb, zero_block, 0)


def _emm(a, ws, block_e, nact, next_e, *, tm, tn, out_dtype):
    m, k = a.shape
    n = ws[0].shape[2]
    n_w = len(ws)
    any_spec = pl.BlockSpec(memory_space=pl.ANY)
    return pl.pallas_call(
        functools.partial(_emm_kernel, n_w=n_w, tm=tm, nb=m // tm),
        grid_spec=pltpu.PrefetchScalarGridSpec(
            num_scalar_prefetch=3,
            grid=(n // tn,),
            in_specs=[any_spec] * (1 + n_w),
            out_specs=any_spec,
            scratch_shapes=[pltpu.VMEM((2, tm, k), a.dtype), pltpu.VMEM((2, tm, tn), out_dtype),
                            pltpu.VMEM((n_w, k, tn), F32), pltpu.VMEM((n_w, k, tn), BF16),
                            pltpu.SemaphoreType.DMA((2,)), pltpu.SemaphoreType.DMA((2,)),
                            pltpu.SemaphoreType.DMA((n_w,))],
        ),
        out_shape=jax.ShapeDtypeStruct((m, n), out_dtype),
        compiler_params=_cparams(("arbitrary",)),
    )(block_e, nact, next_e, a, *ws)


ELT_TILE = 256


def _rep_rows(s, tile):
    return s if tile == DEC_BATCH else jnp.concatenate([s] * (tile // DEC_BATCH), axis=0)


def _pick(i, p_ref, s_ref, tile=ROW_TILE):
    return jnp.where(i < N_PROMPT // tile, p_ref[0], _rep_rows(s_ref[...], tile))


def _prenorm_kernel(xp_ref, xs_ref, g_ref, scp_ref, scs_ref, shp_ref, shs_ref, h_ref):
    i = pl.program_id(0)
    tile = h_ref.shape[0]
    x = jnp.where(i < N_PROMPT // tile, xp_ref[...], _rep_rows(xs_ref[...], tile))
    sc = _pick(i, scp_ref, scs_ref, tile)
    sh = _pick(i, shp_ref, shs_ref, tile)
    h_ref[...] = (_rms(x, g_ref[...]) * (1.0 + sc) + sh).astype(h_ref.dtype)


def _resid_kernel(*refs, x_split, y_split, with_next, with_router, h_dtype):
    it = iter(refs)
    if x_split:
        xp_ref, xs_ref = next(it), next(it)
    else:
        x_ref = next(it)
    f_ref, gpost_ref, gap_ref, gas_ref = [next(it) for _ in range(4)]
    if with_next:
        gpre_ref, scp_ref, scs_ref, shp_ref, shs_ref = [next(it) for _ in range(5)]
    if with_router:
        rw_ref, rb_ref = next(it), next(it)
    if y_split:
        yp_ref, ys_ref = next(it), next(it)
    else:
        xo_ref = next(it)
    if with_next:
        h_ref = next(it)
    if with_router:
        ri_ref, gate_ref, cnt_ref, carry_ref = [next(it) for _ in range(4)]
    i = pl.program_id(0)
    tile = f_ref.shape[0]
    prompt_tiles = N_PROMPT // tile
    ga = _pick(i, gap_ref, gas_ref, tile)
    if x_split:
        x = jnp.where(i < prompt_tiles, xp_ref[...], _rep_rows(xs_ref[...], tile))
    else:
        x = x_ref[...]
    x = x + ga * _rms(f_ref[...], gpost_ref[...])
    if y_split:
        @pl.when(i < prompt_tiles)
        def _():
            yp_ref[...] = x

        @pl.when(i == prompt_tiles)
        def _():
            ys_ref[...] = x[:DEC_BATCH]
    else:
        xo_ref[...] = x
    if with_next:
        sc = _pick(i, scp_ref, scs_ref, tile)
        sh = _pick(i, shp_ref, shs_ref, tile)
        h = _rms(x, gpre_ref[...]) * (1.0 + sc) + sh
        h_ref[...] = h.astype(h_dtype)
        if with_router:
            logits = jnp.dot(h, rw_ref[...], precision=HIGHEST, preferred_element_type=F32) + rb_ref[...]
            _route_tile(logits, ri_ref, gate_ref, cnt_ref, carry_ref)


def _n_tiles(tile):
    return N_PROMPT // tile + 1


def _row_spec(width, tile=ROW_TILE):
    return pl.BlockSpec((tile, width), lambda i, *_: (i, 0))


def _vec_spec():
    return pl.BlockSpec((1, D_MODEL), lambda i, *_: (0, 0))


def _modp_spec(tile=ROW_TILE):
    return pl.BlockSpec((1, 1, D_MODEL), lambda i, *_: (jnp.minimum(i // (SEQ // tile), BATCH - 1), 0, 0))


def _mods_spec():
    return pl.BlockSpec((DEC_BATCH, D_MODEL), lambda i, *_: (0, 0))


def _prompt_rows_spec(tile=ROW_TILE):
    return pl.BlockSpec((tile, D_MODEL), lambda i, *_: (jnp.minimum(i, N_PROMPT // tile - 1), 0))


def _prenorm(xp, xs, g, sc, sh):
    t = ELT_TILE
    return pl.pallas_call(
        _prenorm_kernel,
        grid=(_n_tiles(t),),
        in_specs=[_prompt_rows_spec(t), _mods_spec(), _vec_spec(), _modp_spec(t), _mods_spec(), _modp_spec(t),
                  _mods_spec()],
        out_specs=_row_spec(D_MODEL, t),
        out_shape=jax.ShapeDtypeStruct((N_ROWS, D_MODEL), BF16),
        compiler_params=_cparams(("arbitrary",)),
    )(xp, xs, g, sc[0], sc[1], sh[0], sh[1])


def _resid(x, f, gpost, ga, nxt=None, router=None, h_dtype=BF16, y_split=False):
    t = ELT_TILE
    x_split = isinstance(x, tuple)
    if x_split:
        args = [x[0], x[1]]
        in_specs = [_prompt_rows_spec(t), _mods_spec()]
    else:
        args = [x]
        in_specs = [_row_spec(D_MODEL, t)]
    args += [f, gpost, ga[0], ga[1]]
    in_specs += [_row_spec(D_MODEL, t), _vec_spec(), _modp_spec(t), _mods_spec()]
    if y_split:
        out_shape = [jax.ShapeDtypeStruct((N_PROMPT, D_MODEL), F32), jax.ShapeDtypeStruct((DEC_BATCH, D_MODEL), F32)]
        out_specs = [_prompt_rows_spec(t), _mods_spec()]
    else:
        out_shape = [jax.ShapeDtypeStruct((N_ROWS, D_MODEL), F32)]
        out_specs = [_row_spec(D_MODEL, t)]
    if nxt is not None:
        gpre, sc, sh = nxt
        args += [gpre, sc[0], sc[1], sh[0], sh[1]]
        in_specs += [_vec_spec(), _modp_spec(t), _mods_spec(), _modp_spec(t), _mods_spec()]
        out_shape.append(jax.ShapeDtypeStruct((N_ROWS, D_MODEL), h_dtype))
        out_specs.append(_row_spec(D_MODEL, t))
    if router is not None:
        rw, rb = router
        args += [rw, rb]
        in_specs += [pl.BlockSpec((D_MODEL, LANES), lambda i: (0, 0)), pl.BlockSpec((1, LANES), lambda i: (0, 0))]
        out_shape += [jax.ShapeDtypeStruct((N_ROWS, LANES), I32), jax.ShapeDtypeStruct((N_ROWS, LANES), F32),
                      jax.ShapeDtypeStruct((1, LANES), F32)]
        out_specs += [_row_spec(LANES, t), _row_spec(LANES, t), pl.BlockSpec((1, LANES), lambda i: (0, 0))]
        scratch = [pltpu.VMEM((1, LANES), F32)]
    else:
        scratch = []
    return pl.pallas_call(
        functools.partial(_resid_kernel, x_split=x_split, y_split=y_split, with_next=nxt is not None,
                          with_router=router is not None, h_dtype=h_dtype),
        grid=(_n_tiles(t),),
        in_specs=in_specs,
        out_specs=out_specs,
        out_shape=out_shape,
        scratch_shapes=scratch,
        compiler_params=_cparams(("arbitrary",)),
    )(*args)


def _group_tables():
    grp = np.arange(CONV_DIM)[:, None] // (CONV_DIM // CONV_GROUPS) == np.arange(LANES)[None, :]
    return jnp.asarray(grp, BF16), jnp.asarray(grp.T, BF16)


def _split2(x):
    hi = x.astype(BF16)
    return hi, (x - hi.astype(F32)).astype(BF16)


def _group_rms(y, gnorm, sel, sel_t):
    width = CONV_DIM // CONV_GROUPS
    hi, lo = _split2(y * y)
    ss = jnp.dot(hi, sel, preferred_element_type=F32) + jnp.dot(lo, sel, preferred_element_type=F32)
    hi, lo = _split2(lax.rsqrt(ss * (1.0 / width) + EPS))
    scale = jnp.dot(hi, sel_t, preferred_element_type=F32) + jnp.dot(lo, sel_t, preferred_element_type=F32)
    return y * scale * gnorm


def _conv_prompt_kernel(hc_ref, bg_ref, cg_ref, hch_ref, cgh_ref, cw_ref, gn_ref, sel_ref, selt_ref, mix_ref,
                        y_ref, nc_ref, *, tt):
    del mix_ref
    t = pl.program_id(1)
    u = cg_ref[...] * hc_ref[...]
    halo = jnp.where(t == 0, 0.0, cgh_ref[...] * hch_ref[...])
    h1 = halo[SUBLANES - 1:SUBLANES]
    h2 = halo[SUBLANES - 2:SUBLANES - 1]
    row = lax.broadcasted_iota(I32, u.shape, 0)
    u1 = jnp.where(row == 0, h1, pltpu.roll(u, 1, 0))
    u2 = jnp.where(row == 0, h2, jnp.where(row == 1, h1, pltpu.roll(u, 2, 0)))
    conv = cw_ref[0:1] * u2 + cw_ref[1:2] * u1 + cw_ref[2:3] * u
    y_ref[...] = _group_rms(bg_ref[...] * conv, gn_ref[...], sel_ref[...], selt_ref[...]).astype(y_ref.dtype)

    @pl.when(t == pl.num_programs(1) - 1)
    def _():
        nc_ref[0] = u[tt - (CONV_W - 1):]


def _conv_prompt(proj, conv_w, conv_norm, mix, *, tt=256):
    nt = SEQ // tt
    cblk = lambda c: pl.BlockSpec((tt, CONV_DIM), lambda b, t: (b * nt + t, c))
    hblk = lambda c: pl.BlockSpec(
        (SUBLANES, CONV_DIM), lambda b, t: (jnp.maximum((b * nt + t) * (tt // SUBLANES) - 1, 0), c))
    return pl.pallas_call(
        functools.partial(_conv_prompt_kernel, tt=tt),
        grid=(BATCH, nt),
        in_specs=[cblk(0), cblk(1), cblk(2), hblk(0), hblk(2),
                  pl.BlockSpec((CONV_W, CONV_DIM), lambda b, t: (0, 0)),
                  pl.BlockSpec((1, CONV_DIM), lambda b, t: (0, 0)),
                  pl.BlockSpec((CONV_DIM, LANES), lambda b, t: (0, 0)),
                  pl.BlockSpec((LANES, CONV_DIM), lambda b, t: (0, 0)),
                  pl.BlockSpec(memory_space=pl.ANY)],
        out_specs=[pl.BlockSpec((tt, CONV_DIM), lambda b, t: (b * nt + t, 0)),
                   pl.BlockSpec((1, CONV_W - 1, CONV_DIM), lambda b, t: (b, 0, 0))],
        out_shape=[jax.ShapeDtypeStruct(mix.shape, mix.dtype),
                   jax.ShapeDtypeStruct((BATCH, CONV_W - 1, CONV_DIM), F32)],
        input_output_aliases={9: 0},
        compiler_params=_cparams(("arbitrary", "arbitrary")),
    )(proj, proj, proj, proj, proj, conv_w, conv_norm, *_group_tables(), mix)


HG_LEVELS = tuple(1 << i for i in range(ROW_TILE.bit_length() - 1))
LOG2_E = 1.4426950408889634


def _gates(z, lb):
    sg = jax.nn.sigmoid(z)
    f = lb + (1.0 - lb) * sg
    return jnp.log(jnp.maximum(f, F_MIN)), (1.0 - lb) * (1.0 - sg)


def _level_table():
    t = np.arange(ROW_TILE)[:, None]
    s = np.arange(ROW_TILE)[None, :]
    x = np.maximum(t ^ s, 1)
    lvl = np.floor(np.log2(x)).astype(np.int32)
    lvl = np.where(t == s, len(HG_LEVELS), np.where(s < t, lvl, -1))
    return jnp.asarray(lvl, I32)


def _nt_dot(x, y):
    return lax.dot_general(x, y, (((1,), (1,)), ((), ())), preferred_element_type=F32)


def _hgrn_tile(q, z, v, lb, st, tri, lvl):
    logf, k = _gates(z, lb)
    lf2 = logf * LOG2_E
    hi = lf2.astype(BF16)
    rem = lf2 - hi.astype(F32)
    mid = rem.astype(BF16)
    lo = (rem - mid.astype(F32)).astype(BF16)
    parts = jnp.dot(tri, jnp.concatenate([hi, mid, lo], axis=1), preferred_element_type=F32)
    a = parts[:, :HG_DK] + parts[:, HG_DK:2 * HG_DK] + parts[:, 2 * HG_DK:]
    row = lax.broadcasted_iota(I32, (ROW_TILE, HG_DK), 0)
    a8 = a.reshape(ROW_TILE // SUBLANES, SUBLANES, HG_DK)
    sub8 = lax.broadcasted_iota(I32, a8.shape, 1)
    scores = jnp.where(lvl == len(HG_LEVELS), _nt_dot(q.astype(BF16), k.astype(BF16)), 0.0)
    for li, c in enumerate(HG_LEVELS):
        if c < SUBLANES:
            if c == 1:
                d = jnp.where((row & 1) == 1, lf2, 0.0)
            elif c == 2:
                anchor = jnp.where(sub8 < 4, a8[:, 1:2, :], a8[:, 5:6, :])
                d = (a8 - anchor).reshape(ROW_TILE, HG_DK)
            else:
                d = (a8 - a8[:, c - 1:c, :]).reshape(ROW_TILE, HG_DK)
            src = jnp.where((row & c) != 0, q, k)
            neg = -jnp.abs(d)
        else:
            shape3 = (ROW_TILE // (2 * c), 2 * c, HG_DK)
            ab, qb, kb = a.reshape(shape3), q.reshape(shape3), k.reshape(shape3)
            anchor = ab[:, c - 1:c, :]
            neg = jnp.concatenate([anchor - ab[:, :c, :], ab[:, c:, :] - anchor], axis=1).reshape(ROW_TILE, HG_DK)
            src = jnp.concatenate([kb[:, :c, :], qb[:, c:, :]], axis=1).reshape(ROW_TILE, HG_DK)
        x = (src * jnp.exp2(neg)).astype(BF16)
        scores = jnp.where(lvl == li, _nt_dot(x, x), scores)
    vb = v.astype(BF16)
    a_last = a[ROW_TILE - 1:ROW_TILE, :]
    o = jnp.dot(scores.astype(BF16), vb, preferred_element_type=F32)
    o = o + _nt_dot((q * jnp.exp2(a)).astype(BF16), st.astype(BF16))
    kt = (k * jnp.exp2(a_last - a)).astype(BF16)
    st = st * jnp.exp2(a_last) + lax.dot_general(vb, kt, (((0,), (0,)), ((), ())),
                                                 preferred_element_type=F32)
    return o, st


def _hgrn_prompt_kernel(q_ref, z_ref, v_ref, og_ref, lb_ref, gn_ref, tri_ref, lvl_ref, mix_ref, o_ref, s_ref,
                        st_ref, *, tt):
    del mix_ref
    t = pl.program_id(1)

    @pl.when(t == 0)
    def _():
        st_ref[...] = jnp.zeros_like(st_ref)

    def body(j, carry):
        rows = pl.ds(pl.multiple_of(j * ROW_TILE, ROW_TILE), ROW_TILE)
        for h in range(HG_HEADS):
            cols = slice(h * LANES, (h + 1) * LANES)
            o, st = _hgrn_tile(q_ref[rows, cols], z_ref[rows, cols], v_ref[rows, cols], lb_ref[:, cols],
                               st_ref[h], tri_ref[...], lvl_ref[...])
            st_ref[h] = st
            o_ref[rows, cols] = (_rms(o, gn_ref[:, cols]) * _silu(og_ref[rows, cols])).astype(o_ref.dtype)
        return carry

    lax.fori_loop(0, tt // ROW_TILE, body, 0)

    @pl.when(t == pl.num_programs(1) - 1)
    def _():
        for h in range(HG_HEADS):
            s_ref[0, h] = st_ref[h].T


def _hgrn_prompt(proj, lb, hgrn_norm, mix, *, tt=256):
    nt = SEQ // tt
    width = HG_HEADS * LANES
    col0 = 3 * CONV_DIM // width
    blk = lambda part: pl.BlockSpec((tt, width), lambda b, t: (b * nt + t, col0 + part))
    hvec = pl.BlockSpec((1, width), lambda b, t: (0, 0))
    const = pl.BlockSpec((ROW_TILE, ROW_TILE), lambda b, t: (0, 0))
    tri =jnp.asarray(np.tril(np.ones((ROW_TILE, ROW_TILE), np.float32)), BF16)
    return pl.pallas_call(
        functools.partial(_hgrn_prompt_kernel, tt=tt),
        grid=(BATCH, nt),
        in_specs=[blk(0), blk(1), blk(2), blk(3), hvec, hvec, const, const, pl.BlockSpec(memory_space=pl.ANY)],
        out_specs=[pl.BlockSpec((tt, width), lambda b, t: (b * nt + t, 1)),
                   pl.BlockSpec((1, HG_HEADS, HG_DK, HG_DV), lambda b, t: (b, 0, 0, 0))],
        out_shape=[jax.ShapeDtypeStruct(mix.shape, mix.dtype),
                   jax.ShapeDtypeStruct((BATCH, HG_HEADS, HG_DK, HG_DV), F32)],
        scratch_shapes=[pltpu.VMEM((HG_HEADS, HG_DV, HG_DK), F32)],
        input_output_aliases={8: 0},
        compiler_params=_cparams(("arbitrary", "arbitrary")),
    )(proj, proj, proj, proj, lb.reshape(1, width), hgrn_norm.reshape(1, width), tri, _level_table(), mix)


def _conv_sample_kernel(hc_ref, bg_ref, cg_ref, cs_ref, cw_ref, gn_ref, sel_ref, selt_ref, mix_ref, y_ref, nc_ref):
    del mix_ref
    u = cg_ref[...] * hc_ref[...]
    s0 = cs_ref[:, 0, :]
    s1 = cs_ref[:, 1, :]
    conv = cw_ref[0:1] * s0 + cw_ref[1:2] * s1 + cw_ref[2:3] * u
    y_ref[...] = _group_rms(bg_ref[...] * conv, gn_ref[...], sel_ref[...], selt_ref[...]).astype(y_ref.dtype)
    nc_ref[:, 0, :] = s1
    nc_ref[:, 1, :] = u


def _conv_sample(proj, conv_state, conv_w, conv_norm, mix):
    rb = PROMPT_TILES
    cblk = lambda c: pl.BlockSpec((DEC_BATCH, CONV_DIM), lambda i: (rb, c))
    return pl.pallas_call(
        _conv_sample_kernel,
        grid=(1,),
        in_specs=[cblk(0), cblk(1), cblk(2),
                  pl.BlockSpec((DEC_BATCH, CONV_W - 1, CONV_DIM), lambda i: (0, 0, 0)),
                  pl.BlockSpec((CONV_W, CONV_DIM), lambda i: (0, 0)),
                  pl.BlockSpec((1, CONV_DIM), lambda i: (0, 0)),
                  pl.BlockSpec((CONV_DIM, LANES), lambda i: (0, 0)),
                  pl.BlockSpec((LANES, CONV_DIM), lambda i: (0, 0)),
                  pl.BlockSpec(memory_space=pl.ANY)],
        out_specs=[pl.BlockSpec((DEC_BATCH, CONV_DIM), lambda i: (rb, 0)),
                   pl.BlockSpec((DEC_BATCH, CONV_W - 1, CONV_DIM), lambda i: (0, 0, 0))],
        out_shape=[jax.ShapeDtypeStruct(mix.shape, mix.dtype),
                   jax.ShapeDtypeStruct((DEC_BATCH, CONV_W - 1, CONV_DIM), F32)],
        input_output_aliases={8: 0},
        compiler_params=_cparams(("arbitrary",)),
    )(proj, proj, proj, conv_state, conv_w, conv_norm, *_group_tables(), mix)


def _hgrn_sample_kernel(q_ref, z_ref, v_ref, og_ref, lb_ref, gn_ref, s_ref, *rest, bg, slab):
    o_ref, so_ref, osc_ref = rest[-3:]
    s_ref = s_ref.at[0]
    for other in range(so_ref.shape[0]):
        if other != slab:
            so_ref[other] = jnp.zeros(so_ref.shape[1:], so_ref.dtype)
    so_ref = so_ref.at[slab]
    g = pl.program_id(1)
    lb = lb_ref[0]
    logf, k = _gates(z_ref[...], lb)
    f = jnp.exp(logf)
    shift = (DEC_BATCH - g * bg) % DEC_BATCH
    ft = pltpu.roll(f.T, shift, 1)
    kt = pltpu.roll(k.T, shift, 1)
    rows = pl.ds(pl.multiple_of(g * bg, bg), bg)
    v = v_ref[rows, :]
    q = q_ref[rows, :]
    for j in range(bg):
        s_new = ft[:, j:j + 1] * s_ref[j, 0] + kt[:, j:j + 1] * v[j:j + 1, :]
        so_ref[j, 0] = s_new
        osc_ref[j:j + 1, :] = jnp.dot(q[j:j + 1, :].astype(BF16), s_new.astype(BF16),
                                      preferred_element_type=F32)
    o = osc_ref[...]
    o_ref[...] = (_rms(o, gn_ref[0]) * _silu(og_ref[rows, :])).astype(o_ref.dtype)


def _hgrn_sample(proj, state_all, l, lb, hgrn_norm, mix, new_state_all=None, *, bg=32):
    rb = PROMPT_TILES
    col0 = 3 * CONV_DIM // LANES
    blk = lambda part: pl.BlockSpec((DEC_BATCH, LANES), lambda h, g: (rb, col0 + part * HG_HEADS + h))
    hvec = pl.BlockSpec((1, 1, LANES), lambda h, g: (h, 0, 0))
    sblk = pl.BlockSpec((1, bg, 1, HG_DK, HG_DV), lambda h, g: (l, g, h, 0, 0))
    any_spec = pl.BlockSpec(memory_space=pl.ANY)
    args = [proj, proj, proj, proj, lb.reshape(HG_HEADS, 1, HG_DK), hgrn_norm.reshape(HG_HEADS, 1, HG_DV),
            state_all, mix]
    in_specs = [blk(0), blk(1), blk(2), blk(3), hvec, hvec, sblk, any_spec]
    aliases = {7: 0}
    if new_state_all is not None:
        args.append(new_state_all)
        in_specs.append(any_spec)
        aliases[8] = 1
        so_blk, slab = sblk, 0
    else:
        n_slabs = state_all.shape[0]
        so_blk = pl.BlockSpec((n_slabs, bg, 1, HG_DK, HG_DV), lambda h, g: (0, g, h, 0, 0))
        slab = l
    return pl.pallas_call(
        functools.partial(_hgrn_sample_kernel, bg=bg, slab=slab),
        grid=(HG_HEADS, DEC_BATCH // bg),
        in_specs=in_specs,
        out_specs=[pl.BlockSpec((bg, LANES), lambda h, g: (N_PROMPT // bg + g, CONV_DIM // LANES + h)), so_blk],
        out_shape=[jax.ShapeDtypeStruct(mix.shape, mix.dtype),
                   jax.ShapeDtypeStruct(state_all.shape, F32)],
        scratch_shapes=[pltpu.VMEM((bg, HG_DV), F32)],
        input_output_aliases=aliases,
        compiler_params=_cparams(("arbitrary", "arbitrary")),
    )(*args)


def _route_tile(logits, ri_ref, gate_ref, cnt_ref, carry_ref):
    i = pl.program_id(0)

    @pl.when(i == 0)
    def _():
        carry_ref[...] = jnp.zeros_like(carry_ref)

    tile = logits.shape[0]
    lane = lax.broadcasted_iota(I32, (tile, LANES), 1)
    lanef = lane.astype(F32)
    valid = i * tile + lax.broadcasted_iota(I32, (tile, LANES), 0) < N_ROWS
    lg = jnp.where(valid & (lane < N_EXPERTS), logits, jnp.where(valid, -jnp.inf, 0.0))
    m1 = jnp.max(lg, axis=-1, keepdims=True)
    i1 = jnp.min(jnp.where(lg == m1, lanef, float(LANES)), axis=-1, keepdims=True).astype(I32)
    lg2 = jnp.where(lane == i1, -jnp.inf, lg)
    m2 = jnp.max(lg2, axis=-1, keepdims=True)
    i2 = jnp.min(jnp.where(lg2 == m2, lanef, float(LANES)), axis=-1, keepdims=True).astype(I32)
    e = jnp.exp(m2 - m1)
    g1 = 1.0 / (1.0 + e)
    g2 = e / (1.0 + e)
    hot1 = lane == i1
    hot2 = lane == i2
    hot = jnp.where(valid, (hot1 | hot2).astype(F32), 0.0).astype(BF16)
    r = lax.broadcasted_iota(I32, (tile, tile), 0)
    c = lax.broadcasted_iota(I32, (tile, tile), 1)
    before = (c < r).astype(BF16)
    tot = jnp.dot(before, hot, preferred_element_type=F32) + carry_ref[...]
    r1 = jnp.sum(jnp.where(hot1, tot, 0.0), axis=-1, keepdims=True).astype(I32)
    r2 = jnp.sum(jnp.where(hot2, tot, 0.0), axis=-1, keepdims=True).astype(I32)
    ri_ref[...] = jnp.where(lane == 0, i1, jnp.where(lane == 1, i2, jnp.where(lane == 2, r1,
                            jnp.where(lane == 3, r2, 0))))
    gate_ref[...] = jnp.where(lane == 0, g1, jnp.where(lane == 1, g2, 0.0))
    carry_ref[...] += jnp.sum(hot.astype(F32), axis=0, keepdims=True)
    cnt_ref[...] = carry_ref[...]


DISPATCH_CHUNK = 128


def _dispatch_kernel(d1_ref, d2_ref, cnt_ref, pstart_ref, nact_ref, h_ref, xb_ref, zero_ref, sem, fill_sem):
    n_chunks = N_ROWS // DISPATCH_CHUNK

    def row_copy(src, dst, s):
        return pltpu.make_async_copy(h_ref.at[pl.ds(src, 1), :], xb_ref.at[pl.ds(dst, 1), :], s)

    def chunk_wait(c):
        rows = pl.ds(0, 2 * DISPATCH_CHUNK)
        pltpu.make_async_copy(h_ref.at[rows, :], xb_ref.at[rows, :], sem.at[c % 2]).wait()

    def chunk(c, carry):
        def token(r, carry2):
            t = c * DISPATCH_CHUNK + r
            row_copy(t, d1_ref[t], sem.at[c % 2]).start()
            row_copy(t, d2_ref[t], sem.at[c % 2]).start()
            return carry2
        lax.fori_loop(0, DISPATCH_CHUNK, token, 0, unroll=8)

        @pl.when(c > 0)
        def _():
            chunk_wait(c - 1)
        return carry

    lax.fori_loop(0, n_chunks, chunk, 0)
    chunk_wait(n_chunks - 1)

    for e in range(N_EXPERTS):
        first = pstart_ref[e] + cnt_ref[e]
        n_pad = (MOE_BLOCK - cnt_ref[e] % MOE_BLOCK) % MOE_BLOCK

        def pad_start(r, carry, first=first):
            row_copy(0, first + r, fill_sem).start()
            return carry
        lax.fori_loop(0, n_pad, pad_start, 0)

        def pad_wait(r, carry, first=first):
            row_copy(0, first + r, fill_sem).wait()
            return carry
        lax.fori_loop(0, n_pad, pad_wait, 0)

    zero_ref[...] = jnp.zeros_like(zero_ref)

    def zero_block(b, carry):
        c = pltpu.make_async_copy(zero_ref, xb_ref.at[pl.ds(pl.multiple_of(b * MOE_BLOCK, MOE_BLOCK), MOE_BLOCK), :],
                                  fill_sem)
        c.start()
        c.wait()
        return carry
    lax.fori_loop(nact_ref[0], N_BLOCKS, zero_block, 0)


def _dispatch(h, dest1, dest2, counts, pstarts, nact):
    any_spec = pl.BlockSpec(memory_space=pl.ANY)
    return pl.pallas_call(
        _dispatch_kernel,
        grid_spec=pltpu.PrefetchScalarGridSpec(
            num_scalar_prefetch=5,
            grid=(1,),
            in_specs=[any_spec],
            out_specs=any_spec,
            scratch_shapes=[pltpu.VMEM((MOE_BLOCK, D_MODEL), F32), pltpu.SemaphoreType.DMA((2,)),
                            pltpu.SemaphoreType.DMA(())],
        ),
        out_shape=jax.ShapeDtypeStruct((P_ROWS, D_MODEL), F32),
        compiler_params=_cparams(("arbitrary",)),
    )(dest1, dest2, counts, pstarts, nact, h)


def _combine_resid_kernel(d1_ref, d2_ref, x_ref, gate_ref, gpost_ref, gap_ref, gas_ref, yb_ref, yp_ref, ys_ref,
                          b1_ref, b2_ref, sem):
    i = pl.program_id(0)

    def start_tile(t):
        slot = t % 2

        def start(r, carry):
            tok = t * ROW_TILE + r
            pltpu.make_async_copy(yb_ref.at[pl.ds(d1_ref[tok], 1), :], b1_ref.at[slot, pl.ds(r, 1), :],
                                  sem.at[0, slot]).start()
            pltpu.make_async_copy(yb_ref.at[pl.ds(d2_ref[tok], 1), :], b2_ref.at[slot, pl.ds(r, 1), :],
                                  sem.at[1, slot]).start()
            return carry
        lax.fori_loop(0, ROW_TILE, start, 0, unroll=8)

    @pl.when(i == 0)
    def _():
        start_tile(i)

    @pl.when(i + 1 < pl.num_programs(0))
    def _():
        start_tile(i + 1)

    slot = i % 2
    pltpu.make_async_copy(yb_ref.at[pl.ds(0, ROW_TILE), :], b1_ref.at[slot], sem.at[0, slot]).wait()
    pltpu.make_async_copy(yb_ref.at[pl.ds(0, ROW_TILE), :], b2_ref.at[slot], sem.at[1, slot]).wait()
    gate = gate_ref[...]
    f = gate[:, 0:1] * b1_ref[slot] + gate[:, 1:2] * b2_ref[slot]
    x = x_ref[...] + _pick(i, gap_ref, gas_ref) * _rms(f, gpost_ref[...])

    @pl.when(i < PROMPT_TILES)
    def _():
        yp_ref[...] = x

    @pl.when(i == PROMPT_TILES)
    def _():
        ys_ref[...] = x


def _combine_resid(x, yb, gates, dest1, dest2, gpost, ga):
    return pl.pallas_call(
        _combine_resid_kernel,
        grid_spec=pltpu.PrefetchScalarGridSpec(
            num_scalar_prefetch=2,
            grid=(N_TILES,),
            in_specs=[_row_spec(D_MODEL), _row_spec(LANES), _vec_spec(), _modp_spec(), _mods_spec(),
                      pl.BlockSpec(memory_space=pl.ANY)],
            out_specs=[_prompt_rows_spec(), _mods_spec()],
            scratch_shapes=[pltpu.VMEM((2, ROW_TILE, D_MODEL), F32), pltpu.VMEM((2, ROW_TILE, D_MODEL), F32),
                            pltpu.SemaphoreType.DMA((2, 2))],
        ),
        out_shape=[jax.ShapeDtypeStruct((N_PROMPT, D_MODEL), F32), jax.ShapeDtypeStruct((DEC_BATCH, D_MODEL), F32)],
        compiler_params=_cparams(("arbitrary",)),
    )(dest1, dest2, x, gates, gpost, ga[0], ga[1], yb)


def _moe_experts(h, ri, gates, counts, w1, w3, w2):
    counts = counts[0, :N_EXPERTS].astype(I32)
    padded = (counts + MOE_BLOCK - 1) // MOE_BLOCK * MOE_BLOCK
    pends = jnp.cumsum(padded)
    pstarts = pends - padded
    dest1 = pstarts[ri[:, 0]] + ri[:, 2]
    dest2 = pstarts[ri[:, 1]] + ri[:, 3]
    nact = (pends[-1:] // MOE_BLOCK).astype(I32)
    block_start = jnp.arange(N_BLOCKS, dtype=I32) * MOE_BLOCK
    block_e = jnp.minimum(jnp.sum((pends[None, :] <= block_start[:, None]).astype(I32), axis=1), N_EXPERTS - 1)
    run_end = (pends // MOE_BLOCK)[block_e]
    next_e = jnp.where(run_end < nact[0], block_e[jnp.minimum(run_end, N_BLOCKS - 1)], -1).astype(I32)
    xb = _dispatch(h, dest1, dest2, counts, pstarts, nact)
    gb = _emm(xb, (w1, w3), block_e, nact, next_e, tm=MOE_BLOCK, tn=1792, out_dtype=BF16)
    yb = _emm(gb, (w2,), block_e, nact, next_e, tm=MOE_BLOCK, tn=512, out_dtype=F32)
    return yb, gates, dest1, dest2


def _dense(a, ws, e, *, tm, tn, out_dtype):
    nb = a.shape[0] // tm
    return _emm(a, ws, jnp.full((nb,), e, I32), jnp.full((1,), nb, I32), jnp.full((nb,), -1, I32),
                tm=tm, tn=tn, out_dtype=out_dtype)


def kernel(x_prompt, x_sample, state_conv, state_hgrn, c_prompt, c_sample, norm_pre, norm_post, w_mod, b_mod, w_in, conv_w, conv_norm, lb_logits, hgrn_norm, w_out, ffn_w1, ffn_w3, ffn_w2, router_w, router_b, moe_w1, moe_w3, moe_w2):
    p = jax.nn.softmax(lb_logits.astype(F32), axis=0)
    lb_all = jnp.cumsum(p, axis=0) - p[0:1]

    n_cond = BATCH + DEC_BATCH
    cond_rows = (n_cond + SUBLANES - 1) // SUBLANES * SUBLANES
    c_all = jnp.concatenate([c_prompt, c_sample, jnp.zeros((cond_rows - n_cond, D_MODEL), F32)], axis=0)
    mod = _gmm(jnp.concatenate([c_all] * DEPTH, axis=0), w_mod, jnp.arange(DEPTH, dtype=I32),
               jnp.full((1,), DEPTH, I32), tm=cond_rows, tn=1024, out_dtype=F32,
               bias=b_mod.reshape(DEPTH, 1, 6 * D_MODEL), lhs_silu=True)

    def mod_vec(l, j):
        rows = mod[l * cond_rows:l * cond_rows + n_cond, j * D_MODEL:(j + 1) * D_MODEL]
        return rows[:BATCH].reshape(BATCH, 1, D_MODEL), rows[BATCH:]

    x = (x_prompt.reshape(N_PROMPT, D_MODEL), x_sample.reshape(DEC_BATCH, D_MODEL))
    vec = lambda a: a.reshape(1, -1)

    h = _prenorm(x[0], x[1], vec(norm_pre[0, 0]), mod_vec(0, 1), mod_vec(0, 0))
    new_conv_p, new_hgrn_p, new_conv_s, new_hgrn_s = [], [], [], None
    for l in range(DEPTH):
        sh_f, sc_f, ga_f = mod_vec(l, 3), mod_vec(l, 4), mod_vec(l, 5)
        ga_a = mod_vec(l, 2)
        proj = _dense(h, (w_in,), l, tm=1664, tn=1024, out_dtype=F32)
        mix = jnp.zeros((N_ROWS, 2 * CONV_DIM), BF16)
        mix, nc_p = _conv_prompt(proj, conv_w[l], vec(conv_norm[l]), mix)
        mix, ns_p = _hgrn_prompt(proj, lb_all[l], hgrn_norm[l], mix)
        mix, nc_s = _conv_sample(proj, state_conv[l], conv_w[l], vec(conv_norm[l]), mix)
        mix, new_hgrn_s = _hgrn_sample(proj, state_hgrn, l, lb_all[l], hgrn_norm[l], mix, new_hgrn_s)
        mixed = _dense(mix, (w_out,), l, tm=1664, tn=1024, out_dtype=F32)
        new_conv_p.append(nc_p)
        new_hgrn_p.append(ns_p)
        new_conv_s.append(nc_s)
        nxt = (vec(norm_pre[l, 1]), sc_f, sh_f)
        if l % 2 == 0:
            j = l // 2
            x, h2 = _resid(x, mixed, vec(norm_post[l, 0]), ga_a, nxt=nxt)
            g = _dense(h2, (ffn_w1, ffn_w3), j, tm=1664, tn=512, out_dtype=BF16)
            f = _dense(g, (ffn_w2,), j, tm=1040, tn=512, out_dtype=F32)
        else:
            j = l // 2
            rw = jnp.pad(router_w[j].astype(F32), ((0, 0), (0, LANES - N_EXPERTS)))
            rb = jnp.pad(router_b[j].astype(F32), (0, LANES - N_EXPERTS)).reshape(1, LANES)
            x, h2, ri, gates, counts = _resid(x, mixed, vec(norm_post[l, 0]), ga_a, nxt=nxt, router=(rw, rb),
                                              h_dtype=F32)
            f = _moe_experts(h2, ri, gates, counts, moe_w1[j], moe_w3[j], moe_w2[j])
        if l + 1 < DEPTH:
            assert l % 2 == 0, "the expert combine is fused with the trunk's last residual step only"
            nxt = (vec(norm_pre[l + 1, 0]), mod_vec(l + 1, 1), mod_vec(l + 1, 0))
            x, h = _resid(x, f, vec(norm_post[l, 1]), ga_f, nxt=nxt)
        elif l % 2 == 0:
            y_prompt, y_sample = _resid(x, f, vec(norm_post[l, 1]), ga_f, y_split=True)
        else:
            y_prompt, y_sample = _combine_resid(x, *f, vec(norm_post[l, 1]), ga_f)

    y_prompt = y_prompt.reshape(BATCH, SEQ, D_MODEL)
    y_sample = y_sample.reshape(DEC_BATCH, 1, D_MODEL)
    return (y_prompt, y_sample, jnp.stack(new_conv_p), jnp.stack(new_hgrn_p),
            jnp.stack(new_conv_s), new_hgrn_s)
```

```python
import functools

import jax
import jax.numpy as jnp
import numpy as np
from jax import lax
from jax.experimental import pallas as pl
from jax.experimental.pallas import tpu as pltpu

F32 = jnp.float32
BF16 = jnp.bfloat16
I32 = jnp.int32
HIGHEST = lax.Precision.HIGHEST

LANES = 128
SUBLANES = 8
VMEM_LIMIT_BYTES = 56 * 1024 * 1024

D_MODEL = 2048
BATCH = 4
SEQ = 2048
DEPTH = 2
DEC_BATCH = 128
N_PROMPT = BATCH * SEQ
N_ROWS = N_PROMPT + DEC_BATCH
CONV_DIM = 1024
CONV_GROUPS = 16
CONV_W = 3
HG_HEADS = 8
HG_DK = 128
HG_DV = 128
PROJ_WIDTH = 7168
N_EXPERTS = 8
EPS = 1e-6
F_MIN = 1e-6

ROW_TILE = 128
N_TILES = N_ROWS // ROW_TILE
PROMPT_TILES = N_PROMPT // ROW_TILE
TILES_PER_SEQ = SEQ // ROW_TILE
MOE_BLOCK = 256
N_PAIRS = 2 * N_ROWS
N_BLOCKS = (N_PAIRS + N_EXPERTS * (MOE_BLOCK - 1) + MOE_BLOCK - 1) // MOE_BLOCK
P_ROWS = N_BLOCKS * MOE_BLOCK


def _cparams(sem):
    return pltpu.CompilerParams(dimension_semantics=sem, vmem_limit_bytes=VMEM_LIMIT_BYTES)


def _rms(x, g):
    return x * lax.rsqrt(jnp.mean(x * x, axis=-1, keepdims=True) + EPS) * g


def _silu(x):
    return x * jax.nn.sigmoid(x)


def _gmm_kernel(be_ref, nact_ref, a_ref, w_ref, *rest, lhs_silu, has_bias):
    if has_bias:
        b_ref, o_ref, wb_ref = rest
    else:
        o_ref, wb_ref = rest
    i = pl.program_id(1)
    prev = be_ref[jnp.maximum(i - 1, 0)]

    @pl.when((i == 0) | (be_ref[i] != prev))
    def _():
        wb_ref[...] = w_ref[0].astype(BF16)

    @pl.when(i < nact_ref[0])
    def _():
        a = a_ref[...]
        if lhs_silu:
            a = _silu(a)
        acc = jnp.dot(a.astype(BF16), wb_ref[...], preferred_element_type=F32)
        if has_bias:
            acc = acc + b_ref[0]
        o_ref[...] = acc.astype(o_ref.dtype)

    @pl.when(i >= nact_ref[0])
    def _():
        o_ref[...] = jnp.zeros_like(o_ref)


def _gmm(a, w, block_e, nact, *, tm, tn, out_dtype, bias=None, lhs_silu=False):
    m, k = a.shape
    _, _, n = w.shape
    grid = (n // tn, m // tm)
    in_specs = [
        pl.BlockSpec((tm, k), lambda j, i, be, na: (i, 0)),
        pl.BlockSpec((1, k, tn), lambda j, i, be, na: (be[i], 0, j)),
    ]
    args = [a, w]
    if bias is not None:
        in_specs.append(pl.BlockSpec((1, 1, tn), lambda j, i, be, na: (be[i], 0, j)))
        args.append(bias)
    return pl.pallas_call(
        functools.partial(_gmm_kernel, lhs_silu=lhs_silu, has_bias=bias is not None),
        grid_spec=pltpu.PrefetchScalarGridSpec(
            num_scalar_prefetch=2,
            grid=grid,
            in_specs=in_specs,
            out_specs=pl.BlockSpec((tm, tn), lambda j, i, be, na: (i, j)),
            scratch_shapes=[pltpu.VMEM((k, tn), BF16)],
        ),
        out_shape=jax.ShapeDtypeStruct((m, n), out_dtype),
        compiler_params=_cparams(("arbitrary", "arbitrary")),
    )(block_e, nact, *args)


CAST_ROWS = 32


def _emm_kernel(be_ref, nact_ref, nexte_ref, a_hbm, *rest, n_w, tm, nb, col0):
    w_hbm = rest[:n_w]
    o_hbm, abuf, obuf, stage_ref, wb_ref, asem, osem, wsem = rest[n_w:]
    j = pl.program_id(0)
    last_tile = j + 1 == pl.num_programs(0)
    tn = obuf.shape[2]
    nact = nact_ref[0]
    base = (j * nact) % 2

    def a_copy(i, slot):
        rows = pl.ds(pl.multiple_of(i * tm, tm), tm)
        return pltpu.make_async_copy(a_hbm.at[rows, :], abuf.at[slot], asem.at[slot])

    def o_copy(i, slot):
        rows = pl.ds(pl.multiple_of(i * tm, tm), tm)
        cols = pl.ds(pl.multiple_of(j * tn, LANES), tn)
        return pltpu.make_async_copy(obuf.at[slot], o_hbm.at[rows, cols], osem.at[slot])

    def slab_copies(expert, col_tile):
        cols = pl.ds(pl.multiple_of((col0 + col_tile) * tn, LANES), tn)
        return [pltpu.make_async_copy(w_hbm[t].at[expert, :, cols], stage_ref.at[t], wsem.at[t])
                for t in range(n_w)]

    @pl.when(j == 0)
    def _():
        for c in slab_copies(be_ref[0], j):
            c.start()
        a_copy(0, base).start()

    def block(i, carry):
        slot = (base + i) % 2
        e = be_ref[i]

        @pl.when(i + 1 < nact)
        def _():
            a_copy(i + 1, 1 - slot).start()

        @pl.when((i + 1 == nact) & jnp.logical_not(last_tile))
        def _():
            a_copy(0, 1 - slot).start()

        @pl.when((i == 0) | (e != be_ref[jnp.maximum(i - 1, 0)]))
        def _():
            for c in slab_copies(e, j):
                c.wait()

            def round_rows(c, carry):
                rows = pl.ds(pl.multiple_of(c * CAST_ROWS, CAST_ROWS), CAST_ROWS)
                for t in range(n_w):
                    wb_ref[t, rows, :] = stage_ref[t, rows, :].astype(BF16)
                return carry
            lax.fori_loop(0, stage_ref.shape[1] // CAST_ROWS, round_rows, 0)
            nxt = nexte_ref[i]

            @pl.when(nxt >= 0)
            def _():
                for c in slab_copies(nxt, j):
                    c.start()

            @pl.when((nxt < 0) & jnp.logical_not(last_tile))
            def _():
                for c in slab_copies(be_ref[0], j + 1):
                    c.start()

        a_copy(i, slot).wait()

        @pl.when(i >= 2)
        def _():
            o_copy(i - 2, slot).wait()

        a = abuf[slot]
        acc = jnp.dot(a, wb_ref[0], preferred_element_type=F32)
        if n_w == 2:
            acc = _silu(acc) * jnp.dot(a, wb_ref[1], preferred_element_type=F32)
        obuf[slot] = acc.astype(obuf.dtype)
        o_copy(i, slot).start()
        return carry

    lax.fori_loop(0, nact, block, 0)

    @pl.when(nact >= 2)
    def _():
        o_copy(nact - 2, (base + nact) % 2).wait()
    o_copy(nact - 1, (base + nact - 1) % 2).wait()

    @pl.when(nact < nb)
    def _():
        obuf[0] = jnp.zeros(obuf.shape[1:], obuf.dtype)

        def zero_block(i, carry):
            c = o_copy(i, 0)
            c.start()
            c.wait()
            return carry
        lax.fori_loop(nact, nb, zero_block, 0)


def _emm(a, ws, block_e, nact, next_e, *, tm, tn, out_dtype, cols=None):
    m, k = a.shape
    col0, n_tiles = cols if cols is not None else (0, ws[0].shape[2] // tn)
    n = n_tiles * tn
    n_w = len(ws)
    any_spec = pl.BlockSpec(memory_space=pl.ANY)
    return pl.pallas_call(
        functools.partial(_emm_kernel, n_w=n_w, tm=tm, nb=m // tm, col0=col0),
        grid_spec=pltpu.PrefetchScalarGridSpec(
            num_scalar_prefetch=3,
            grid=(n_tiles,),
            in_specs=[any_spec] * (1 + n_w),
            out_specs=any_spec,
            scratch_shapes=[pltpu.VMEM((2, tm, k), BF16), pltpu.VMEM((2, tm, tn), out_dtype),
                            pltpu.VMEM((n_w, k, tn), F32), pltpu.VMEM((n_w, k, tn), BF16),
                            pltpu.SemaphoreType.DMA((2,)), pltpu.SemaphoreType.DMA((2,)),
                            pltpu.SemaphoreType.DMA((n_w,))],
        ),
        out_shape=jax.ShapeDtypeStruct((m, n), out_dtype),
        compiler_params=_cparams(("arbitrary",)),
    )(block_e, nact, next_e, a, *ws)


ELT_TILE = 256


def _rep_rows(s, tile):
    return s if tile == DEC_BATCH else jnp.concatenate([s] * (tile // DEC_BATCH), axis=0)


def _pick(i, p_ref, s_ref, tile=ROW_TILE):
    return jnp.where(i < N_PROMPT // tile, p_ref[0], _rep_rows(s_ref[...], tile))


def _prenorm_kernel(xp_ref, xs_ref, g_ref, scp_ref, scs_ref, shp_ref, shs_ref, h_ref):
    i = pl.program_id(0)
    tile = h_ref.shape[0]
    x = jnp.where(i < N_PROMPT // tile, xp_ref[...], _rep_rows(xs_ref[...], tile))
    sc = _pick(i, scp_ref, scs_ref, tile)
    sh = _pick(i, shp_ref, shs_ref, tile)
    h_ref[...] = (_rms(x, g_ref[...]) * (1.0 + sc) + sh).astype(h_ref.dtype)


def _resid_kernel(*refs, x_split, y_split, with_next, with_router, h_dtype):
    it = iter(refs)
    if x_split:
        xp_ref, xs_ref = next(it), next(it)
    else:
        x_ref = next(it)
    f_ref, gpost_ref, gap_ref, gas_ref = [next(it) for _ in range(4)]
    if with_next:
        gpre_ref, scp_ref, scs_ref, shp_ref, shs_ref = [next(it) for _ in range(5)]
    if with_router:
        rw_ref, rb_ref = next(it), next(it)
    if y_split:
        yp_ref, ys_ref = next(it), next(it)
    else:
        xo_ref = next(it)
    if with_next:
        h_ref = next(it)
    if with_router:
        ri_ref, gate_ref, cnt_ref, carry_ref = [next(it) for _ in range(4)]
    i = pl.program_id(0)
    tile = f_ref.shape[0]
    prompt_tiles = N_PROMPT // tile
    ga = _pick(i, gap_ref, gas_ref, tile)
    if x_split:
        x = jnp.where(i < prompt_tiles, xp_ref[...], _rep_rows(xs_ref[...], tile))
    else:
        x = x_ref[...]
    x = x + ga * _rms(f_ref[...], gpost_ref[...])
    if y_split:
        @pl.when(i < prompt_tiles)
        def _():
            yp_ref[...] = x

        @pl.when(i == prompt_tiles)
        def _():
            ys_ref[...] = x[:DEC_BATCH]
    else:
        xo_ref[...] = x
    if with_next:
        sc = _pick(i, scp_ref, scs_ref, tile)
        sh = _pick(i, shp_ref, shs_ref, tile)
        h = _rms(x, gpre_ref[...]) * (1.0 + sc) + sh
        h_ref[...] = h.astype(h_dtype)
        if with_router:
            logits = jnp.dot(h, rw_ref[...], precision=HIGHEST, preferred_element_type=F32) + rb_ref[...]
            _route_tile(logits, ri_ref, gate_ref, cnt_ref, carry_ref)


def _n_tiles(tile):
    return N_PROMPT // tile + 1


def _row_spec(width, tile=ROW_TILE):
    return pl.BlockSpec((tile, width), lambda i, *_: (i, 0))


def _vec_spec():
    return pl.BlockSpec((1, D_MODEL), lambda i, *_: (0, 0))


def _modp_spec(tile=ROW_TILE):
    return pl.BlockSpec((1, 1, D_MODEL), lambda i, *_: (jnp.minimum(i // (SEQ // tile), BATCH - 1), 0, 0))


def _mods_spec():
    return pl.BlockSpec((DEC_BATCH, D_MODEL), lambda i, *_: (0, 0))


def _prompt_rows_spec(tile=ROW_TILE):
    return pl.BlockSpec((tile, D_MODEL), lambda i, *_: (jnp.minimum(i, N_PROMPT // tile - 1), 0))


def _prenorm(xp, xs, g, sc, sh):
    t = ELT_TILE
    return pl.pallas_call(
        _prenorm_kernel,
        grid=(_n_tiles(t),),
        in_specs=[_prompt_rows_spec(t), _mods_spec(), _vec_spec(), _modp_spec(t), _mods_spec(), _modp_spec(t),
                  _mods_spec()],
        out_specs=_row_spec(D_MODEL, t),
        out_shape=jax.ShapeDtypeStruct((N_ROWS, D_MODEL), BF16),
        compiler_params=_cparams(("arbitrary",)),
    )(xp, xs, g, sc[0], sc[1], sh[0], sh[1])


def _resid(x, f, gpost, ga, nxt=None, router=None, h_dtype=BF16, y_split=False):
    t = ELT_TILE
    x_split = isinstance(x, tuple)
    if x_split:
        args = [x[0], x[1]]
        in_specs = [_prompt_rows_spec(t), _mods_spec()]
    else:
        args = [x]
        in_specs = [_row_spec(D_MODEL, t)]
    args += [f, gpost, ga[0], ga[1]]
    in_specs += [_row_spec(D_MODEL, t), _vec_spec(), _modp_spec(t), _mods_spec()]
    if y_split:
        out_shape = [jax.ShapeDtypeStruct((N_PROMPT, D_MODEL), F32), jax.ShapeDtypeStruct((DEC_BATCH, D_MODEL), F32)]
        out_specs = [_prompt_rows_spec(t), _mods_spec()]
    else:
        out_shape = [jax.ShapeDtypeStruct((N_ROWS, D_MODEL), F32)]
        out_specs = [_row_spec(D_MODEL, t)]
    if nxt is not None:
        gpre, sc, sh = nxt
        args += [gpre, sc[0], sc[1], sh[0], sh[1]]
        in_specs += [_vec_spec(), _modp_spec(t), _mods_spec(), _modp_spec(t), _mods_spec()]
        out_shape.append(jax.ShapeDtypeStruct((N_ROWS, D_MODEL), h_dtype))
        out_specs.append(_row_spec(D_MODEL, t))
    if router is not None:
        rw, rb = router
        args += [rw, rb]
        in_specs += [pl.BlockSpec((D_MODEL, LANES), lambda i: (0, 0)), pl.BlockSpec((1, LANES), lambda i: (0, 0))]
        out_shape += [jax.ShapeDtypeStruct((N_ROWS, LANES), I32), jax.ShapeDtypeStruct((N_ROWS, LANES), F32),
                      jax.ShapeDtypeStruct((1, LANES), F32)]
        out_specs += [_row_spec(LANES, t), _row_spec(LANES, t), pl.BlockSpec((1, LANES), lambda i: (0, 0))]
        scratch = [pltpu.VMEM((1, LANES), F32)]
    else:
        scratch = []
    return pl.pallas_call(
        functools.partial(_resid_kernel, x_split=x_split, y_split=y_split, with_next=nxt is not None,
                          with_router=router is not None, h_dtype=h_dtype),
        grid=(_n_tiles(t),),
        in_specs=in_specs,
        out_specs=out_specs,
        out_shape=out_shape,
        scratch_shapes=scratch,
        compiler_params=_cparams(("arbitrary",)),
    )(*args)


def _group_tables():
    grp = np.arange(CONV_DIM)[:, None] // (CONV_DIM // CONV_GROUPS) == np.arange(LANES)[None, :]
    return jnp.asarray(grp, BF16), jnp.asarray(grp.T, BF16)


def _split2(x):
    hi = x.astype(BF16)
    return hi, (x - hi.astype(F32)).astype(BF16)


def _group_rms(y, gnorm, sel, sel_t):
    width = CONV_DIM // CONV_GROUPS
    hi, lo = _split2(y * y)
    ss = jnp.dot(hi, sel, preferred_element_type=F32) + jnp.dot(lo, sel, preferred_element_type=F32)
    hi, lo = _split2(lax.rsqrt(ss * (1.0 / width) + EPS))
    scale = jnp.dot(hi, sel_t, preferred_element_type=F32) + jnp.dot(lo, sel_t, preferred_element_type=F32)
    return y * scale * gnorm


def _conv_prompt_kernel(hc_ref, bg_ref, cg_ref, hch_ref, cgh_ref, cw_ref, gn_ref, sel_ref, selt_ref, mix_ref,
                        y_ref, nc_ref, *, tt):
    del mix_ref
    t = pl.program_id(1)
    u = cg_ref[...] * hc_ref[...]
    halo = jnp.where(t == 0, 0.0, cgh_ref[...] * hch_ref[...])
    h1 = halo[SUBLANES - 1:SUBLANES]
    h2 = halo[SUBLANES - 2:SUBLANES - 1]
    row = lax.broadcasted_iota(I32, u.shape, 0)
    u1 = jnp.where(row == 0, h1, pltpu.roll(u, 1, 0))
    u2 = jnp.where(row == 0, h2, jnp.where(row == 1, h1, pltpu.roll(u, 2, 0)))
    conv = cw_ref[0:1] * u2 + cw_ref[1:2] * u1 + cw_ref[2:3] * u
    y_ref[...] = _group_rms(bg_ref[...] * conv, gn_ref[...], sel_ref[...], selt_ref[...]).astype(y_ref.dtype)

    @pl.when(t == pl.num_programs(1) - 1)
    def _():
        nc_ref[0] = u[tt - (CONV_W - 1):]


def _conv_prompt(proj, conv_w, conv_norm, mix, *, tt=256):
    nt = SEQ // tt
    cblk = lambda c: pl.BlockSpec((tt, CONV_DIM), lambda b, t: (b * nt + t, c))
    hblk = lambda c: pl.BlockSpec(
        (SUBLANES, CONV_DIM), lambda b, t: (jnp.maximum((b * nt + t) * (tt // SUBLANES) - 1, 0), c))
    return pl.pallas_call(
        functools.partial(_conv_prompt_kernel, tt=tt),
        grid=(BATCH, nt),
        in_specs=[cblk(0), cblk(1), cblk(2), hblk(0), hblk(2),
                  pl.BlockSpec((CONV_W, CONV_DIM), lambda b, t: (0, 0)),
                  pl.BlockSpec((1, CONV_DIM), lambda b, t: (0, 0)),
                  pl.BlockSpec((CONV_DIM, LANES), lambda b, t: (0, 0)),
                  pl.BlockSpec((LANES, CONV_DIM), lambda b, t: (0, 0)),
                  pl.BlockSpec(memory_space=pl.ANY)],
        out_specs=[pl.BlockSpec((tt, CONV_DIM), lambda b, t: (b * nt + t, 0)),
                   pl.BlockSpec((1, CONV_W - 1, CONV_DIM), lambda b, t: (b, 0, 0))],
        out_shape=[jax.ShapeDtypeStruct(mix.shape, mix.dtype),
                   jax.ShapeDtypeStruct((BATCH, CONV_W - 1, CONV_DIM), F32)],
        input_output_aliases={9: 0},
        compiler_params=_cparams(("arbitrary", "arbitrary")),
    )(proj, proj, proj, proj, proj, conv_w, conv_norm, *_group_tables(), mix)


HG_LEVELS = tuple(1 << i for i in range(ROW_TILE.bit_length() - 1))
LOG2_E = 1.4426950408889634


def _gates(z, lb):
    sg = jax.nn.sigmoid(z)
    f = lb + (1.0 - lb) * sg
    return jnp.log(jnp.maximum(f, F_MIN)), (1.0 - lb) * (1.0 - sg)


def _level_table():
    t = np.arange(ROW_TILE)[:, None]
    s = np.arange(ROW_TILE)[None, :]
    x = np.maximum(t ^ s, 1)
    lvl = np.floor(np.log2(x)).astype(np.int32)
    lvl = np.where(t == s, len(HG_LEVELS), np.where(s < t, lvl, -1))
    return jnp.asarray(lvl, I32)


def _nt_dot(x, y):
    return lax.dot_general(x, y, (((1,), (1,)), ((), ())), preferred_element_type=F32)


def _hgrn_tile(q, z, v, lb, st, tri, lvl):
    logf, k = _gates(z, lb)
    lf2 = logf * LOG2_E
    hi = lf2.astype(BF16)
    rem = lf2 - hi.astype(F32)
    mid = rem.astype(BF16)
    lo = (rem - mid.astype(F32)).astype(BF16)
    parts = jnp.dot(tri, jnp.concatenate([hi, mid, lo], axis=1), preferred_element_type=F32)
    a = parts[:, :HG_DK] + parts[:, HG_DK:2 * HG_DK] + parts[:, 2 * HG_DK:]
    row = lax.broadcasted_iota(I32, (ROW_TILE, HG_DK), 0)
    a8 = a.reshape(ROW_TILE // SUBLANES, SUBLANES, HG_DK)
    sub8 = lax.broadcasted_iota(I32, a8.shape, 1)
    scores = jnp.where(lvl == len(HG_LEVELS), _nt_dot(q.astype(BF16), k.astype(BF16)), 0.0)
    for li, c in enumerate(HG_LEVELS):
        if c < SUBLANES:
            if c == 1:
                d = jnp.where((row & 1) == 1, lf2, 0.0)
            elif c == 2:
                anchor = jnp.where(sub8 < 4, a8[:, 1:2, :], a8[:, 5:6, :])
                d = (a8 - anchor).reshape(ROW_TILE, HG_DK)
            else:
                d = (a8 - a8[:, c - 1:c, :]).reshape(ROW_TILE, HG_DK)
            src = jnp.where((row & c) != 0, q, k)
            neg = -jnp.abs(d)
        else:
            shape3 = (ROW_TILE // (2 * c), 2 * c, HG_DK)
            ab, qb, kb = a.reshape(shape3), q.reshape(shape3), k.reshape(shape3)
            anchor = ab[:, c - 1:c, :]
            neg = jnp.concatenate([anchor - ab[:, :c, :], ab[:, c:, :] - anchor], axis=1).reshape(ROW_TILE, HG_DK)
            src = jnp.concatenate([kb[:, :c, :], qb[:, c:, :]], axis=1).reshape(ROW_TILE, HG_DK)
        x = (src * jnp.exp2(neg)).astype(BF16)
        scores = jnp.where(lvl == li, _nt_dot(x, x), scores)
    vb = v.astype(BF16)
    a_last = a[ROW_TILE - 1:ROW_TILE, :]
    o = jnp.dot(scores.astype(BF16), vb, preferred_element_type=F32)
    o = o + _nt_dot((q * jnp.exp2(a)).astype(BF16), st.astype(BF16))
    kt = (k * jnp.exp2(a_last - a)).astype(BF16)
    st = st * jnp.exp2(a_last) + lax.dot_general(vb, kt, (((0,), (0,)), ((), ())),
                                                 preferred_element_type=F32)
    return o, st


HG_WIDTH = HG_HEADS * LANES
HG_PROJ = 4 * HG_WIDTH
HG_COL0 = 3 * CONV_DIM
W_CHUNK = 512


def _hgrn_prompt_kernel(h_ref, hnext_ref, w_hbm, lb_ref, gn_ref, tri_ref, lvl_ref, mix_ref, o_ref, s_ref,
                        st_ref, wb_ref, stage_ref, pa_ref, pb_ref, wsem, *, tt, layer):
    del mix_ref
    t = pl.program_id(1)
    step = pl.program_id(0) * pl.num_programs(1) + t
    n_sub = tt // ROW_TILE

    head_cols = HG_PROJ // HG_HEADS

    def project(x_ref, j, dst_ref, h):
        rows = pl.ds(pl.multiple_of(j * ROW_TILE, ROW_TILE), ROW_TILE)
        cols = slice(h * head_cols, (h + 1) * head_cols)
        dst_ref[rows, cols] = jnp.dot(x_ref[rows, :], wb_ref[:, cols], preferred_element_type=F32)

    @pl.when(step == 0)
    def _():
        def chunk_copy(c, slot):
            cols = pl.ds(HG_COL0 + c * W_CHUNK, W_CHUNK)
            return pltpu.make_async_copy(w_hbm.at[layer, :, cols], stage_ref.at[slot], wsem.at[slot])

        chunk_copy(0, 0).start()
        for c in range(HG_PROJ // W_CHUNK):
            if c + 1 < HG_PROJ // W_CHUNK:
                chunk_copy(c + 1, (c + 1) % 2).start()
            chunk_copy(c, c % 2).wait()

            def round_rows(r, carry, c=c):
                rows = pl.ds(pl.multiple_of(r * CAST_ROWS, CAST_ROWS), CAST_ROWS)
                part, head0 = divmod(c * W_CHUNK, HG_WIDTH)
                for k in range(W_CHUNK // LANES):
                    dst = (head0 // LANES + k) * head_cols + part * LANES
                    wb_ref[rows, dst:dst + LANES] = stage_ref[c % 2, rows, k * LANES:(k + 1) * LANES].astype(BF16)
                return carry
            lax.fori_loop(0, wb_ref.shape[0] // CAST_ROWS, round_rows, 0)
        for j in range(n_sub):
            for h in range(HG_HEADS):
                project(h_ref, j, pa_ref, h)

    @pl.when(t == 0)
    def _():
        st_ref[...] = jnp.zeros_like(st_ref)

    def run(src_ref, dst_ref):
        def body(j, carry):
            rows = pl.ds(pl.multiple_of(j * ROW_TILE, ROW_TILE), ROW_TILE)
            for h in range(HG_HEADS):
                project(hnext_ref, j, dst_ref, h)
                part = lambda p: src_ref[rows, h * head_cols + p * LANES:h * head_cols + (p + 1) * LANES]
                cols = slice(h * LANES, (h + 1) * LANES)
                o, st = _hgrn_tile(part(0), part(1), part(2), lb_ref[:, cols], st_ref[h], tri_ref[...],
                                   lvl_ref[...])
                st_ref[h] = st
                o_ref[rows, cols] = (_rms(o, gn_ref[:, cols]) * _silu(part(3))).astype(o_ref.dtype)
            return carry
        lax.fori_loop(0, n_sub, body, 0)

    @pl.when(step % 2 == 0)
    def _():
        run(pa_ref, pb_ref)

    @pl.when(step % 2 == 1)
    def _():
        run(pb_ref, pa_ref)

    @pl.when(t == pl.num_programs(1) - 1)
    def _():
        for h in range(HG_HEADS):
            s_ref[0, h] = st_ref[h].T


def _hgrn_prompt(h, w_in, layer, lb, hgrn_norm, mix, *, tt=256):
    nt = SEQ // tt
    last = BATCH * nt - 1
    hvec = pl.BlockSpec((1, HG_WIDTH), lambda b, t: (0, 0))
    const = pl.BlockSpec((ROW_TILE, ROW_TILE), lambda b, t: (0, 0))
    any_spec = pl.BlockSpec(memory_space=pl.ANY)
    tri = jnp.asarray(np.tril(np.ones((ROW_TILE, ROW_TILE), np.float32)), BF16)
    return pl.pallas_call(
        functools.partial(_hgrn_prompt_kernel, tt=tt, layer=layer),
        grid=(BATCH, nt),
        in_specs=[pl.BlockSpec((tt, D_MODEL), lambda b, t: (b * nt + t, 0)),
                  pl.BlockSpec((tt, D_MODEL), lambda b, t: (jnp.minimum(b * nt + t + 1, last), 0)),
                  any_spec, hvec, hvec, const, const, any_spec],
        out_specs=[pl.BlockSpec((tt, HG_WIDTH), lambda b, t: (b * nt + t, 1)),
                   pl.BlockSpec((1, HG_HEADS, HG_DK, HG_DV), lambda b, t: (b, 0, 0, 0))],
        out_shape=[jax.ShapeDtypeStruct(mix.shape, mix.dtype),
                   jax.ShapeDtypeStruct((BATCH, HG_HEADS, HG_DK, HG_DV), F32)],
        scratch_shapes=[pltpu.VMEM((HG_HEADS, HG_DV, HG_DK), F32),
                        pltpu.VMEM((D_MODEL, HG_PROJ), BF16),
                        pltpu.VMEM((2, D_MODEL, W_CHUNK), F32),
                        pltpu.VMEM((tt, HG_PROJ), F32), pltpu.VMEM((tt, HG_PROJ), F32),
                        pltpu.SemaphoreType.DMA((2,))],
        input_output_aliases={7: 0},
        compiler_params=_cparams(("arbitrary", "arbitrary")),
    )(h, h, w_in, lb.reshape(1, HG_WIDTH), hgrn_norm.reshape(1, HG_WIDTH), tri, _level_table(), mix)


def _conv_sample_kernel(hc_ref, bg_ref, cg_ref, cs_ref, cw_ref, gn_ref, sel_ref, selt_ref, mix_ref, y_ref, nc_ref):
    del mix_ref
    u = cg_ref[...] * hc_ref[...]
    s0 = cs_ref[:, 0, :]
    s1 = cs_ref[:, 1, :]
    conv = cw_ref[0:1] * s0 + cw_ref[1:2] * s1 + cw_ref[2:3] * u
    y_ref[...] = _group_rms(bg_ref[...] * conv, gn_ref[...], sel_ref[...], selt_ref[...]).astype(y_ref.dtype)
    nc_ref[:, 0, :] = s1
    nc_ref[:, 1, :] = u


def _conv_sample(proj, conv_state, conv_w, conv_norm, mix):
    rb = PROMPT_TILES
    cblk = lambda c: pl.BlockSpec((DEC_BATCH, CONV_DIM), lambda i: (rb, c))
    return pl.pallas_call(
        _conv_sample_kernel,
        grid=(1,),
        in_specs=[cblk(0), cblk(1), cblk(2),
                  pl.BlockSpec((DEC_BATCH, CONV_W - 1, CONV_DIM), lambda i: (0, 0, 0)),
                  pl.BlockSpec((CONV_W, CONV_DIM), lambda i: (0, 0)),
                  pl.BlockSpec((1, CONV_DIM), lambda i: (0, 0)),
                  pl.BlockSpec((CONV_DIM, LANES), lambda i: (0, 0)),
                  pl.BlockSpec((LANES, CONV_DIM), lambda i: (0, 0)),
                  pl.BlockSpec(memory_space=pl.ANY)],
        out_specs=[pl.BlockSpec((DEC_BATCH, CONV_DIM), lambda i: (rb, 0)),
                   pl.BlockSpec((DEC_BATCH, CONV_W - 1, CONV_DIM), lambda i: (0, 0, 0))],
        out_shape=[jax.ShapeDtypeStruct(mix.shape, mix.dtype),
                   jax.ShapeDtypeStruct((DEC_BATCH, CONV_W - 1, CONV_DIM), F32)],
        input_output_aliases={8: 0},
        compiler_params=_cparams(("arbitrary",)),
    )(proj, proj, proj, conv_state, conv_w, conv_norm, *_group_tables(), mix)


def _hgrn_sample_kernel(q_ref, z_ref, v_ref, og_ref, lb_ref, gn_ref, s_ref, *rest, bg, slab):
    o_ref, so_ref, osc_ref = rest[-3:]
    s_ref = s_ref.at[0]
    for other in range(so_ref.shape[0]):
        if other != slab:
            so_ref[other] = jnp.zeros(so_ref.shape[1:], so_ref.dtype)
    so_ref = so_ref.at[slab]
    g = pl.program_id(1)
    lb = lb_ref[0]
    logf, k = _gates(z_ref[...], lb)
    f = jnp.exp(logf)
    shift = (DEC_BATCH - g * bg) % DEC_BATCH
    ft = pltpu.roll(f.T, shift, 1)
    kt = pltpu.roll(k.T, shift, 1)
    rows = pl.ds(pl.multiple_of(g * bg, bg), bg)
    v = v_ref[rows, :]
    q = q_ref[rows, :]
    for j in range(bg):
        s_new = ft[:, j:j + 1] * s_ref[j, 0] + kt[:, j:j + 1] * v[j:j + 1, :]
        so_ref[j, 0] = s_new
        osc_ref[j:j + 1, :] = jnp.dot(q[j:j + 1, :].astype(BF16), s_new.astype(BF16),
                                      preferred_element_type=F32)
    o = osc_ref[...]
    o_ref[...] = (_rms(o, gn_ref[0]) * _silu(og_ref[rows, :])).astype(o_ref.dtype)


def _hgrn_sample(proj, state_all, l, lb, hgrn_norm, mix, new_state_all=None, *, bg=32):
    blk = lambda part: pl.BlockSpec((DEC_BATCH, LANES), lambda h, g: (0, part * HG_HEADS + h))
    hvec = pl.BlockSpec((1, 1, LANES), lambda h, g: (h, 0, 0))
    sblk = pl.BlockSpec((1, bg, 1, HG_DK, HG_DV), lambda h, g: (l, g, h, 0, 0))
    any_spec = pl.BlockSpec(memory_space=pl.ANY)
    args = [proj, proj, proj, proj, lb.reshape(HG_HEADS, 1, HG_DK), hgrn_norm.reshape(HG_HEADS, 1, HG_DV),
            state_all, mix]
    in_specs = [blk(0), blk(1), blk(2), blk(3), hvec, hvec, sblk, any_spec]
    aliases = {7: 0}
    if new_state_all is not None:
        args.append(new_state_all)
        in_specs.append(any_spec)
        aliases[8] = 1
        so_blk, slab = sblk, 0
    else:
        n_slabs = state_all.shape[0]
        so_blk = pl.BlockSpec((n_slabs, bg, 1, HG_DK, HG_DV), lambda h, g: (0, g, h, 0, 0))
        slab = l
    return pl.pallas_call(
        functools.partial(_hgrn_sample_kernel, bg=bg, slab=slab),
        grid=(HG_HEADS, DEC_BATCH // bg),
        in_specs=in_specs,
        out_specs=[pl.BlockSpec((bg, LANES), lambda h, g: (N_PROMPT // bg + g, CONV_DIM // LANES + h)), so_blk],
        out_shape=[jax.ShapeDtypeStruct(mix.shape, mix.dtype),
                   jax.ShapeDtypeStruct(state_all.shape, F32)],
        scratch_shapes=[pltpu.VMEM((bg, HG_DV), F32)],
        input_output_aliases=aliases,
        compiler_params=_cparams(("arbitrary", "arbitrary")),
    )(*args)


def _route_tile(logits, ri_ref, gate_ref, cnt_ref, carry_ref):
    i = pl.program_id(0)

    @pl.when(i == 0)
    def _():
        carry_ref[...] = jnp.zeros_like(carry_ref)

    tile = logits.shape[0]
    lane = lax.broadcasted_iota(I32, (tile, LANES), 1)
    lanef = lane.astype(F32)
    valid = i * tile + lax.broadcasted_iota(I32, (tile, LANES), 0) < N_ROWS
    lg = jnp.where(valid & (lane < N_EXPERTS), logits, jnp.where(valid, -jnp.inf, 0.0))
    m1 = jnp.max(lg, axis=-1, keepdims=True)
    i1 = jnp.min(jnp.where(lg == m1, lanef, float(LANES)), axis=-1, keepdims=True).astype(I32)
    lg2 = jnp.where(lane == i1, -jnp.inf, lg)
    m2 = jnp.max(lg2, axis=-1, keepdims=True)
    i2 = jnp.min(jnp.where(lg2 == m2, lanef, float(LANES)), axis=-1, keepdims=True).astype(I32)
    e = jnp.exp(m2 - m1)
    g1 = 1.0 / (1.0 + e)
    g2 = e / (1.0 + e)
    hot1 = lane == i1
    hot2 = lane == i2
    hot = jnp.where(valid, (hot1 | hot2).astype(F32), 0.0).astype(BF16)
    r = lax.broadcasted_iota(I32, (tile, tile), 0)
    c = lax.broadcasted_iota(I32, (tile, tile), 1)
    before = (c < r).astype(BF16)
    tot = jnp.dot(before, hot, preferred_element_type=F32) + carry_ref[...]
    r1 = jnp.sum(jnp.where(hot1, tot, 0.0), axis=-1, keepdims=True).astype(I32)
    r2 = jnp.sum(jnp.where(hot2, tot, 0.0), axis=-1, keepdims=True).astype(I32)
    ri_ref[...] = jnp.where(lane == 0, i1, jnp.where(lane == 1, i2, jnp.where(lane == 2, r1,
                            jnp.where(lane == 3, r2, 0))))
    gate_ref[...] = jnp.where(lane == 0, g1, jnp.where(lane == 1, g2, 0.0))
    carry_ref[...] += jnp.sum(hot.astype(F32), axis=0, keepdims=True)
    cnt_ref[...] = carry_ref[...]


def _gather_kernel(d1_ref, d2_ref, nact_ref, h_ref, xb_ref, tok_ref, buf_ref, sem):
    i = pl.program_id(0)

    nact = nact_ref[0]

    @pl.when(i == 0)
    def _():
        def clear(p, carry):
            tok_ref[p] = 0
            return carry
        lax.fori_loop(0, P_ROWS, clear, 0, unroll=8)

        def scatter(t, carry):
            tok_ref[d1_ref[t]] = t
            tok_ref[d2_ref[t]] = t
            return carry
        lax.fori_loop(0, N_ROWS, scatter, 0, unroll=4)

    def start_block(b):
        slot = b % 2

        def start(r, carry):
            tok = tok_ref[b * MOE_BLOCK + r]
            pltpu.make_async_copy(h_ref.at[pl.ds(tok, 1), :], buf_ref.at[slot, pl.ds(r, 1), :],
                                  sem.at[slot]).start()
            return carry
        lax.fori_loop(0, MOE_BLOCK, start, 0, unroll=8)

    @pl.when(i == 0)
    def _():
        start_block(i)

    @pl.when(i + 1 < nact)
    def _():
        start_block(i + 1)

    @pl.when(i < nact)
    def _():
        slot = i % 2
        pltpu.make_async_copy(h_ref.at[pl.ds(0, MOE_BLOCK), :], buf_ref.at[slot], sem.at[slot]).wait()
        xb_ref[...] = buf_ref[slot].astype(xb_ref.dtype)

    @pl.when(i >= nact)
    def _():
        xb_ref[...] = jnp.zeros_like(xb_ref)


def _gather(h, dest1, dest2, nact):
    return pl.pallas_call(
        _gather_kernel,
        grid_spec=pltpu.PrefetchScalarGridSpec(
            num_scalar_prefetch=3,
            grid=(N_BLOCKS,),
            in_specs=[pl.BlockSpec(memory_space=pl.ANY)],
            out_specs=pl.BlockSpec((MOE_BLOCK, D_MODEL), lambda i, d1, d2, na: (i, 0)),
            scratch_shapes=[pltpu.SMEM((P_ROWS,), I32), pltpu.VMEM((2, MOE_BLOCK, D_MODEL), F32),
                            pltpu.SemaphoreType.DMA((2,))],
        ),
        out_shape=jax.ShapeDtypeStruct((P_ROWS, D_MODEL), BF16),
        compiler_params=_cparams(("arbitrary",)),
    )(dest1, dest2, nact, h)


def _combine_resid_kernel(d1_ref, d2_ref, x_ref, gate_ref, gpost_ref, gap_ref, gas_ref, yb_ref, yp_ref, ys_ref,
                          b1_ref, b2_ref, sem):
    i = pl.program_id(0)

    def start_tile(t):
        slot = t % 2

        def start(r, carry):
            tok = t * ROW_TILE + r
            pltpu.make_async_copy(yb_ref.at[pl.ds(d1_ref[tok], 1), :], b1_ref.at[slot, pl.ds(r, 1), :],
                                  sem.at[0, slot]).start()
            pltpu.make_async_copy(yb_ref.at[pl.ds(d2_ref[tok], 1), :], b2_ref.at[slot, pl.ds(r, 1), :],
                                  sem.at[1, slot]).start()
            return carry
        lax.fori_loop(0, ROW_TILE, start, 0, unroll=8)

    @pl.when(i == 0)
    def _():
        start_tile(i)

    @pl.when(i + 1 < pl.num_programs(0))
    def _():
        start_tile(i + 1)

    slot = i % 2
    pltpu.make_async_copy(yb_ref.at[pl.ds(0, ROW_TILE), :], b1_ref.at[slot], sem.at[0, slot]).wait()
    pltpu.make_async_copy(yb_ref.at[pl.ds(0, ROW_TILE), :], b2_ref.at[slot], sem.at[1, slot]).wait()
    gate = gate_ref[...]
    f = gate[:, 0:1] * b1_ref[slot] + gate[:, 1:2] * b2_ref[slot]
    x = x_ref[...] + _pick(i, gap_ref, gas_ref) * _rms(f, gpost_ref[...])

    @pl.when(i < PROMPT_TILES)
    def _():
        yp_ref[...] = x

    @pl.when(i == PROMPT_TILES)
    def _():
        ys_ref[...] = x


def _combine_resid(x, yb, gates, dest1, dest2, gpost, ga):
    return pl.pallas_call(
        _combine_resid_kernel,
        grid_spec=pltpu.PrefetchScalarGridSpec(
            num_scalar_prefetch=2,
            grid=(N_TILES,),
            in_specs=[_row_spec(D_MODEL), _row_spec(LANES), _vec_spec(), _modp_spec(), _mods_spec(),
                      pl.BlockSpec(memory_space=pl.ANY)],
            out_specs=[_prompt_rows_spec(), _mods_spec()],
            scratch_shapes=[pltpu.VMEM((2, ROW_TILE, D_MODEL), F32), pltpu.VMEM((2, ROW_TILE, D_MODEL), F32),
                            pltpu.SemaphoreType.DMA((2, 2))],
        ),
        out_shape=[jax.ShapeDtypeStruct((N_PROMPT, D_MODEL), F32), jax.ShapeDtypeStruct((DEC_BATCH, D_MODEL), F32)],
        compiler_params=_cparams(("arbitrary",)),
    )(dest1, dest2, x, gates, gpost, ga[0], ga[1], yb)


def _moe_experts(h, ri, gates, counts, w1, w3, w2):
    counts = counts[0, :N_EXPERTS].astype(I32)
    padded = (counts + MOE_BLOCK - 1) // MOE_BLOCK * MOE_BLOCK
    pends = jnp.cumsum(padded)
    pstarts = pends - padded
    dest1 = pstarts[ri[:, 0]] + ri[:, 2]
    dest2 = pstarts[ri[:, 1]] + ri[:, 3]
    nact = (pends[-1:] // MOE_BLOCK).astype(I32)
    block_start = jnp.arange(N_BLOCKS, dtype=I32) * MOE_BLOCK
    block_e = jnp.minimum(jnp.sum((pends[None, :] <= block_start[:, None]).astype(I32), axis=1), N_EXPERTS - 1)
    run_end = (pends // MOE_BLOCK)[block_e]
    next_e = jnp.where(run_end < nact[0], block_e[jnp.minimum(run_end, N_BLOCKS - 1)], -1).astype(I32)
    xb = _gather(h, dest1, dest2, nact)
    gb = _emm(xb, (w1, w3), block_e, nact, next_e, tm=MOE_BLOCK, tn=1792, out_dtype=BF16)
    yb = _emm(gb, (w2,), block_e, nact, next_e, tm=MOE_BLOCK, tn=512, out_dtype=F32)
    return yb, gates, dest1, dest2


def _dense(a, ws, e, *, tm, tn, out_dtype, cols=None):
    nb = a.shape[0] // tm
    return _emm(a, ws, jnp.full((nb,), e, I32), jnp.full((1,), nb, I32), jnp.full((nb,), -1, I32),
                tm=tm, tn=tn, out_dtype=out_dtype, cols=cols)


def kernel(x_prompt, x_sample, state_conv, state_hgrn, c_prompt, c_sample, norm_pre, norm_post, w_mod, b_mod, w_in, conv_w, conv_norm, lb_logits, hgrn_norm, w_out, ffn_w1, ffn_w3, ffn_w2, router_w, router_b, moe_w1, moe_w3, moe_w2):
    p = jax.nn.softmax(lb_logits.astype(F32), axis=0)
    lb_all = jnp.cumsum(p, axis=0) - p[0:1]

    n_cond = BATCH + DEC_BATCH
    cond_rows = (n_cond + SUBLANES - 1) // SUBLANES * SUBLANES
    c_all = jnp.concatenate([c_prompt, c_sample, jnp.zeros((cond_rows - n_cond, D_MODEL), F32)], axis=0)
    mod = _gmm(jnp.concatenate([c_all] * DEPTH, axis=0), w_mod, jnp.arange(DEPTH, dtype=I32),
               jnp.full((1,), DEPTH, I32), tm=cond_rows, tn=1024, out_dtype=F32,
               bias=b_mod.reshape(DEPTH, 1, 6 * D_MODEL), lhs_silu=True)

    def mod_vec(l, j):
        rows = mod[l * cond_rows:l * cond_rows + n_cond, j * D_MODEL:(j + 1) * D_MODEL]
        return rows[:BATCH].reshape(BATCH, 1, D_MODEL), rows[BATCH:]

    x = (x_prompt.reshape(N_PROMPT, D_MODEL), x_sample.reshape(DEC_BATCH, D_MODEL))
    vec = lambda a: a.reshape(1, -1)

    h = _prenorm(x[0], x[1], vec(norm_pre[0, 0]), mod_vec(0, 1), mod_vec(0, 0))
    new_conv_p, new_hgrn_p, new_conv_s, new_hgrn_s = [], [], [], None
    for l in range(DEPTH):
        sh_f, sc_f, ga_f = mod_vec(l, 3), mod_vec(l, 4), mod_vec(l, 5)
        ga_a = mod_vec(l, 2)
        tn_in = 1024
        proj = _dense(h, (w_in,), l, tm=1664, tn=tn_in, out_dtype=F32, cols=(0, HG_COL0 // tn_in))
        proj_s = _emm(h[N_PROMPT:], (w_in,), jnp.full((1,), l, I32), jnp.ones((1,), I32), jnp.full((1,), -1, I32),
                      tm=DEC_BATCH, tn=tn_in, out_dtype=F32, cols=(HG_COL0 // tn_in, HG_PROJ // tn_in))
        mix = jnp.zeros((N_ROWS, 2 * CONV_DIM), BF16)
        mix, nc_p = _conv_prompt(proj, conv_w[l], vec(conv_norm[l]), mix)
        mix, ns_p = _hgrn_prompt(h, w_in, l, lb_all[l], hgrn_norm[l], mix)
        mix, nc_s = _conv_sample(proj, state_conv[l], conv_w[l], vec(conv_norm[l]), mix)
        mix, new_hgrn_s = _hgrn_sample(proj_s, state_hgrn, l, lb_all[l], hgrn_norm[l], mix, new_hgrn_s)
        mixed = _dense(mix, (w_out,), l, tm=1664, tn=1024, out_dtype=F32)
        new_conv_p.append(nc_p)
        new_hgrn_p.append(ns_p)
        new_conv_s.append(nc_s)
        nxt = (vec(norm_pre[l, 1]), sc_f, sh_f)
        if l % 2 == 0:
            j = l // 2
            x, h2 = _resid(x, mixed, vec(norm_post[l, 0]), ga_a, nxt=nxt)
            g = _dense(h2, (ffn_w1, ffn_w3), j, tm=1664, tn=512, out_dtype=BF16)
            f = _dense(g, (ffn_w2,), j, tm=1040, tn=512, out_dtype=F32)
        else:
            j = l // 2
            rw = jnp.pad(router_w[j].astype(F32), ((0, 0), (0, LANES - N_EXPERTS)))
            rb = jnp.pad(router_b[j].astype(F32), (0, LANES - N_EXPERTS)).reshape(1, LANES)
            x, h2, ri, gates, counts = _resid(x, mixed, vec(norm_post[l, 0]), ga_a, nxt=nxt, router=(rw, rb),
                                              h_dtype=F32)
            f = _moe_experts(h2, ri, gates, counts, moe_w1[j], moe_w3[j], moe_w2[j])
        if l + 1 < DEPTH:
            assert l % 2 == 0, "the expert combine is fused with the trunk's last residual step only"
            nxt = (vec(norm_pre[l + 1, 0]), mod_vec(l + 1, 1), mod_vec(l + 1, 0))
            x, h = _resid(x, f, vec(norm_post[l, 1]), ga_f, nxt=nxt)
        elif l % 2 == 0:
            y_prompt, y_sample = _resid(x, f, vec(norm_post[l, 1]), ga_f, y_split=True)
        else:
            y_prompt, y_sample = _combine_resid(x, *f, vec(norm_post[l, 1]), ga_f)

    y_prompt = y_prompt.reshape(BATCH, SEQ, D_MODEL)
    y_sample = y_sample.reshape(DEC_BATCH, 1, D_MODEL)
    return (y_prompt, y_sample, jnp.stack(new_conv_p), jnp.stack(new_hgrn_p),
            jnp.stack(new_conv_s), new_hgrn_s)
```

```python
import functools

import jax
import jax.numpy as jnp
import numpy as np
from jax import lax
from jax.experimental import pallas as pl
from jax.experimental.pallas import tpu as pltpu

F32 = jnp.float32
BF16 = jnp.bfloat16
I32 = jnp.int32
HIGHEST = lax.Precision.HIGHEST

LANES = 128
SUBLANES = 8
VMEM_LIMIT_BYTES = 56 * 1024 * 1024

D_MODEL = 2048
BATCH = 4
SEQ = 2048
DEPTH = 2
DEC_BATCH = 128
N_PROMPT = BATCH * SEQ
N_ROWS = N_PROMPT + DEC_BATCH
CONV_DIM = 1024
CONV_GROUPS = 16
CONV_W = 3
HG_HEADS = 8
HG_DK = 128
HG_DV = 128
PROJ_WIDTH = 7168
N_EXPERTS = 8
EPS = 1e-6
F_MIN = 1e-6

ROW_TILE = 128
N_TILES = N_ROWS // ROW_TILE
PROMPT_TILES = N_PROMPT // ROW_TILE
TILES_PER_SEQ = SEQ // ROW_TILE
MOE_BLOCK = 256
N_PAIRS = 2 * N_ROWS
N_BLOCKS = (N_PAIRS + N_EXPERTS * (MOE_BLOCK - 1) + MOE_BLOCK - 1) // MOE_BLOCK
P_ROWS = N_BLOCKS * MOE_BLOCK


def _cparams(sem):
    return pltpu.CompilerParams(dimension_semantics=sem, vmem_limit_bytes=VMEM_LIMIT_BYTES)


def _rms(x, g):
    return x * lax.rsqrt(jnp.mean(x * x, axis=-1, keepdims=True) + EPS) * g


def _silu(x):
    return x * jax.nn.sigmoid(x)


def _gmm_kernel(be_ref, nact_ref, a_ref, w_ref, *rest, lhs_silu, has_bias):
    if has_bias:
        b_ref, o_ref, wb_ref = rest
    else:
        o_ref, wb_ref = rest
    i = pl.program_id(1)
    prev = be_ref[jnp.maximum(i - 1, 0)]

    @pl.when((i == 0) | (be_ref[i] != prev))
    def _():
        wb_ref[...] = w_ref[0].astype(BF16)

    @pl.when(i < nact_ref[0])
    def _():
        a = a_ref[...]
        if lhs_silu:
            a = _silu(a)
        acc = jnp.dot(a.astype(BF16), wb_ref[...], preferred_element_type=F32)
        if has_bias:
            acc = acc + b_ref[0]
        o_ref[...] = acc.astype(o_ref.dtype)

    @pl.when(i >= nact_ref[0])
    def _():
        o_ref[...] = jnp.zeros_like(o_ref)


def _gmm(a, w, block_e, nact, *, tm, tn, out_dtype, bias=None, lhs_silu=False):
    m, k = a.shape
    _, _, n = w.shape
    grid = (n // tn, m // tm)
    in_specs = [
        pl.BlockSpec((tm, k), lambda j, i, be, na: (i, 0)),
        pl.BlockSpec((1, k, tn), lambda j, i, be, na: (be[i], 0, j)),
    ]
    args = [a, w]
    if bias is not None:
        in_specs.append(pl.BlockSpec((1, 1, tn), lambda j, i, be, na: (be[i], 0, j)))
        args.append(bias)
    return pl.pallas_call(
        functools.partial(_gmm_kernel, lhs_silu=lhs_silu, has_bias=bias is not None),
        grid_spec=pltpu.PrefetchScalarGridSpec(
            num_scalar_prefetch=2,
            grid=grid,
            in_specs=in_specs,
            out_specs=pl.BlockSpec((tm, tn), lambda j, i, be, na: (i, j)),
            scratch_shapes=[pltpu.VMEM((k, tn), BF16)],
        ),
        out_shape=jax.ShapeDtypeStruct((m, n), out_dtype),
        compiler_params=_cparams(("arbitrary", "arbitrary")),
    )(block_e, nact, *args)


CAST_ROWS = 32


def _emm_kernel(be_ref, nact_ref, nexte_ref, a_hbm, *rest, n_w, tm, nb):
    w_hbm = rest[:n_w]
    o_hbm, abuf, obuf, stage_ref, wb_ref, asem, osem, wsem = rest[n_w:]
    j = pl.program_id(0)
    last_tile = j + 1 == pl.num_programs(0)
    tn = obuf.shape[2]
    nact = nact_ref[0]
    base = (j * nact) % 2

    def a_copy(i, slot):
        rows = pl.ds(pl.multiple_of(i * tm, tm), tm)
        return pltpu.make_async_copy(a_hbm.at[rows, :], abuf.at[slot], asem.at[slot])

    def o_copy(i, slot):
        rows = pl.ds(pl.multiple_of(i * tm, tm), tm)
        cols = pl.ds(pl.multiple_of(j * tn, LANES), tn)
        return pltpu.make_async_copy(obuf.at[slot], o_hbm.at[rows, cols], osem.at[slot])

    def slab_copies(expert, col_tile):
        cols = pl.ds(pl.multiple_of(col_tile * tn, LANES), tn)
        return [pltpu.make_async_copy(w_hbm[t].at[expert, :, cols], stage_ref.at[t], wsem.at[t])
                for t in range(n_w)]

    @pl.when(j == 0)
    def _():
        for c in slab_copies(be_ref[0], j):
            c.start()
        a_copy(0, base).start()

    def block(i, carry):
        slot = (base + i) % 2
        e = be_ref[i]

        @pl.when(i + 1 < nact)
        def _():
            a_copy(i + 1, 1 - slot).start()

        @pl.when((i + 1 == nact) & jnp.logical_not(last_tile))
        def _():
            a_copy(0, 1 - slot).start()

        @pl.when((i == 0) | (e != be_ref[jnp.maximum(i - 1, 0)]))
        def _():
            for c in slab_copies(e, j):
                c.wait()

            def round_rows(c, carry):
                rows = pl.ds(pl.multiple_of(c * CAST_ROWS, CAST_ROWS), CAST_ROWS)
                for t in range(n_w):
                    wb_ref[t, rows, :] = stage_ref[t, rows, :].astype(BF16)
                return carry
            lax.fori_loop(0, stage_ref.shape[1] // CAST_ROWS, round_rows, 0)
            nxt = nexte_ref[i]

            @pl.when(nxt >= 0)
            def _():
                for c in slab_copies(nxt, j):
                    c.start()

            @pl.when((nxt < 0) & jnp.logical_not(last_tile))
            def _():
                for c in slab_copies(be_ref[0], j + 1):
                    c.start()

        a_copy(i, slot).wait()

        @pl.when(i >= 2)
        def _():
            o_copy(i - 2, slot).wait()

        a = abuf[slot]
        acc = jnp.dot(a, wb_ref[0], preferred_element_type=F32)
        if n_w == 2:
            acc = _silu(acc) * jnp.dot(a, wb_ref[1], preferred_element_type=F32)
        obuf[slot] = acc.astype(obuf.dtype)
        o_copy(i, slot).start()
        return carry

    lax.fori_loop(0, nact, block, 0)

    @pl.when(nact >= 2)
    def _():
        o_copy(nact - 2, (base + nact) % 2).wait()
    o_copy(nact - 1, (base + nact - 1) % 2).wait()

    @pl.when(nact < nb)
    def _():
        obuf[0] = jnp.zeros(obuf.shape[1:], obuf.dtype)

        def zero_block(i, carry):
            c = o_copy(i, 0)
            c.start()
            c.wait()
            return carry
        lax.fori_loop(nact, nb, zero_block, 0)


def _emm(a, ws, block_e, nact, next_e, *, tm, tn, out_dtype):
    m, k = a.shape
    n = ws[0].shape[2]
    n_w = len(ws)
    any_spec = pl.BlockSpec(memory_space=pl.ANY)
    return pl.pallas_call(
        functools.partial(_emm_kernel, n_w=n_w, tm=tm, nb=m // tm),
        grid_spec=pltpu.PrefetchScalarGridSpec(
            num_scalar_prefetch=3,
            grid=(n // tn,),
            in_specs=[any_spec] * (1 + n_w),
            out_specs=any_spec,
            scratch_shapes=[pltpu.VMEM((2, tm, k), BF16), pltpu.VMEM((2, tm, tn), out_dtype),
                            pltpu.VMEM((n_w, k, tn), F32), pltpu.VMEM((n_w, k, tn), BF16),
                            pltpu.SemaphoreType.DMA((2,)), pltpu.SemaphoreType.DMA((2,)),
                            pltpu.SemaphoreType.DMA((n_w,))],
        ),
        out_shape=jax.ShapeDtypeStruct((m, n), out_dtype),
        compiler_params=_cparams(("arbitrary",)),
    )(block_e, nact, next_e, a, *ws)


ELT_TILE = 256


def _rep_rows(s, tile):
    return s if tile == DEC_BATCH else jnp.concatenate([s] * (tile // DEC_BATCH), axis=0)


def _pick(i, p_ref, s_ref, tile=ROW_TILE):
    return jnp.where(i < N_PROMPT // tile, p_ref[0], _rep_rows(s_ref[...], tile))


def _prenorm_kernel(xp_ref, xs_ref, g_ref, scp_ref, scs_ref, shp_ref, shs_ref, h_ref):
    i = pl.program_id(0)
    tile = h_ref.shape[0]
    x = jnp.where(i < N_PROMPT // tile, xp_ref[...], _rep_rows(xs_ref[...], tile))
    sc = _pick(i, scp_ref, scs_ref, tile)
    sh = _pick(i, shp_ref, shs_ref, tile)
    h_ref[...] = (_rms(x, g_ref[...]) * (1.0 + sc) + sh).astype(h_ref.dtype)


def _resid_kernel(*refs, x_split, y_split, with_next, with_router, h_dtype):
    it = iter(refs)
    if x_split:
        xp_ref, xs_ref = next(it), next(it)
    else:
        x_ref = next(it)
    f_ref, gpost_ref, gap_ref, gas_ref = [next(it) for _ in range(4)]
    if with_next:
        gpre_ref, scp_ref, scs_ref, shp_ref, shs_ref = [next(it) for _ in range(5)]
    if with_router:
        rw_ref, rb_ref = next(it), next(it)
    if y_split:
        yp_ref, ys_ref = next(it), next(it)
    else:
        xo_ref = next(it)
    if with_next:
        h_ref = next(it)
    if with_router:
        ri_ref, gate_ref, cnt_ref, carry_ref = [next(it) for _ in range(4)]
    i = pl.program_id(0)
    tile = f_ref.shape[0]
    prompt_tiles = N_PROMPT // tile
    ga = _pick(i, gap_ref, gas_ref, tile)
    if x_split:
        x = jnp.where(i < prompt_tiles, xp_ref[...], _rep_rows(xs_ref[...], tile))
    else:
        x = x_ref[...]
    x = x + ga * _rms(f_ref[...], gpost_ref[...])
    if y_split:
        @pl.when(i < prompt_tiles)
        def _():
            yp_ref[...] = x

        @pl.when(i == prompt_tiles)
        def _():
            ys_ref[...] = x[:DEC_BATCH]
    else:
        xo_ref[...] = x
    if with_next:
        sc = _pick(i, scp_ref, scs_ref, tile)
        sh = _pick(i, shp_ref, shs_ref, tile)
        h = _rms(x, gpre_ref[...]) * (1.0 + sc) + sh
        h_ref[...] = h.astype(h_dtype)
        if with_router:
            logits = jnp.dot(h, rw_ref[...], precision=HIGHEST, preferred_element_type=F32) + rb_ref[...]
            _route_tile(logits, ri_ref, gate_ref, cnt_ref, carry_ref)


def _n_tiles(tile):
    return N_PROMPT // tile + 1


def _row_spec(width, tile=ROW_TILE):
    return pl.BlockSpec((tile, width), lambda i, *_: (i, 0))


def _vec_spec():
    return pl.BlockSpec((1, D_MODEL), lambda i, *_: (0, 0))


def _modp_spec(tile=ROW_TILE):
    return pl.BlockSpec((1, 1, D_MODEL), lambda i, *_: (jnp.minimum(i // (SEQ // tile), BATCH - 1), 0, 0))


def _mods_spec():
    return pl.BlockSpec((DEC_BATCH, D_MODEL), lambda i, *_: (0, 0))


def _prompt_rows_spec(tile=ROW_TILE):
    return pl.BlockSpec((tile, D_MODEL), lambda i, *_: (jnp.minimum(i, N_PROMPT // tile - 1), 0))


def _prenorm(xp, xs, g, sc, sh):
    t = ELT_TILE
    return pl.pallas_call(
        _prenorm_kernel,
        grid=(_n_tiles(t),),
        in_specs=[_prompt_rows_spec(t), _mods_spec(), _vec_spec(), _modp_spec(t), _mods_spec(), _modp_spec(t),
                  _mods_spec()],
        out_specs=_row_spec(D_MODEL, t),
        out_shape=jax.ShapeDtypeStruct((N_ROWS, D_MODEL), BF16),
        compiler_params=_cparams(("arbitrary",)),
    )(xp, xs, g, sc[0], sc[1], sh[0], sh[1])


def _resid(x, f, gpost, ga, nxt=None, router=None, h_dtype=BF16, y_split=False):
    t = ELT_TILE
    x_split = isinstance(x, tuple)
    if x_split:
        args = [x[0], x[1]]
        in_specs = [_prompt_rows_spec(t), _mods_spec()]
    else:
        args = [x]
        in_specs = [_row_spec(D_MODEL, t)]
    args += [f, gpost, ga[0], ga[1]]
    in_specs += [_row_spec(D_MODEL, t), _vec_spec(), _modp_spec(t), _mods_spec()]
    if y_split:
        out_shape = [jax.ShapeDtypeStruct((N_PROMPT, D_MODEL), F32), jax.ShapeDtypeStruct((DEC_BATCH, D_MODEL), F32)]
        out_specs = [_prompt_rows_spec(t), _mods_spec()]
    else:
        out_shape = [jax.ShapeDtypeStruct((N_ROWS, D_MODEL), F32)]
        out_specs = [_row_spec(D_MODEL, t)]
    if nxt is not None:
        gpre, sc, sh = nxt
        args += [gpre, sc[0], sc[1], sh[0], sh[1]]
        in_specs += [_vec_spec(), _modp_spec(t), _mods_spec(), _modp_spec(t), _mods_spec()]
        out_shape.append(jax.ShapeDtypeStruct((N_ROWS, D_MODEL), h_dtype))
        out_specs.append(_row_spec(D_MODEL, t))
    if router is not None:
        rw, rb = router
        args += [rw, rb]
        in_specs += [pl.BlockSpec((D_MODEL, LANES), lambda i: (0, 0)), pl.BlockSpec((1, LANES), lambda i: (0, 0))]
        out_shape += [jax.ShapeDtypeStruct((N_ROWS, LANES), I32), jax.ShapeDtypeStruct((N_ROWS, LANES), F32),
                      jax.ShapeDtypeStruct((1, LANES), F32)]
        out_specs += [_row_spec(LANES, t), _row_spec(LANES, t), pl.BlockSpec((1, LANES), lambda i: (0, 0))]
        scratch = [pltpu.VMEM((1, LANES), F32)]
    else:
        scratch = []
    return pl.pallas_call(
        functools.partial(_resid_kernel, x_split=x_split, y_split=y_split, with_next=nxt is not None,
                          with_router=router is not None, h_dtype=h_dtype),
        grid=(_n_tiles(t),),
        in_specs=in_specs,
        out_specs=out_specs,
        out_shape=out_shape,
        scratch_shapes=scratch,
        compiler_params=_cparams(("arbitrary",)),
    )(*args)


def _group_tables():
    grp = np.arange(CONV_DIM)[:, None] // (CONV_DIM // CONV_GROUPS) == np.arange(LANES)[None, :]
    return jnp.asarray(grp, BF16), jnp.asarray(grp.T, BF16)


def _split2(x):
    hi = x.astype(BF16)
    return hi, (x - hi.astype(F32)).astype(BF16)


def _group_rms(y, gnorm, sel, sel_t):
    width = CONV_DIM // CONV_GROUPS
    hi, lo = _split2(y * y)
    ss = jnp.dot(hi, sel, preferred_element_type=F32) + jnp.dot(lo, sel, preferred_element_type=F32)
    hi, lo = _split2(lax.rsqrt(ss * (1.0 / width) + EPS))
    scale = jnp.dot(hi, sel_t, preferred_element_type=F32) + jnp.dot(lo, sel_t, preferred_element_type=F32)
    return y * scale * gnorm


def _conv_prompt_kernel(hc_ref, bg_ref, cg_ref, hch_ref, cgh_ref, cw_ref, gn_ref, sel_ref, selt_ref, mix_ref,
                        y_ref, nc_ref, *, tt):
    del mix_ref
    t = pl.program_id(1)
    u = cg_ref[...] * hc_ref[...]
    halo = jnp.where(t == 0, 0.0, cgh_ref[...] * hch_ref[...])
    h1 = halo[SUBLANES - 1:SUBLANES]
    h2 = halo[SUBLANES - 2:SUBLANES - 1]
    row = lax.broadcasted_iota(I32, u.shape, 0)
    u1 = jnp.where(row == 0, h1, pltpu.roll(u, 1, 0))
    u2 = jnp.where(row == 0, h2, jnp.where(row == 1, h1, pltpu.roll(u, 2, 0)))
    conv = cw_ref[0:1] * u2 + cw_ref[1:2] * u1 + cw_ref[2:3] * u
    y_ref[...] = _group_rms(bg_ref[...] * conv, gn_ref[...], sel_ref[...], selt_ref[...]).astype(y_ref.dtype)

    @pl.when(t == pl.num_programs(1) - 1)
    def _():
        nc_ref[0] = u[tt - (CONV_W - 1):]


def _conv_prompt(proj, conv_w, conv_norm, mix, *, tt=512):
    nt = SEQ // tt
    cblk = lambda c: pl.BlockSpec((tt, CONV_DIM), lambda b, t: (b * nt + t, c))
    hblk = lambda c: pl.BlockSpec(
        (SUBLANES, CONV_DIM), lambda b, t: (jnp.maximum((b * nt + t) * (tt // SUBLANES) - 1, 0), c))
    return pl.pallas_call(
        functools.partial(_conv_prompt_kernel, tt=tt),
        grid=(BATCH, nt),
        in_specs=[cblk(0), cblk(1), cblk(2), hblk(0), hblk(2),
                  pl.BlockSpec((CONV_W, CONV_DIM), lambda b, t: (0, 0)),
                  pl.BlockSpec((1, CONV_DIM), lambda b, t: (0, 0)),
                  pl.BlockSpec((CONV_DIM, LANES), lambda b, t: (0, 0)),
                  pl.BlockSpec((LANES, CONV_DIM), lambda b, t: (0, 0)),
                  pl.BlockSpec(memory_space=pl.ANY)],
        out_specs=[pl.BlockSpec((tt, CONV_DIM), lambda b, t: (b * nt + t, 0)),
                   pl.BlockSpec((1, CONV_W - 1, CONV_DIM), lambda b, t: (b, 0, 0))],
        out_shape=[jax.ShapeDtypeStruct(mix.shape, mix.dtype),
                   jax.ShapeDtypeStruct((BATCH, CONV_W - 1, CONV_DIM), F32)],
        input_output_aliases={9: 0},
        compiler_params=_cparams(("arbitrary", "arbitrary")),
    )(proj, proj, proj, proj, proj, conv_w, conv_norm, *_group_tables(), mix)


HG_LEVELS = tuple(1 << i for i in range(ROW_TILE.bit_length() - 1))
LOG2_E = 1.4426950408889634


def _gates(z, lb):
    sg = jax.nn.sigmoid(z)
    f = lb + (1.0 - lb) * sg
    return jnp.log(jnp.maximum(f, F_MIN)), (1.0 - lb) * (1.0 - sg)


def _level_table():
    t = np.arange(ROW_TILE)[:, None]
    s = np.arange(ROW_TILE)[None, :]
    x = np.maximum(t ^ s, 1)
    lvl = np.floor(np.log2(x)).astype(np.int32)
    lvl = np.where(t == s, len(HG_LEVELS), np.where(s < t, lvl, -1))
    return jnp.asarray(lvl, I32)


def _nt_dot(x, y):
    return lax.dot_general(x, y, (((1,), (1,)), ((), ())), preferred_element_type=F32)


def _hgrn_tile(q, z, v, lb, st, tri, lvl):
    logf, k = _gates(z, lb)
    lf2 = logf * LOG2_E
    hi = lf2.astype(BF16)
    rem = lf2 - hi.astype(F32)
    mid = rem.astype(BF16)
    lo = (rem - mid.astype(F32)).astype(BF16)
    parts = jnp.dot(tri, jnp.concatenate([hi, mid, lo], axis=1), preferred_element_type=F32)
    a = parts[:, :HG_DK] + parts[:, HG_DK:2 * HG_DK] + parts[:, 2 * HG_DK:]
    row = lax.broadcasted_iota(I32, (ROW_TILE, HG_DK), 0)
    a8 = a.reshape(ROW_TILE // SUBLANES, SUBLANES, HG_DK)
    sub8 = lax.broadcasted_iota(I32, a8.shape, 1)
    scores = jnp.where(lvl == len(HG_LEVELS), _nt_dot(q.astype(BF16), k.astype(BF16)), 0.0)
    for li, c in enumerate(HG_LEVELS):
        if c < SUBLANES:
            if c == 1:
                d = jnp.where((row & 1) == 1, lf2, 0.0)
            elif c == 2:
                anchor = jnp.where(sub8 < 4, a8[:, 1:2, :], a8[:, 5:6, :])
                d = (a8 - anchor).reshape(ROW_TILE, HG_DK)
            else:
                d = (a8 - a8[:, c - 1:c, :]).reshape(ROW_TILE, HG_DK)
            src = jnp.where((row & c) != 0, q, k)
            neg = -jnp.abs(d)
        else:
            shape3 = (ROW_TILE // (2 * c), 2 * c, HG_DK)
            ab, qb, kb = a.reshape(shape3), q.reshape(shape3), k.reshape(shape3)
            anchor = ab[:, c - 1:c, :]
            neg = jnp.concatenate([anchor - ab[:, :c, :], ab[:, c:, :] - anchor], axis=1).reshape(ROW_TILE, HG_DK)
            src = jnp.concatenate([kb[:, :c, :], qb[:, c:, :]], axis=1).reshape(ROW_TILE, HG_DK)
        x = (src * jnp.exp2(neg)).astype(BF16)
        scores = jnp.where(lvl == li, _nt_dot(x, x), scores)
    vb = v.astype(BF16)
    a_last = a[ROW_TILE - 1:ROW_TILE, :]
    o = jnp.dot(scores.astype(BF16), vb, preferred_element_type=F32)
    o = o + _nt_dot((q * jnp.exp2(a)).astype(BF16), st.astype(BF16))
    kt = (k * jnp.exp2(a_last - a)).astype(BF16)
    st = st * jnp.exp2(a_last) + lax.dot_general(vb, kt, (((0,), (0,)), ((), ())),
                                                 preferred_element_type=F32)
    return o, st


def _hgrn_prompt_kernel(q_ref, z_ref, v_ref, og_ref, lb_ref, gn_ref, tri_ref, lvl_ref, mix_ref, o_ref, s_ref,
                        st_ref, *, tt):
    del mix_ref
    t = pl.program_id(1)

    @pl.when(t == 0)
    def _():
        st_ref[...] = jnp.zeros_like(st_ref)

    def body(j, carry):
        rows = pl.ds(pl.multiple_of(j * ROW_TILE, ROW_TILE), ROW_TILE)
        for h in range(HG_HEADS):
            cols = slice(h * LANES, (h + 1) * LANES)
            o, st = _hgrn_tile(q_ref[rows, cols], z_ref[rows, cols], v_ref[rows, cols], lb_ref[:, cols],
                               st_ref[h], tri_ref[...], lvl_ref[...])
            st_ref[h] = st
            o_ref[rows, cols] = (_rms(o, gn_ref[:, cols]) * _silu(og_ref[rows, cols])).astype(o_ref.dtype)
        return carry

    lax.fori_loop(0, tt // ROW_TILE, body, 0)

    @pl.when(t == pl.num_programs(1) - 1)
    def _():
        for h in range(HG_HEADS):
            s_ref[0, h] = st_ref[h].T


def _hgrn_prompt(proj, lb, hgrn_norm, mix, *, tt=512):
    nt = SEQ // tt
    width = HG_HEADS * LANES
    col0 = 3 * CONV_DIM // width
    blk = lambda part: pl.BlockSpec((tt, width), lambda b, t: (b * nt + t, col0 + part))
    hvec = pl.BlockSpec((1, width), lambda b, t: (0, 0))
    const = pl.BlockSpec((ROW_TILE, ROW_TILE), lambda b, t: (0, 0))
    tri = jnp.asarray(np.tril(np.ones((ROW_TILE, ROW_TILE), np.float32)), BF16)
    return pl.pallas_call(
        functools.partial(_hgrn_prompt_kernel, tt=tt),
        grid=(BATCH, nt),
        in_specs=[blk(0), blk(1), blk(2), blk(3), hvec, hvec, const, const, pl.BlockSpec(memory_space=pl.ANY)],
        out_specs=[pl.BlockSpec((tt, width), lambda b, t: (b * nt + t, 1)),
                   pl.BlockSpec((1, HG_HEADS, HG_DK, HG_DV), lambda b, t: (b, 0, 0, 0))],
        out_shape=[jax.ShapeDtypeStruct(mix.shape, mix.dtype),
                   jax.ShapeDtypeStruct((BATCH, HG_HEADS, HG_DK, HG_DV), F32)],
        scratch_shapes=[pltpu.VMEM((HG_HEADS, HG_DV, HG_DK), F32)],
        input_output_aliases={8: 0},
        compiler_params=_cparams(("arbitrary", "arbitrary")),
    )(proj, proj, proj, proj, lb.reshape(1, width), hgrn_norm.reshape(1, width), tri, _level_table(), mix)


def _conv_sample_kernel(hc_ref, bg_ref, cg_ref, cs_ref, cw_ref, gn_ref, sel_ref, selt_ref, mix_ref, y_ref, nc_ref):
    del mix_ref
    u = cg_ref[...] * hc_ref[...]
    s0 = cs_ref[:, 0, :]
    s1 = cs_ref[:, 1, :]
    conv = cw_ref[0:1] * s0 + cw_ref[1:2] * s1 + cw_ref[2:3] * u
    y_ref[...] = _group_rms(bg_ref[...] * conv, gn_ref[...], sel_ref[...], selt_ref[...]).astype(y_ref.dtype)
    nc_ref[:, 0, :] = s1
    nc_ref[:, 1, :] = u


def _conv_sample(proj, conv_state, conv_w, conv_norm, mix):
    rb = PROMPT_TILES
    cblk = lambda c: pl.BlockSpec((DEC_BATCH, CONV_DIM), lambda i: (rb, c))
    return pl.pallas_call(
        _conv_sample_kernel,
        grid=(1,),
        in_specs=[cblk(0), cblk(1), cblk(2),
                  pl.BlockSpec((DEC_BATCH, CONV_W - 1, CONV_DIM), lambda i: (0, 0, 0)),
                  pl.BlockSpec((CONV_W, CONV_DIM), lambda i: (0, 0)),
                  pl.BlockSpec((1, CONV_DIM), lambda i: (0, 0)),
                  pl.BlockSpec((CONV_DIM, LANES), lambda i: (0, 0)),
                  pl.BlockSpec((LANES, CONV_DIM), lambda i: (0, 0)),
                  pl.BlockSpec(memory_space=pl.ANY)],
        out_specs=[pl.BlockSpec((DEC_BATCH, CONV_DIM), lambda i: (rb, 0)),
                   pl.BlockSpec((DEC_BATCH, CONV_W - 1, CONV_DIM), lambda i: (0, 0, 0))],
        out_shape=[jax.ShapeDtypeStruct(mix.shape, mix.dtype),
                   jax.ShapeDtypeStruct((DEC_BATCH, CONV_W - 1, CONV_DIM), F32)],
        input_output_aliases={8: 0},
        compiler_params=_cparams(("arbitrary",)),
    )(proj, proj, proj, conv_state, conv_w, conv_norm, *_group_tables(), mix)


def _hgrn_sample_kernel(q_ref, z_ref, v_ref, og_ref, lb_ref, gn_ref, s_ref, *rest, bg, slab):
    o_ref, so_ref, osc_ref = rest[-3:]
    s_ref = s_ref.at[0]
    for other in range(so_ref.shape[0]):
        if other != slab:
            so_ref[other] = jnp.zeros(so_ref.shape[1:], so_ref.dtype)
    so_ref = so_ref.at[slab]
    g = pl.program_id(1)
    lb = lb_ref[0]
    logf, k = _gates(z_ref[...], lb)
    f = jnp.exp(logf)
    shift = (DEC_BATCH - g * bg) % DEC_BATCH
    ft = pltpu.roll(f.T, shift, 1)
    kt = pltpu.roll(k.T, shift, 1)
    rows = pl.ds(pl.multiple_of(g * bg, bg), bg)
    v = v_ref[rows, :]
    q = q_ref[rows, :]
    for j in range(bg):
        s_new = ft[:, j:j + 1] * s_ref[j, 0] + kt[:, j:j + 1] * v[j:j + 1, :]
        so_ref[j, 0] = s_new
        osc_ref[j:j + 1, :] = jnp.dot(q[j:j + 1, :].astype(BF16), s_new.astype(BF16),
                                      preferred_element_type=F32)
    o = osc_ref[...]
    o_ref[...] = (_rms(o, gn_ref[0]) * _silu(og_ref[rows, :])).astype(o_ref.dtype)


def _hgrn_sample(proj, state_all, l, lb, hgrn_norm, mix, new_state_all=None, *, bg=32):
    rb = PROMPT_TILES
    col0 = 3 * CONV_DIM // LANES
    blk = lambda part: pl.BlockSpec((DEC_BATCH, LANES), lambda h, g: (rb, col0 + part * HG_HEADS + h))
    hvec = pl.BlockSpec((1, 1, LANES), lambda h, g: (h, 0, 0))
    sblk = pl.BlockSpec((1, bg, 1, HG_DK, HG_DV), lambda h, g: (l, g, h, 0, 0))
    any_spec = pl.BlockSpec(memory_space=pl.ANY)
    args = [proj, proj, proj, proj, lb.reshape(HG_HEADS, 1, HG_DK), hgrn_norm.reshape(HG_HEADS, 1, HG_DV),
            state_all, mix]
    in_specs = [blk(0), blk(1), blk(2), blk(3), hvec, hvec, sblk, any_spec]
    aliases = {7: 0}
    if new_state_all is not None:
        args.append(new_state_all)
        in_specs.append(any_spec)
        aliases[8] = 1
        so_blk, slab = sblk, 0
    else:
        n_slabs = state_all.shape[0]
        so_blk = pl.BlockSpec((n_slabs, bg, 1, HG_DK, HG_DV), lambda h, g: (0, g, h, 0, 0))
        slab = l
    return pl.pallas_call(
        functools.partial(_hgrn_sample_kernel, bg=bg, slab=slab),
        grid=(HG_HEADS, DEC_BATCH // bg),
        in_specs=in_specs,
        out_specs=[pl.BlockSpec((bg, LANES), lambda h, g: (N_PROMPT // bg + g, CONV_DIM // LANES + h)), so_blk],
        out_shape=[jax.ShapeDtypeStruct(mix.shape, mix.dtype),
                   jax.ShapeDtypeStruct(state_all.shape, F32)],
        scratch_shapes=[pltpu.VMEM((bg, HG_DV), F32)],
        input_output_aliases=aliases,
        compiler_params=_cparams(("arbitrary", "arbitrary")),
    )(*args)


def _route_tile(logits, ri_ref, gate_ref, cnt_ref, carry_ref):
    i = pl.program_id(0)

    @pl.when(i == 0)
    def _():
        carry_ref[...] = jnp.zeros_like(carry_ref)

    tile = logits.shape[0]
    lane = lax.broadcasted_iota(I32, (tile, LANES), 1)
    lanef = lane.astype(F32)
    valid = i * tile + lax.broadcasted_iota(I32, (tile, LANES), 0) < N_ROWS
    lg = jnp.where(valid & (lane < N_EXPERTS), logits, jnp.where(valid, -jnp.inf, 0.0))
    m1 = jnp.max(lg, axis=-1, keepdims=True)
    i1 = jnp.min(jnp.where(lg == m1, lanef, float(LANES)), axis=-1, keepdims=True).astype(I32)
    lg2 = jnp.where(lane == i1, -jnp.inf, lg)
    m2 = jnp.max(lg2, axis=-1, keepdims=True)
    i2 = jnp.min(jnp.where(lg2 == m2, lanef, float(LANES)), axis=-1, keepdims=True).astype(I32)
    e = jnp.exp(m2 - m1)
    g1 = 1.0 / (1.0 + e)
    g2 = e / (1.0 + e)
    hot1 = lane == i1
    hot2 = lane == i2
    hot = jnp.where(valid, (hot1 | hot2).astype(F32), 0.0).astype(BF16)
    r = lax.broadcasted_iota(I32, (tile, tile), 0)
    c = lax.broadcasted_iota(I32, (tile, tile), 1)
    before = (c < r).astype(BF16)
    tot = jnp.dot(before, hot, preferred_element_type=F32) + carry_ref[...]
    r1 = jnp.sum(jnp.where(hot1, tot, 0.0), axis=-1, keepdims=True).astype(I32)
    r2 = jnp.sum(jnp.where(hot2, tot, 0.0), axis=-1, keepdims=True).astype(I32)
    ri_ref[...] = jnp.where(lane == 0, i1, jnp.where(lane == 1, i2, jnp.where(lane == 2, r1,
                            jnp.where(lane == 3, r2, 0))))
    gate_ref[...] = jnp.where(lane == 0, g1, jnp.where(lane == 1, g2, 0.0))
    carry_ref[...] += jnp.sum(hot.astype(F32), axis=0, keepdims=True)
    cnt_ref[...] = carry_ref[...]


def _gather_kernel(d1_ref, d2_ref, nact_ref, h_ref, xb_ref, tok_ref, buf_ref, sem):
    i = pl.program_id(0)

    nact = nact_ref[0]

    @pl.when(i == 0)
    def _():
        def clear(p, carry):
            tok_ref[p] = 0
            return carry
        lax.fori_loop(0, P_ROWS, clear, 0, unroll=8)

        def scatter(t, carry):
            tok_ref[d1_ref[t]] = t
            tok_ref[d2_ref[t]] = t
            return carry
        lax.fori_loop(0, N_ROWS, scatter, 0, unroll=4)

    def start_block(b):
        slot = b % 2

        def start(r, carry):
            tok = tok_ref[b * MOE_BLOCK + r]
            pltpu.make_async_copy(h_ref.at[pl.ds(tok, 1), :], buf_ref.at[slot, pl.ds(r, 1), :],
                                  sem.at[slot]).start()
            return carry
        lax.fori_loop(0, MOE_BLOCK, start, 0, unroll=8)

    @pl.when(i == 0)
    def _():
        start_block(i)

    @pl.when(i + 1 < nact)
    def _():
        start_block(i + 1)

    @pl.when(i < nact)
    def _():
        slot = i % 2
        pltpu.make_async_copy(h_ref.at[pl.ds(0, MOE_BLOCK), :], buf_ref.at[slot], sem.at[slot]).wait()
        xb_ref[...] = buf_ref[slot].astype(xb_ref.dtype)

    @pl.when(i >= nact)
    def _():
        xb_ref[...] = jnp.zeros_like(xb_ref)


def _gather(h, dest1, dest2, nact):
    return pl.pallas_call(
        _gather_kernel,
        grid_spec=pltpu.PrefetchScalarGridSpec(
            num_scalar_prefetch=3,
            grid=(N_BLOCKS,),
            in_specs=[pl.BlockSpec(memory_space=pl.ANY)],
            out_specs=pl.BlockSpec((MOE_BLOCK, D_MODEL), lambda i, d1, d2, na: (i, 0)),
            scratch_shapes=[pltpu.SMEM((P_ROWS,), I32), pltpu.VMEM((2, MOE_BLOCK, D_MODEL), F32),
                            pltpu.SemaphoreType.DMA((2,))],
        ),
        out_shape=jax.ShapeDtypeStruct((P_ROWS, D_MODEL), BF16),
        compiler_params=_cparams(("arbitrary",)),
    )(dest1, dest2, nact, h)


def _combine_resid_kernel(d1_ref, d2_ref, x_ref, gate_ref, gpost_ref, gap_ref, gas_ref, yb_ref, yp_ref, ys_ref,
                          b1_ref, b2_ref, sem):
    i = pl.program_id(0)

    def start_tile(t):
        slot = t % 2

        def start(r, carry):
            tok = t * ROW_TILE + r
            pltpu.make_async_copy(yb_ref.at[pl.ds(d1_ref[tok], 1), :], b1_ref.at[slot, pl.ds(r, 1), :],
                                  sem.at[0, slot]).start()
            pltpu.make_async_copy(yb_ref.at[pl.ds(d2_ref[tok], 1), :], b2_ref.at[slot, pl.ds(r, 1), :],
                                  sem.at[1, slot]).start()
            return carry
        lax.fori_loop(0, ROW_TILE, start, 0, unroll=8)

    @pl.when(i == 0)
    def _():
        start_tile(i)

    @pl.when(i + 1 < pl.num_programs(0))
    def _():
        start_tile(i + 1)

    slot = i % 2
    pltpu.make_async_copy(yb_ref.at[pl.ds(0, ROW_TILE), :], b1_ref.at[slot], sem.at[0, slot]).wait()
    pltpu.make_async_copy(yb_ref.at[pl.ds(0, ROW_TILE), :], b2_ref.at[slot], sem.at[1, slot]).wait()
    gate = gate_ref[...]
    f = gate[:, 0:1] * b1_ref[slot] + gate[:, 1:2] * b2_ref[slot]
    x = x_ref[...] + _pick(i, gap_ref, gas_ref) * _rms(f, gpost_ref[...])

    @pl.when(i < PROMPT_TILES)
    def _():
        yp_ref[...] = x

    @pl.when(i == PROMPT_TILES)
    def _():
        ys_ref[...] = x


def _combine_resid(x, yb, gates, dest1, dest2, gpost, ga):
    return pl.pallas_call(
        _combine_resid_kernel,
        grid_spec=pltpu.PrefetchScalarGridSpec(
            num_scalar_prefetch=2,
            grid=(N_TILES,),
            in_specs=[_row_spec(D_MODEL), _row_spec(LANES), _vec_spec(), _modp_spec(), _mods_spec(),
                      pl.BlockSpec(memory_space=pl.ANY)],
            out_specs=[_prompt_rows_spec(), _mods_spec()],
            scratch_shapes=[pltpu.VMEM((2, ROW_TILE, D_MODEL), F32), pltpu.VMEM((2, ROW_TILE, D_MODEL), F32),
                            pltpu.SemaphoreType.DMA((2, 2))],
        ),
        out_shape=[jax.ShapeDtypeStruct((N_PROMPT, D_MODEL), F32), jax.ShapeDtypeStruct((DEC_BATCH, D_MODEL), F32)],
        compiler_params=_cparams(("arbitrary",)),
    )(dest1, dest2, x, gates, gpost, ga[0], ga[1], yb)


def _moe_experts(h, ri, gates, counts, w1, w3, w2):
    counts = counts[0, :N_EXPERTS].astype(I32)
    padded = (counts + MOE_BLOCK - 1) // MOE_BLOCK * MOE_BLOCK
    pends = jnp.cumsum(padded)
    pstarts = pends - padded
    dest1 = pstarts[ri[:, 0]] + ri[:, 2]
    dest2 = pstarts[ri[:, 1]] + ri[:, 3]
    nact = (pends[-1:] // MOE_BLOCK).astype(I32)
    block_start = jnp.arange(N_BLOCKS, dtype=I32) * MOE_BLOCK
    block_e = jnp.minimum(jnp.sum((pends[None, :] <= block_start[:, None]).astype(I32), axis=1), N_EXPERTS - 1)
    run_end = (pends // MOE_BLOCK)[block_e]
    next_e = jnp.where(run_end < nact[0], block_e[jnp.minimum(run_end, N_BLOCKS - 1)], -1).astype(I32)
    xb = _gather(h, dest1, dest2, nact)
    gb = _emm(xb, (w1, w3), block_e, nact, next_e, tm=MOE_BLOCK, tn=1792, out_dtype=BF16)
    yb = _emm(gb, (w2,), block_e, nact, next_e, tm=MOE_BLOCK, tn=512, out_dtype=F32)
    return yb, gates, dest1, dest2


def _dense(a, ws, e, *, tm, tn, out_dtype):
    nb = a.shape[0] // tm
    return _emm(a, ws, jnp.full((nb,), e, I32), jnp.full((1,), nb, I32), jnp.full((nb,), -1, I32),
                tm=tm, tn=tn, out_dtype=out_dtype)


def kernel(x_prompt, x_sample, state_conv, state_hgrn, c_prompt, c_sample, norm_pre, norm_post, w_mod, b_mod, w_in, conv_w, conv_norm, lb_logits, hgrn_norm, w_out, ffn_w1, ffn_w3, ffn_w2, router_w, router_b, moe_w1, moe_w3, moe_w2):
    p = jax.nn.softmax(lb_logits.astype(F32), axis=0)
    lb_all = jnp.cumsum(p, axis=0) - p[0:1]

    n_cond = BATCH + DEC_BATCH
    cond_rows = (n_cond + SUBLANES - 1) // SUBLANES * SUBLANES
    c_all = jnp.concatenate([c_prompt, c_sample, jnp.zeros((cond_rows - n_cond, D_MODEL), F32)], axis=0)
    mod = _gmm(jnp.concatenate([c_all] * DEPTH, axis=0), w_mod, jnp.arange(DEPTH, dtype=I32),
               jnp.full((1,), DEPTH, I32), tm=cond_rows, tn=1024, out_dtype=F32,
               bias=b_mod.reshape(DEPTH, 1, 6 * D_MODEL), lhs_silu=True)

    def mod_vec(l, j):
        rows = mod[l * cond_rows:l * cond_rows + n_cond, j * D_MODEL:(j + 1) * D_MODEL]
        return rows[:BATCH].reshape(BATCH, 1, D_MODEL), rows[BATCH:]

    x = (x_prompt.reshape(N_PROMPT, D_MODEL), x_sample.reshape(DEC_BATCH, D_MODEL))
    vec = lambda a: a.reshape(1, -1)

    h = _prenorm(x[0], x[1], vec(norm_pre[0, 0]), mod_vec(0, 1), mod_vec(0, 0))
    new_conv_p, new_hgrn_p, new_conv_s, new_hgrn_s = [], [], [], None
    for l in range(DEPTH):
        sh_f, sc_f, ga_f = mod_vec(l, 3), mod_vec(l, 4), mod_vec(l, 5)
        ga_a = mod_vec(l, 2)
        proj = _dense(h, (w_in,), l, tm=1664, tn=1024, out_dtype=F32)
        mix = jnp.zeros((N_ROWS, 2 * CONV_DIM), BF16)
        mix, nc_p = _conv_prompt(proj, conv_w[l], vec(conv_norm[l]), mix)
        mix, ns_p = _hgrn_prompt(proj, lb_all[l], hgrn_norm[l], mix)
        mix, nc_s = _conv_sample(proj, state_conv[l], conv_w[l], vec(conv_norm[l]), mix)
        mix, new_hgrn_s = _hgrn_sample(proj, state_hgrn, l, lb_all[l], hgrn_norm[l], mix, new_hgrn_s)
        mixed = _dense(mix, (w_out,), l, tm=1664, tn=1024, out_dtype=F32)
        new_conv_p.append(nc_p)
        new_hgrn_p.append(ns_p)
        new_conv_s.append(nc_s)
        nxt = (vec(norm_pre[l, 1]), sc_f, sh_f)
        if l % 2 == 0:
            j = l // 2
            x, h2 = _resid(x, mixed, vec(norm_post[l, 0]), ga_a, nxt=nxt)
            g = _dense(h2, (ffn_w1, ffn_w3), j, tm=1664, tn=512, out_dtype=BF16)
            f = _dense(g, (ffn_w2,), j, tm=1040, tn=512, out_dtype=F32)
        else:
            j = l // 2
            rw = jnp.pad(router_w[j].astype(F32), ((0, 0), (0, LANES - N_EXPERTS)))
            rb = jnp.pad(router_b[j].astype(F32), (0, LANES - N_EXPERTS)).reshape(1, LANES)
            x, h2, ri, gates, counts = _resid(x, mixed, vec(norm_post[l, 0]), ga_a, nxt=nxt, router=(rw, rb),
                                              h_dtype=F32)
            f = _moe_experts(h2, ri, gates, counts, moe_w1[j], moe_w3[j], moe_w2[j])
        if l + 1 < DEPTH:
            assert l % 2 == 0, "the expert combine is fused with the trunk's last residual step only"
            nxt = (vec(norm_pre[l + 1, 0]), mod_vec(l + 1, 1), mod_vec(l + 1, 0))
            x, h = _resid(x, f, vec(norm_post[l, 1]), ga_f, nxt=nxt)
        elif l % 2 == 0:
            y_prompt, y_sample = _resid(x, f, vec(norm_post[l, 1]), ga_f, y_split=True)
        else:
            y_prompt, y_sample = _combine_resid(x, *f, vec(norm_post[l, 1]), ga_f)

    y_prompt = y_prompt.reshape(BATCH, SEQ, D_MODEL)
    y_sample = y_sample.reshape(DEC_BATCH, 1, D_MODEL)
    return (y_prompt, y_sample, jnp.stack(new_conv_p), jnp.stack(new_hgrn_p),
            jnp.stack(new_conv_s), new_hgrn_s)
```

```python
import functools

import jax
import jax.numpy as jnp
import numpy as np
from jax import lax
from jax.experimental import pallas as pl
from jax.experimental.pallas import tpu as pltpu

F32 = jnp.float32
BF16 = jnp.bfloat16
I32 = jnp.int32
HIGHEST = lax.Precision.HIGHEST

LANES = 128
SUBLANES = 8
VMEM_LIMIT_BYTES = 56 * 1024 * 1024

D_MODEL = 2048
BATCH = 4
SEQ = 2048
DEPTH = 2
DEC_BATCH = 128
N_PROMPT = BATCH * SEQ
N_ROWS = N_PROMPT + DEC_BATCH
CONV_DIM = 1024
CONV_GROUPS = 16
CONV_W = 3
HG_HEADS = 8
HG_DK = 128
HG_DV = 128
PROJ_WIDTH = 7168
N_EXPERTS = 8
EPS = 1e-6
F_MIN = 1e-6

ROW_TILE = 128
N_TILES = N_ROWS // ROW_TILE
PROMPT_TILES = N_PROMPT // ROW_TILE
TILES_PER_SEQ = SEQ // ROW_TILE
MOE_BLOCK = 256
N_PAIRS = 2 * N_ROWS
N_BLOCKS = (N_PAIRS + N_EXPERTS * (MOE_BLOCK - 1) + MOE_BLOCK - 1) // MOE_BLOCK
P_ROWS = N_BLOCKS * MOE_BLOCK


def _cparams(sem):
    return pltpu.CompilerParams(dimension_semantics=sem, vmem_limit_bytes=VMEM_LIMIT_BYTES)


def _rms(x, g):
    return x * lax.rsqrt(jnp.mean(x * x, axis=-1, keepdims=True) + EPS) * g


def _silu(x):
    return x * jax.nn.sigmoid(x)


def _gmm_kernel(be_ref, nact_ref, a_ref, w_ref, *rest, lhs_silu, has_bias):
    if has_bias:
        b_ref, o_ref, wb_ref = rest
    else:
        o_ref, wb_ref = rest
    i = pl.program_id(1)
    prev = be_ref[jnp.maximum(i - 1, 0)]

    @pl.when((i == 0) | (be_ref[i] != prev))
    def _():
        wb_ref[...] = w_ref[0].astype(BF16)

    @pl.when(i < nact_ref[0])
    def _():
        a = a_ref[...]
        if lhs_silu:
            a = _silu(a)
        acc = jnp.dot(a.astype(BF16), wb_ref[...], preferred_element_type=F32)
        if has_bias:
            acc = acc + b_ref[0]
        o_ref[...] = acc.astype(o_ref.dtype)

    @pl.when(i >= nact_ref[0])
    def _():
        o_ref[...] = jnp.zeros_like(o_ref)


def _gmm(a, w, block_e, nact, *, tm, tn, out_dtype, bias=None, lhs_silu=False):
    m, k = a.shape
    _, _, n = w.shape
    grid = (n // tn, m // tm)
    in_specs = [
        pl.BlockSpec((tm, k), lambda j, i, be, na: (i, 0)),
        pl.BlockSpec((1, k, tn), lambda j, i, be, na: (be[i], 0, j)),
    ]
    args = [a, w]
    if bias is not None:
        in_specs.append(pl.BlockSpec((1, 1, tn), lambda j, i, be, na: (be[i], 0, j)))
        args.append(bias)
    return pl.pallas_call(
        functools.partial(_gmm_kernel, lhs_silu=lhs_silu, has_bias=bias is not None),
        grid_spec=pltpu.PrefetchScalarGridSpec(
            num_scalar_prefetch=2,
            grid=grid,
            in_specs=in_specs,
            out_specs=pl.BlockSpec((tm, tn), lambda j, i, be, na: (i, j)),
            scratch_shapes=[pltpu.VMEM((k, tn), BF16)],
        ),
        out_shape=jax.ShapeDtypeStruct((m, n), out_dtype),
        compiler_params=_cparams(("arbitrary", "arbitrary")),
    )(block_e, nact, *args)


CAST_ROWS = 32


def _emm_kernel(be_ref, nact_ref, nexte_ref, a_hbm, *rest, n_w, tm, nb):
    w_hbm = rest[:n_w]
    o_hbm, abuf, obuf, stage_ref, wb_ref, asem, osem, wsem = rest[n_w:]
    j = pl.program_id(0)
    last_tile = j + 1 == pl.num_programs(0)
    tn = obuf.shape[2]
    nact = nact_ref[0]
    base = (j * nact) % 2

    def a_copy(i, slot):
        rows = pl.ds(pl.multiple_of(i * tm, tm), tm)
        return pltpu.make_async_copy(a_hbm.at[rows, :], abuf.at[slot], asem.at[slot])

    def o_copy(i, slot):
        rows = pl.ds(pl.multiple_of(i * tm, tm), tm)
        cols = pl.ds(pl.multiple_of(j * tn, LANES), tn)
        return pltpu.make_async_copy(obuf.at[slot], o_hbm.at[rows, cols], osem.at[slot])

    def slab_copies(expert, col_tile):
        cols = pl.ds(pl.multiple_of(col_tile * tn, LANES), tn)
        return [pltpu.make_async_copy(w_hbm[t].at[expert, :, cols], stage_ref.at[t], wsem.at[t])
                for t in range(n_w)]

    @pl.when(j == 0)
    def _():
        for c in slab_copies(be_ref[0], j):
            c.start()
        a_copy(0, base).start()

    def block(i, carry):
        slot = (base + i) % 2
        e = be_ref[i]

        @pl.when(i + 1 < nact)
        def _():
            a_copy(i + 1, 1 - slot).start()

        @pl.when((i + 1 == nact) & jnp.logical_not(last_tile))
        def _():
            a_copy(0, 1 - slot).start()

        @pl.when((i == 0) | (e != be_ref[jnp.maximum(i - 1, 0)]))
        def _():
            for c in slab_copies(e, j):
                c.wait()

            def round_rows(c, carry):
                rows = pl.ds(pl.multiple_of(c * CAST_ROWS, CAST_ROWS), CAST_ROWS)
                for t in range(n_w):
                    wb_ref[t, rows, :] = stage_ref[t, rows, :].astype(BF16)
                return carry
            lax.fori_loop(0, stage_ref.shape[1] // CAST_ROWS, round_rows, 0)
            nxt = nexte_ref[i]

            @pl.when(nxt >= 0)
            def _():
                for c in slab_copies(nxt, j):
                    c.start()

            @pl.when((nxt < 0) & jnp.logical_not(last_tile))
            def _():
                for c in slab_copies(be_ref[0], j + 1):
                    c.start()

        a_copy(i, slot).wait()

        @pl.when(i >= 2)
        def _():
            o_copy(i - 2, slot).wait()

        a = abuf[slot]
        acc = jnp.dot(a, wb_ref[0], preferred_element_type=F32)
        if n_w == 2:
            acc = _silu(acc) * jnp.dot(a, wb_ref[1], preferred_element_type=F32)
        obuf[slot] = acc.astype(obuf.dtype)
        o_copy(i, slot).start()
        return carry

    lax.fori_loop(0, nact, block, 0)

    @pl.when(nact >= 2)
    def _():
        o_copy(nact - 2, (base + nact) % 2).wait()
    o_copy(nact - 1, (base + nact - 1) % 2).wait()

    @pl.when(nact < nb)
    def _():
        obuf[0] = jnp.zeros(obuf.shape[1:], obuf.dtype)

        def zero_block(i, carry):
            c = o_copy(i, 0)
            c.start()
            c.wait()
            return carry
        lax.fori_loop(nact, nb, zero_block, 0)


def _emm(a, ws, block_e, nact, next_e, *, tm, tn, out_dtype):
    m, k = a.shape
    n = ws[0].shape[2]
    n_w = len(ws)
    any_spec = pl.BlockSpec(memory_space=pl.ANY)
    return pl.pallas_call(
        functools.partial(_emm_kernel, n_w=n_w, tm=tm, nb=m // tm),
        grid_spec=pltpu.PrefetchScalarGridSpec(
            num_scalar_prefetch=3,
            grid=(n // tn,),
            in_specs=[any_spec] * (1 + n_w),
            out_specs=any_spec,
            scratch_shapes=[pltpu.VMEM((2, tm, k), BF16), pltpu.VMEM((2, tm, tn), out_dtype),
                            pltpu.VMEM((n_w, k, tn), F32), pltpu.VMEM((n_w, k, tn), BF16),
                            pltpu.SemaphoreType.DMA((2,)), pltpu.SemaphoreType.DMA((2,)),
                            pltpu.SemaphoreType.DMA((n_w,))],
        ),
        out_shape=jax.ShapeDtypeStruct((m, n), out_dtype),
        compiler_params=_cparams(("arbitrary",)),
    )(block_e, nact, next_e, a, *ws)


ELT_TILE = 512
ROUTER_TILE = 256


def _rep_rows(s, tile):
    return s if tile == DEC_BATCH else jnp.concatenate([s] * (tile // DEC_BATCH), axis=0)


def _pick(i, p_ref, s_ref, tile=ROW_TILE):
    return jnp.where(i < N_PROMPT // tile, p_ref[0], _rep_rows(s_ref[...], tile))


def _prenorm_kernel(xp_ref, xs_ref, g_ref, scp_ref, scs_ref, shp_ref, shs_ref, h_ref):
    i = pl.program_id(0)
    tile = h_ref.shape[0]
    x = jnp.where(i < N_PROMPT // tile, xp_ref[...], _rep_rows(xs_ref[...], tile))
    sc = _pick(i, scp_ref, scs_ref, tile)
    sh = _pick(i, shp_ref, shs_ref, tile)
    h_ref[...] = (_rms(x, g_ref[...]) * (1.0 + sc) + sh).astype(h_ref.dtype)


def _resid_kernel(*refs, x_split, y_split, with_next, with_router, h_dtype):
    it = iter(refs)
    if x_split:
        xp_ref, xs_ref = next(it), next(it)
    else:
        x_ref = next(it)
    f_ref, gpost_ref, gap_ref, gas_ref = [next(it) for _ in range(4)]
    if with_next:
        gpre_ref, scp_ref, scs_ref, shp_ref, shs_ref = [next(it) for _ in range(5)]
    if with_router:
        rw_ref, rb_ref = next(it), next(it)
    if y_split:
        yp_ref, ys_ref = next(it), next(it)
    else:
        xo_ref = next(it)
    if with_next:
        h_ref = next(it)
    if with_router:
        ri_ref, gate_ref, cnt_ref, carry_ref = [next(it) for _ in range(4)]
    i = pl.program_id(0)
    tile = f_ref.shape[0]
    prompt_tiles = N_PROMPT // tile
    ga = _pick(i, gap_ref, gas_ref, tile)
    if x_split:
        x = jnp.where(i < prompt_tiles, xp_ref[...], _rep_rows(xs_ref[...], tile))
    else:
        x = x_ref[...]
    x = x + ga * _rms(f_ref[...], gpost_ref[...])
    if y_split:
        @pl.when(i < prompt_tiles)
        def _():
            yp_ref[...] = x

        @pl.when(i == prompt_tiles)
        def _():
            ys_ref[...] = x[:DEC_BATCH]
    else:
        xo_ref[...] = x
    if with_next:
        sc = _pick(i, scp_ref, scs_ref, tile)
        sh = _pick(i, shp_ref, shs_ref, tile)
        h = _rms(x, gpre_ref[...]) * (1.0 + sc) + sh
        h_ref[...] = h.astype(h_dtype)
        if with_router:
            logits = jnp.dot(h, rw_ref[...], precision=HIGHEST, preferred_element_type=F32) + rb_ref[...]
            _route_tile(logits, ri_ref, gate_ref, cnt_ref, carry_ref)


def _n_tiles(tile):
    return N_PROMPT // tile + 1


def _row_spec(width, tile=ROW_TILE):
    return pl.BlockSpec((tile, width), lambda i, *_: (i, 0))


def _vec_spec():
    return pl.BlockSpec((1, D_MODEL), lambda i, *_: (0, 0))


def _modp_spec(tile=ROW_TILE):
    return pl.BlockSpec((1, 1, D_MODEL), lambda i, *_: (jnp.minimum(i // (SEQ // tile), BATCH - 1), 0, 0))


def _mods_spec():
    return pl.BlockSpec((DEC_BATCH, D_MODEL), lambda i, *_: (0, 0))


def _prompt_rows_spec(tile=ROW_TILE):
    return pl.BlockSpec((tile, D_MODEL), lambda i, *_: (jnp.minimum(i, N_PROMPT // tile - 1), 0))


def _prenorm(xp, xs, g, sc, sh):
    t = ELT_TILE
    return pl.pallas_call(
        _prenorm_kernel,
        grid=(_n_tiles(t),),
        in_specs=[_prompt_rows_spec(t), _mods_spec(), _vec_spec(), _modp_spec(t), _mods_spec(), _modp_spec(t),
                  _mods_spec()],
        out_specs=_row_spec(D_MODEL, t),
        out_shape=jax.ShapeDtypeStruct((N_ROWS, D_MODEL), BF16),
        compiler_params=_cparams(("arbitrary",)),
    )(xp, xs, g, sc[0], sc[1], sh[0], sh[1])


def _resid(x, f, gpost, ga, nxt=None, router=None, h_dtype=BF16, y_split=False):
    t = ELT_TILE if router is None else ROUTER_TILE
    x_split = isinstance(x, tuple)
    if x_split:
        args = [x[0], x[1]]
        in_specs = [_prompt_rows_spec(t), _mods_spec()]
    else:
        args = [x]
        in_specs = [_row_spec(D_MODEL, t)]
    args += [f, gpost, ga[0], ga[1]]
    in_specs += [_row_spec(D_MODEL, t), _vec_spec(), _modp_spec(t), _mods_spec()]
    if y_split:
        out_shape = [jax.ShapeDtypeStruct((N_PROMPT, D_MODEL), F32), jax.ShapeDtypeStruct((DEC_BATCH, D_MODEL), F32)]
        out_specs = [_prompt_rows_spec(t), _mods_spec()]
    else:
        out_shape = [jax.ShapeDtypeStruct((N_ROWS, D_MODEL), F32)]
        out_specs = [_row_spec(D_MODEL, t)]
    if nxt is not None:
        gpre, sc, sh = nxt
        args += [gpre, sc[0], sc[1], sh[0], sh[1]]
        in_specs += [_vec_spec(), _modp_spec(t), _mods_spec(), _modp_spec(t), _mods_spec()]
        out_shape.append(jax.ShapeDtypeStruct((N_ROWS, D_MODEL), h_dtype))
        out_specs.append(_row_spec(D_MODEL, t))
    if router is not None:
        rw, rb = router
        args += [rw, rb]
        in_specs += [pl.BlockSpec((D_MODEL, LANES), lambda i: (0, 0)), pl.BlockSpec((1, LANES), lambda i: (0, 0))]
        out_shape += [jax.ShapeDtypeStruct((N_ROWS, LANES), I32), jax.ShapeDtypeStruct((N_ROWS, LANES), F32),
                      jax.ShapeDtypeStruct((1, LANES), F32)]
        out_specs += [_row_spec(LANES, t), _row_spec(LANES, t), pl.BlockSpec((1, LANES), lambda i: (0, 0))]
        scratch = [pltpu.VMEM((1, LANES), F32)]
    else:
        scratch = []
    return pl.pallas_call(
        functools.partial(_resid_kernel, x_split=x_split, y_split=y_split, with_next=nxt is not None,
                          with_router=router is not None, h_dtype=h_dtype),
        grid=(_n_tiles(t),),
        in_specs=in_specs,
        out_specs=out_specs,
        out_shape=out_shape,
        scratch_shapes=scratch,
        compiler_params=_cparams(("arbitrary",)),
    )(*args)


def _group_tables():
    grp = np.arange(CONV_DIM)[:, None] // (CONV_DIM // CONV_GROUPS) == np.arange(LANES)[None, :]
    return jnp.asarray(grp, BF16), jnp.asarray(grp.T, BF16)


def _split2(x):
    hi = x.astype(BF16)
    return hi, (x - hi.astype(F32)).astype(BF16)


def _group_rms(y, gnorm, sel, sel_t):
    width = CONV_DIM // CONV_GROUPS
    hi, lo = _split2(y * y)
    ss = jnp.dot(hi, sel, preferred_element_type=F32) + jnp.dot(lo, sel, preferred_element_type=F32)
    hi, lo = _split2(lax.rsqrt(ss * (1.0 / width) + EPS))
    scale = jnp.dot(hi, sel_t, preferred_element_type=F32) + jnp.dot(lo, sel_t, preferred_element_type=F32)
    return y * scale * gnorm


def _conv_prompt_kernel(hc_ref, bg_ref, cg_ref, hch_ref, cgh_ref, cw_ref, gn_ref, sel_ref, selt_ref, mix_ref,
                        y_ref, nc_ref, *, tt):
    del mix_ref
    t = pl.program_id(1)
    u = cg_ref[...] * hc_ref[...]
    halo = jnp.where(t == 0, 0.0, cgh_ref[...] * hch_ref[...])
    h1 = halo[SUBLANES - 1:SUBLANES]
    h2 = halo[SUBLANES - 2:SUBLANES - 1]
    row = lax.broadcasted_iota(I32, u.shape, 0)
    u1 = jnp.where(row == 0, h1, pltpu.roll(u, 1, 0))
    u2 = jnp.where(row == 0, h2, jnp.where(row == 1, h1, pltpu.roll(u, 2, 0)))
    conv = cw_ref[0:1] * u2 + cw_ref[1:2] * u1 + cw_ref[2:3] * u
    y_ref[...] = _group_rms(bg_ref[...] * conv, gn_ref[...], sel_ref[...], selt_ref[...]).astype(y_ref.dtype)

    @pl.when(t == pl.num_programs(1) - 1)
    def _():
        nc_ref[0] = u[tt - (CONV_W - 1):]


def _conv_prompt(proj, conv_w, conv_norm, mix, *, tt=512):
    nt = SEQ // tt
    cblk = lambda c: pl.BlockSpec((tt, CONV_DIM), lambda b, t: (b * nt + t, c))
    hblk = lambda c: pl.BlockSpec(
        (SUBLANES, CONV_DIM), lambda b, t: (jnp.maximum((b * nt + t) * (tt // SUBLANES) - 1, 0), c))
    return pl.pallas_call(
        functools.partial(_conv_prompt_kernel, tt=tt),
        grid=(BATCH, nt),
        in_specs=[cblk(0), cblk(1), cblk(2), hblk(0), hblk(2),
                  pl.BlockSpec((CONV_W, CONV_DIM), lambda b, t: (0, 0)),
                  pl.BlockSpec((1, CONV_DIM), lambda b, t: (0, 0)),
                  pl.BlockSpec((CONV_DIM, LANES), lambda b, t: (0, 0)),
                  pl.BlockSpec((LANES, CONV_DIM), lambda b, t: (0, 0)),
                  pl.BlockSpec(memory_space=pl.ANY)],
        out_specs=[pl.BlockSpec((tt, CONV_DIM), lambda b, t: (b * nt + t, 0)),
                   pl.BlockSpec((1, CONV_W - 1, CONV_DIM), lambda b, t: (b, 0, 0))],
        out_shape=[jax.ShapeDtypeStruct(mix.shape, mix.dtype),
                   jax.ShapeDtypeStruct((BATCH, CONV_W - 1, CONV_DIM), F32)],
        input_output_aliases={9: 0},
        compiler_params=_cparams(("arbitrary", "arbitrary")),
    )(proj, proj, proj, proj, proj, conv_w, conv_norm, *_group_tables(), mix)


HG_LEVELS = tuple(1 << i for i in range(ROW_TILE.bit_length() - 1))
LOG2_E = 1.4426950408889634


def _gates(z, lb):
    sg = jax.nn.sigmoid(z)
    f = lb + (1.0 - lb) * sg
    return jnp.log(jnp.maximum(f, F_MIN)), (1.0 - lb) * (1.0 - sg)


def _level_table():
    t = np.arange(ROW_TILE)[:, None]
    s = np.arange(ROW_TILE)[None, :]
    x = np.maximum(t ^ s, 1)
    lvl = np.floor(np.log2(x)).astype(np.int32)
    lvl = np.where(t == s, len(HG_LEVELS), np.where(s < t, lvl, -1))
    return jnp.asarray(lvl, I32)


def _nt_dot(x, y):
    return lax.dot_general(x, y, (((1,), (1,)), ((), ())), preferred_element_type=F32)


def _hgrn_tile(q, z, v, lb, st, tri, lvl):
    logf, k = _gates(z, lb)
    lf2 = logf * LOG2_E
    hi = lf2.astype(BF16)
    rem = lf2 - hi.astype(F32)
    mid = rem.astype(BF16)
    lo = (rem - mid.astype(F32)).astype(BF16)
    parts = jnp.dot(tri, jnp.concatenate([hi, mid, lo], axis=1), preferred_element_type=F32)
    a = parts[:, :HG_DK] + parts[:, HG_DK:2 * HG_DK] + parts[:, 2 * HG_DK:]
    row = lax.broadcasted_iota(I32, (ROW_TILE, HG_DK), 0)
    a8 = a.reshape(ROW_TILE // SUBLANES, SUBLANES, HG_DK)
    sub8 = lax.broadcasted_iota(I32, a8.shape, 1)
    scores = jnp.where(lvl == len(HG_LEVELS), _nt_dot(q.astype(BF16), k.astype(BF16)), 0.0)
    for li, c in enumerate(HG_LEVELS):
        if c < SUBLANES:
            if c == 1:
                d = jnp.where((row & 1) == 1, lf2, 0.0)
            elif c == 2:
                anchor = jnp.where(sub8 < 4, a8[:, 1:2, :], a8[:, 5:6, :])
                d = (a8 - anchor).reshape(ROW_TILE, HG_DK)
            else:
                d = (a8 - a8[:, c - 1:c, :]).reshape(ROW_TILE, HG_DK)
            src = jnp.where((row & c) != 0, q, k)
            neg = -jnp.abs(d)
        else:
            shape3 = (ROW_TILE // (2 * c), 2 * c, HG_DK)
            ab, qb, kb = a.reshape(shape3), q.reshape(shape3), k.reshape(shape3)
            anchor = ab[:, c - 1:c, :]
            neg = jnp.concatenate([anchor - ab[:, :c, :], ab[:, c:, :] - anchor], axis=1).reshape(ROW_TILE, HG_DK)
            src = jnp.concatenate([kb[:, :c, :], qb[:, c:, :]], axis=1).reshape(ROW_TILE, HG_DK)
        x = (src * jnp.exp2(neg)).astype(BF16)
        scores = jnp.where(lvl == li, _nt_dot(x, x), scores)
    vb = v.astype(BF16)
    a_last = a[ROW_TILE - 1:ROW_TILE, :]
    o = jnp.dot(scores.astype(BF16), vb, preferred_element_type=F32)
    o = o + _nt_dot((q * jnp.exp2(a)).astype(BF16), st.astype(BF16))
    kt = (k * jnp.exp2(a_last - a)).astype(BF16)
    st = st * jnp.exp2(a_last) + lax.dot_general(vb, kt, (((0,), (0,)), ((), ())),
                                                 preferred_element_type=F32)
    return o, st


def _hgrn_prompt_kernel(q_ref, z_ref, v_ref, og_ref, lb_ref, gn_ref, tri_ref, lvl_ref, mix_ref, o_ref, s_ref,
                        st_ref, *, tt):
    del mix_ref
    t = pl.program_id(1)

    @pl.when(t == 0)
    def _():
        st_ref[...] = jnp.zeros_like(st_ref)

    def body(j, carry):
        rows = pl.ds(pl.multiple_of(j * ROW_TILE, ROW_TILE), ROW_TILE)
        for h in range(HG_HEADS):
            cols = slice(h * LANES, (h + 1) * LANES)
            o, st = _hgrn_tile(q_ref[rows, cols], z_ref[rows, cols], v_ref[rows, cols], lb_ref[:, cols],
                               st_ref[h], tri_ref[...], lvl_ref[...])
            st_ref[h] = st
            o_ref[rows, cols] = (_rms(o, gn_ref[:, cols]) * _silu(og_ref[rows, cols])).astype(o_ref.dtype)
        return carry

    lax.fori_loop(0, tt // ROW_TILE, body, 0)

    @pl.when(t == pl.num_programs(1) - 1)
    def _():
        for h in range(HG_HEADS):
            s_ref[0, h] = st_ref[h].T


def _hgrn_prompt(proj, lb, hgrn_norm, mix, *, tt=512):
    nt = SEQ // tt
    width = HG_HEADS * LANES
    col0 = 3 * CONV_DIM // width
    blk = lambda part: pl.BlockSpec((tt, width), lambda b, t: (b * nt + t, col0 + part))
    hvec = pl.BlockSpec((1, width), lambda b, t: (0, 0))
    const = pl.BlockSpec((ROW_TILE, ROW_TILE), lambda b, t: (0, 0))
    tri = jnp.asarray(np.tril(np.ones((ROW_TILE, ROW_TILE), np.float32)), BF16)
    return pl.pallas_call(
        functools.partial(_hgrn_prompt_kernel, tt=tt),
        grid=(BATCH, nt),
        in_specs=[blk(0), blk(1), blk(2), blk(3), hvec, hvec, const, const, pl.BlockSpec(memory_space=pl.ANY)],
        out_specs=[pl.BlockSpec((tt, width), lambda b, t: (b * nt + t, 1)),
                   pl.BlockSpec((1, HG_HEADS, HG_DK, HG_DV), lambda b, t: (b, 0, 0, 0))],
        out_shape=[jax.ShapeDtypeStruct(mix.shape, mix.dtype),
                   jax.ShapeDtypeStruct((BATCH, HG_HEADS, HG_DK, HG_DV), F32)],
        scratch_shapes=[pltpu.VMEM((HG_HEADS, HG_DV, HG_DK), F32)],
        input_output_aliases={8: 0},
        compiler_params=_cparams(("arbitrary", "arbitrary")),
    )(proj, proj, proj, proj, lb.reshape(1, width), hgrn_norm.reshape(1, width), tri, _level_table(), mix)


def _conv_sample_kernel(hc_ref, bg_ref, cg_ref, cs_ref, cw_ref, gn_ref, sel_ref, selt_ref, mix_ref, y_ref, nc_ref):
    del mix_ref
    u = cg_ref[...] * hc_ref[...]
    s0 = cs_ref[:, 0, :]
    s1 = cs_ref[:, 1, :]
    conv = cw_ref[0:1] * s0 + cw_ref[1:2] * s1 + cw_ref[2:3] * u
    y_ref[...] = _group_rms(bg_ref[...] * conv, gn_ref[...], sel_ref[...], selt_ref[...]).astype(y_ref.dtype)
    nc_ref[:, 0, :] = s1
    nc_ref[:, 1, :] = u


def _conv_sample(proj, conv_state, conv_w, conv_norm, mix):
    rb = PROMPT_TILES
    cblk = lambda c: pl.BlockSpec((DEC_BATCH, CONV_DIM), lambda i: (rb, c))
    return pl.pallas_call(
        _conv_sample_kernel,
        grid=(1,),
        in_specs=[cblk(0), cblk(1), cblk(2),
                  pl.BlockSpec((DEC_BATCH, CONV_W - 1, CONV_DIM), lambda i: (0, 0, 0)),
                  pl.BlockSpec((CONV_W, CONV_DIM), lambda i: (0, 0)),
                  pl.BlockSpec((1, CONV_DIM), lambda i: (0, 0)),
                  pl.BlockSpec((CONV_DIM, LANES), lambda i: (0, 0)),
                  pl.BlockSpec((LANES, CONV_DIM), lambda i: (0, 0)),
                  pl.BlockSpec(memory_space=pl.ANY)],
        out_specs=[pl.BlockSpec((DEC_BATCH, CONV_DIM), lambda i: (rb, 0)),
                   pl.BlockSpec((DEC_BATCH, CONV_W - 1, CONV_DIM), lambda i: (0, 0, 0))],
        out_shape=[jax.ShapeDtypeStruct(mix.shape, mix.dtype),
                   jax.ShapeDtypeStruct((DEC_BATCH, CONV_W - 1, CONV_DIM), F32)],
        input_output_aliases={8: 0},
        compiler_params=_cparams(("arbitrary",)),
    )(proj, proj, proj, conv_state, conv_w, conv_norm, *_group_tables(), mix)


def _hgrn_sample_kernel(q_ref, z_ref, v_ref, og_ref, lb_ref, gn_ref, s_ref, *rest, bg, slab):
    o_ref, so_ref, osc_ref = rest[-3:]
    s_ref = s_ref.at[0]
    for other in range(so_ref.shape[0]):
        if other != slab:
            so_ref[other] = jnp.zeros(so_ref.shape[1:], so_ref.dtype)
    so_ref = so_ref.at[slab]
    g = pl.program_id(1)
    lb = lb_ref[0]
    logf, k = _gates(z_ref[...], lb)
    f = jnp.exp(logf)
    shift = (DEC_BATCH - g * bg) % DEC_BATCH
    ft = pltpu.roll(f.T, shift, 1)
    kt = pltpu.roll(k.T, shift, 1)
    rows = pl.ds(pl.multiple_of(g * bg, bg), bg)
    v = v_ref[rows, :]
    q = q_ref[rows, :]
    for j in range(bg):
        s_new = ft[:, j:j + 1] * s_ref[j, 0] + kt[:, j:j + 1] * v[j:j + 1, :]
        so_ref[j, 0] = s_new
        osc_ref[j:j + 1, :] = jnp.dot(q[j:j + 1, :].astype(BF16), s_new.astype(BF16),
                                      preferred_element_type=F32)
    o = osc_ref[...]
    o_ref[...] = (_rms(o, gn_ref[0]) * _silu(og_ref[rows, :])).astype(o_ref.dtype)


def _hgrn_sample(proj, state_all, l, lb, hgrn_norm, mix, new_state_all=None, *, bg=32):
    rb = PROMPT_TILES
    col0 = 3 * CONV_DIM // LANES
    blk = lambda part: pl.BlockSpec((DEC_BATCH, LANES), lambda h, g: (rb, col0 + part * HG_HEADS + h))
    hvec = pl.BlockSpec((1, 1, LANES), lambda h, g: (h, 0, 0))
    sblk = pl.BlockSpec((1, bg, 1, HG_DK, HG_DV), lambda h, g: (l, g, h, 0, 0))
    any_spec = pl.BlockSpec(memory_space=pl.ANY)
    args = [proj, proj, proj, proj, lb.reshape(HG_HEADS, 1, HG_DK), hgrn_norm.reshape(HG_HEADS, 1, HG_DV),
            state_all, mix]
    in_specs = [blk(0), blk(1), blk(2), blk(3), hvec, hvec, sblk, any_spec]
    aliases = {7: 0}
    if new_state_all is not None:
        args.append(new_state_all)
        in_specs.append(any_spec)
        aliases[8] = 1
        so_blk, slab = sblk, 0
    else:
        n_slabs = state_all.shape[0]
        so_blk = pl.BlockSpec((n_slabs, bg, 1, HG_DK, HG_DV), lambda h, g: (0, g, h, 0, 0))
        slab = l
    return pl.pallas_call(
        functools.partial(_hgrn_sample_kernel, bg=bg, slab=slab),
        grid=(HG_HEADS, DEC_BATCH // bg),
        in_specs=in_specs,
        out_specs=[pl.BlockSpec((bg, LANES), lambda h, g: (N_PROMPT // bg + g, CONV_DIM // LANES + h)), so_blk],
        out_shape=[jax.ShapeDtypeStruct(mix.shape, mix.dtype),
                   jax.ShapeDtypeStruct(state_all.shape, F32)],
        scratch_shapes=[pltpu.VMEM((bg, HG_DV), F32)],
        input_output_aliases=aliases,
        compiler_params=_cparams(("arbitrary", "arbitrary")),
    )(*args)


def _route_tile(logits, ri_ref, gate_ref, cnt_ref, carry_ref):
    i = pl.program_id(0)

    @pl.when(i == 0)
    def _():
        carry_ref[...] = jnp.zeros_like(carry_ref)

    tile = logits.shape[0]
    lane = lax.broadcasted_iota(I32, (tile, LANES), 1)
    lanef = lane.astype(F32)
    valid = i * tile + lax.broadcasted_iota(I32, (tile, LANES), 0) < N_ROWS
    lg = jnp.where(valid & (lane < N_EXPERTS), logits, jnp.where(valid, -jnp.inf, 0.0))
    m1 = jnp.max(lg, axis=-1, keepdims=True)
    i1 = jnp.min(jnp.where(lg == m1, lanef, float(LANES)), axis=-1, keepdims=True).astype(I32)
    lg2 = jnp.where(lane == i1, -jnp.inf, lg)
    m2 = jnp.max(lg2, axis=-1, keepdims=True)
    i2 = jnp.min(jnp.where(lg2 == m2, lanef, float(LANES)), axis=-1, keepdims=True).astype(I32)
    e = jnp.exp(m2 - m1)
    g1 = 1.0 / (1.0 + e)
    g2 = e / (1.0 + e)
    hot1 = lane == i1
    hot2 = lane == i2
    hot = jnp.where(valid, (hot1 | hot2).astype(F32), 0.0).astype(BF16)
    r = lax.broadcasted_iota(I32, (tile, tile), 0)
    c = lax.broadcasted_iota(I32, (tile, tile), 1)
    before = (c < r).astype(BF16)
    tot = jnp.dot(before, hot, preferred_element_type=F32) + carry_ref[...]
    r1 = jnp.sum(jnp.where(hot1, tot, 0.0), axis=-1, keepdims=True).astype(I32)
    r2 = jnp.sum(jnp.where(hot2, tot, 0.0), axis=-1, keepdims=True).astype(I32)
    ri_ref[...] = jnp.where(lane == 0, i1, jnp.where(lane == 1, i2, jnp.where(lane == 2, r1,
                            jnp.where(lane == 3, r2, 0))))
    gate_ref[...] = jnp.where(lane == 0, g1, jnp.where(lane == 1, g2, 0.0))
    carry_ref[...] += jnp.sum(hot.astype(F32), axis=0, keepdims=True)
    cnt_ref[...] = carry_ref[...]


DMA_GROUP = 8


def _gather_kernel(d1_ref, d2_ref, nact_ref, h_ref, xb_ref, tok_ref, buf_ref, sem):
    i = pl.program_id(0)

    nact = nact_ref[0]

    @pl.when(i == 0)
    def _():
        def clear(p, carry):
            tok_ref[p] = 0
            return carry
        lax.fori_loop(0, P_ROWS, clear, 0, unroll=8)

        def scatter(t, carry):
            tok_ref[d1_ref[t]] = t
            tok_ref[d2_ref[t]] = t
            return carry
        lax.fori_loop(0, N_ROWS, scatter, 0, unroll=4)

    def start_block(b):
        slot = b % 2

        def start(g, carry):
            for k in range(DMA_GROUP):
                r = g * DMA_GROUP + k
                tok = tok_ref[b * MOE_BLOCK + r]
                pltpu.make_async_copy(h_ref.at[pl.ds(tok, 1), :], buf_ref.at[slot, pl.ds(r, 1), :],
                                      sem.at[slot]).start(priority=k % 2)
            return carry
        lax.fori_loop(0, MOE_BLOCK // DMA_GROUP, start, 0)

    @pl.when(i == 0)
    def _():
        start_block(i)

    @pl.when(i + 1 < nact)
    def _():
        start_block(i + 1)

    @pl.when(i < nact)
    def _():
        slot = i % 2
        pltpu.make_async_copy(h_ref.at[pl.ds(0, MOE_BLOCK), :], buf_ref.at[slot], sem.at[slot]).wait()
        xb_ref[...] = buf_ref[slot].astype(xb_ref.dtype)

    @pl.when(i >= nact)
    def _():
        xb_ref[...] = jnp.zeros_like(xb_ref)


def _gather(h, dest1, dest2, nact):
    return pl.pallas_call(
        _gather_kernel,
        grid_spec=pltpu.PrefetchScalarGridSpec(
            num_scalar_prefetch=3,
            grid=(N_BLOCKS,),
            in_specs=[pl.BlockSpec(memory_space=pl.ANY)],
            out_specs=pl.BlockSpec((MOE_BLOCK, D_MODEL), lambda i, d1, d2, na: (i, 0)),
            scratch_shapes=[pltpu.SMEM((P_ROWS,), I32), pltpu.VMEM((2, MOE_BLOCK, D_MODEL), F32),
                            pltpu.SemaphoreType.DMA((2,))],
        ),
        out_shape=jax.ShapeDtypeStruct((P_ROWS, D_MODEL), BF16),
        compiler_params=_cparams(("arbitrary",)),
    )(dest1, dest2, nact, h)


def _combine_resid_kernel(d1_ref, d2_ref, x_ref, gate_ref, gpost_ref, gap_ref, gas_ref, yb_ref, yp_ref, ys_ref,
                          b1_ref, b2_ref, sem):
    i = pl.program_id(0)

    def start_tile(t):
        slot = t % 2

        def start(g, carry):
            for k in range(DMA_GROUP):
                r = g * DMA_GROUP + k
                tok = t * ROW_TILE + r
                pltpu.make_async_copy(yb_ref.at[pl.ds(d1_ref[tok], 1), :], b1_ref.at[slot, pl.ds(r, 1), :],
                                      sem.at[0, slot]).start(priority=0)
                pltpu.make_async_copy(yb_ref.at[pl.ds(d2_ref[tok], 1), :], b2_ref.at[slot, pl.ds(r, 1), :],
                                      sem.at[1, slot]).start(priority=1)
            return carry
        lax.fori_loop(0, ROW_TILE // DMA_GROUP, start, 0)

    @pl.when(i == 0)
    def _():
        start_tile(i)

    @pl.when(i + 1 < pl.num_programs(0))
    def _():
        start_tile(i + 1)

    slot = i % 2
    pltpu.make_async_copy(yb_ref.at[pl.ds(0, ROW_TILE), :], b1_ref.at[slot], sem.at[0, slot]).wait()
    pltpu.make_async_copy(yb_ref.at[pl.ds(0, ROW_TILE), :], b2_ref.at[slot], sem.at[1, slot]).wait()
    gate = gate_ref[...]
    f = gate[:, 0:1] * b1_ref[slot] + gate[:, 1:2] * b2_ref[slot]
    x = x_ref[...] + _pick(i, gap_ref, gas_ref) * _rms(f, gpost_ref[...])

    @pl.when(i < PROMPT_TILES)
    def _():
        yp_ref[...] = x

    @pl.when(i == PROMPT_TILES)
    def _():
        ys_ref[...] = x


def _combine_resid(x, yb, gates, dest1, dest2, gpost, ga):
    return pl.pallas_call(
        _combine_resid_kernel,
        grid_spec=pltpu.PrefetchScalarGridSpec(
            num_scalar_prefetch=2,
            grid=(N_TILES,),
            in_specs=[_row_spec(D_MODEL), _row_spec(LANES), _vec_spec(), _modp_spec(), _mods_spec(),
                      pl.BlockSpec(memory_space=pl.ANY)],
            out_specs=[_prompt_rows_spec(), _mods_spec()],
            scratch_shapes=[pltpu.VMEM((2, ROW_TILE, D_MODEL), F32), pltpu.VMEM((2, ROW_TILE, D_MODEL), F32),
                            pltpu.SemaphoreType.DMA((2, 2))],
        ),
        out_shape=[jax.ShapeDtypeStruct((N_PROMPT, D_MODEL), F32), jax.ShapeDtypeStruct((DEC_BATCH, D_MODEL), F32)],
        compiler_params=_cparams(("arbitrary",)),
    )(dest1, dest2, x, gates, gpost, ga[0], ga[1], yb)


def _moe_experts(h, ri, gates, counts, w1, w3, w2):
    counts = counts[0, :N_EXPERTS].astype(I32)
    padded = (counts + MOE_BLOCK - 1) // MOE_BLOCK * MOE_BLOCK
    pends = jnp.cumsum(padded)
    pstarts = pends - padded
    dest1 = pstarts[ri[:, 0]] + ri[:, 2]
    dest2 = pstarts[ri[:, 1]] + ri[:, 3]
    nact = (pends[-1:] // MOE_BLOCK).astype(I32)
    block_start = jnp.arange(N_BLOCKS, dtype=I32) * MOE_BLOCK
    block_e = jnp.minimum(jnp.sum((pends[None, :] <= block_start[:, None]).astype(I32), axis=1), N_EXPERTS - 1)
    run_end = (pends // MOE_BLOCK)[block_e]
    next_e = jnp.where(run_end < nact[0], block_e[jnp.minimum(run_end, N_BLOCKS - 1)], -1).astype(I32)
    xb = _gather(h, dest1, dest2, nact)
    gb = _emm(xb, (w1, w3), block_e, nact, next_e, tm=MOE_BLOCK, tn=1792, out_dtype=BF16)
    yb = _emm(gb, (w2,), block_e, nact, next_e, tm=MOE_BLOCK, tn=512, out_dtype=F32)
    return yb, gates, dest1, dest2


def _dense(a, ws, e, *, tm, tn, out_dtype):
    nb = a.shape[0] // tm
    return _emm(a, ws, jnp.full((nb,), e, I32), jnp.full((1,), nb, I32), jnp.full((nb,), -1, I32),
                tm=tm, tn=tn, out_dtype=out_dtype)


def kernel(x_prompt, x_sample, state_conv, state_hgrn, c_prompt, c_sample, norm_pre, norm_post, w_mod, b_mod, w_in, conv_w, conv_norm, lb_logits, hgrn_norm, w_out, ffn_w1, ffn_w3, ffn_w2, router_w, router_b, moe_w1, moe_w3, moe_w2):
    p = jax.nn.softmax(lb_logits.astype(F32), axis=0)
    lb_all = jnp.cumsum(p, axis=0) - p[0:1]

    n_cond = BATCH + DEC_BATCH
    cond_rows = (n_cond + SUBLANES - 1) // SUBLANES * SUBLANES
    c_all = jnp.concatenate([c_prompt, c_sample, jnp.zeros((cond_rows - n_cond, D_MODEL), F32)], axis=0)
    mod = _gmm(jnp.concatenate([c_all] * DEPTH, axis=0), w_mod, jnp.arange(DEPTH, dtype=I32),
               jnp.full((1,), DEPTH, I32), tm=cond_rows, tn=1024, out_dtype=F32,
               bias=b_mod.reshape(DEPTH, 1, 6 * D_MODEL), lhs_silu=True)

    def mod_vec(l, j):
        rows = mod[l * cond_rows:l * cond_rows + n_cond, j * D_MODEL:(j + 1) * D_MODEL]
        return rows[:BATCH].reshape(BATCH, 1, D_MODEL), rows[BATCH:]

    x = (x_prompt.reshape(N_PROMPT, D_MODEL), x_sample.reshape(DEC_BATCH, D_MODEL))
    vec = lambda a: a.reshape(1, -1)

    h = _prenorm(x[0], x[1], vec(norm_pre[0, 0]), mod_vec(0, 1), mod_vec(0, 0))
    new_conv_p, new_hgrn_p, new_conv_s, new_hgrn_s = [], [], [], None
    for l in range(DEPTH):
        sh_f, sc_f, ga_f = mod_vec(l, 3), mod_vec(l, 4), mod_vec(l, 5)
        ga_a = mod_vec(l, 2)
        proj = _dense(h, (w_in,), l, tm=1664, tn=1024, out_dtype=F32)
        mix = jnp.zeros((N_ROWS, 2 * CONV_DIM), BF16)
        mix, nc_p = _conv_prompt(proj, conv_w[l], vec(conv_norm[l]), mix)
        mix, ns_p = _hgrn_prompt(proj, lb_all[l], hgrn_norm[l], mix)
        mix, nc_s = _conv_sample(proj, state_conv[l], conv_w[l], vec(conv_norm[l]), mix)
        mix, new_hgrn_s = _hgrn_sample(proj, state_hgrn, l, lb_all[l], hgrn_norm[l], mix, new_hgrn_s)
        mixed = _dense(mix, (w_out,), l, tm=1664, tn=1024, out_dtype=F32)
        new_conv_p.append(nc_p)
        new_hgrn_p.append(ns_p)
        new_conv_s.append(nc_s)
        nxt = (vec(norm_pre[l, 1]), sc_f, sh_f)
        if l % 2 == 0:
            j = l // 2
            x, h2 = _resid(x, mixed, vec(norm_post[l, 0]), ga_a, nxt=nxt)
            g = _dense(h2, (ffn_w1, ffn_w3), j, tm=1664, tn=512, out_dtype=BF16)
            f = _dense(g, (ffn_w2,), j, tm=1040, tn=512, out_dtype=F32)
        else:
            j = l // 2
            rw = jnp.pad(router_w[j].astype(F32), ((0, 0), (0, LANES - N_EXPERTS)))
            rb = jnp.pad(router_b[j].astype(F32), (0, LANES - N_EXPERTS)).reshape(1, LANES)
            x, h2, ri, gates, counts = _resid(x, mixed, vec(norm_post[l, 0]), ga_a, nxt=nxt, router=(rw, rb),
                                              h_dtype=F32)
            f = _moe_experts(h2, ri, gates, counts, moe_w1[j], moe_w3[j], moe_w2[j])
        if l + 1 < DEPTH:
            assert l % 2 == 0, "the expert combine is fused with the trunk's last residual step only"
            nxt = (vec(norm_pre[l + 1, 0]), mod_vec(l + 1, 1), mod_vec(l + 1, 0))
            x, h = _resid(x, f, vec(norm_post[l, 1]), ga_f, nxt=nxt)
        elif l % 2 == 0:
            y_prompt, y_sample = _resid(x, f, vec(norm_post[l, 1]), ga_f, y_split=True)
        else:
            y_prompt, y_sample = _combine_resid(x, *f, vec(norm_post[l, 1]), ga_f)

    y_prompt = y_prompt.reshape(BATCH, SEQ, D_MODEL)
    y_sample = y_sample.reshape(DEC_BATCH, 1, D_MODEL)
    return (y_prompt, y_sample, jnp.stack(new_conv_p), jnp.stack(new_hgrn_p),
            jnp.stack(new_conv_s), new_hgrn_s)
```

```python
import functools

import jax
import jax.numpy as jnp
import numpy as np
from jax import lax
from jax.experimental import pallas as pl
from jax.experimental.pallas import tpu as pltpu

F32 = jnp.float32
BF16 = jnp.bfloat16
I32 = jnp.int32
HIGHEST = lax.Precision.HIGHEST

LANES = 128
SUBLANES = 8
VMEM_LIMIT_BYTES = 56 * 1024 * 1024

D_MODEL = 2048
BATCH = 4
SEQ = 2048
DEPTH = 2
DEC_BATCH = 128
N_PROMPT = BATCH * SEQ
N_ROWS = N_PROMPT + DEC_BATCH
CONV_DIM = 1024
CONV_GROUPS = 16
CONV_W = 3
HG_HEADS = 8
HG_DK = 128
HG_DV = 128
N_EXPERTS = 8
EPS = 1e-6
F_MIN = 1e-6

ROW_TILE = 128
N_TILES = N_ROWS // ROW_TILE
PROMPT_TILES = N_PROMPT // ROW_TILE
MOE_BLOCK = 256
N_PAIRS = 2 * N_ROWS
N_BLOCKS = (N_PAIRS + N_EXPERTS * (MOE_BLOCK - 1) + MOE_BLOCK - 1) // MOE_BLOCK
P_ROWS = N_BLOCKS * MOE_BLOCK


def _cparams(sem):
    return pltpu.CompilerParams(dimension_semantics=sem, vmem_limit_bytes=VMEM_LIMIT_BYTES)


def _rms(x, g):
    return x * lax.rsqrt(jnp.mean(x * x, axis=-1, keepdims=True) + EPS) * g


def _silu(x):
    return x * jax.nn.sigmoid(x)


def _gmm_kernel(be_ref, nact_ref, a_ref, w_ref, *rest, lhs_silu, has_bias):
    if has_bias:
        b_ref, o_ref, wb_ref = rest
    else:
        o_ref, wb_ref = rest
    i = pl.program_id(1)
    prev = be_ref[jnp.maximum(i - 1, 0)]

    @pl.when((i == 0) | (be_ref[i] != prev))
    def _():
        wb_ref[...] = w_ref[0].astype(BF16)

    @pl.when(i < nact_ref[0])
    def _():
        a = a_ref[...]
        if lhs_silu:
            a = _silu(a)
        acc = jnp.dot(a.astype(BF16), wb_ref[...], preferred_element_type=F32)
        if has_bias:
            acc = acc + b_ref[0]
        o_ref[...] = acc.astype(o_ref.dtype)

    @pl.when(i >= nact_ref[0])
    def _():
        o_ref[...] = jnp.zeros_like(o_ref)


def _gmm(a, w, block_e, nact, *, tm, tn, out_dtype, bias=None, lhs_silu=False):
    m, k = a.shape
    _, _, n = w.shape
    grid = (n // tn, m // tm)
    in_specs = [
        pl.BlockSpec((tm, k), lambda j, i, be, na: (i, 0)),
        pl.BlockSpec((1, k, tn), lambda j, i, be, na: (be[i], 0, j)),
    ]
    args = [a, w]
    if bias is not None:
        in_specs.append(pl.BlockSpec((1, 1, tn), lambda j, i, be, na: (be[i], 0, j)))
        args.append(bias)
    return pl.pallas_call(
        functools.partial(_gmm_kernel, lhs_silu=lhs_silu, has_bias=bias is not None),
        grid_spec=pltpu.PrefetchScalarGridSpec(
            num_scalar_prefetch=2,
            grid=grid,
            in_specs=in_specs,
            out_specs=pl.BlockSpec((tm, tn), lambda j, i, be, na: (i, j)),
            scratch_shapes=[pltpu.VMEM((k, tn), BF16)],
        ),
        out_shape=jax.ShapeDtypeStruct((m, n), out_dtype),
        compiler_params=_cparams(("arbitrary", "arbitrary")),
    )(block_e, nact, *args)


CAST_ROWS = 32


def _emm_kernel(be_ref, nact_ref, nexte_ref, a_hbm, *rest, n_w, tm, nb):
    w_hbm = rest[:n_w]
    o_hbm, abuf, obuf, stage_ref, wb_ref, asem, osem, wsem = rest[n_w:]
    j = pl.program_id(0)
    last_tile = j + 1 == pl.num_programs(0)
    tn = obuf.shape[2]
    nact = nact_ref[0]
    base = (j * nact) % 2

    def a_copy(i, slot):
        rows = pl.ds(pl.multiple_of(i * tm, tm), tm)
        return pltpu.make_async_copy(a_hbm.at[rows, :], abuf.at[slot], asem.at[slot])

    def o_copy(i, slot):
        rows = pl.ds(pl.multiple_of(i * tm, tm), tm)
        cols = pl.ds(pl.multiple_of(j * tn, LANES), tn)
        return pltpu.make_async_copy(obuf.at[slot], o_hbm.at[rows, cols], osem.at[slot])

    def slab_copies(expert, col_tile):
        cols = pl.ds(pl.multiple_of(col_tile * tn, LANES), tn)
        return [pltpu.make_async_copy(w_hbm[t].at[expert, :, cols], stage_ref.at[t], wsem.at[t])
                for t in range(n_w)]

    @pl.when(j == 0)
    def _():
        for c in slab_copies(be_ref[0], j):
            c.start()
        a_copy(0, base).start()

    def block(i, carry):
        slot = (base + i) % 2
        e = be_ref[i]

        @pl.when(i + 1 < nact)
        def _():
            a_copy(i + 1, 1 - slot).start()

        @pl.when((i + 1 == nact) & jnp.logical_not(last_tile))
        def _():
            a_copy(0, 1 - slot).start()

        @pl.when((i == 0) | (e != be_ref[jnp.maximum(i - 1, 0)]))
        def _():
            for c in slab_copies(e, j):
                c.wait()

            def round_rows(c, carry):
                rows = pl.ds(pl.multiple_of(c * CAST_ROWS, CAST_ROWS), CAST_ROWS)
                for t in range(n_w):
                    wb_ref[t, rows, :] = stage_ref[t, rows, :].astype(BF16)
                return carry
            lax.fori_loop(0, stage_ref.shape[1] // CAST_ROWS, round_rows, 0)
            nxt = nexte_ref[i]

            @pl.when(nxt >= 0)
            def _():
                for c in slab_copies(nxt, j):
                    c.start()

            @pl.when((nxt < 0) & jnp.logical_not(last_tile))
            def _():
                for c in slab_copies(be_ref[0], j + 1):
                    c.start()

        a_copy(i, slot).wait()

        @pl.when(i >= 2)
        def _():
            o_copy(i - 2, slot).wait()

        a = abuf[slot]
        acc = jnp.dot(a, wb_ref[0], preferred_element_type=F32)
        if n_w == 2:
            acc = _silu(acc) * jnp.dot(a, wb_ref[1], preferred_element_type=F32)
        obuf[slot] = acc.astype(obuf.dtype)
        o_copy(i, slot).start()
        return carry

    lax.fori_loop(0, nact, block, 0)

    @pl.when(nact >= 2)
    def _():
        o_copy(nact - 2, (base + nact) % 2).wait()
    o_copy(nact - 1, (base + nact - 1) % 2).wait()

    @pl.when(nact < nb)
    def _():
        obuf[0] = jnp.zeros(obuf.shape[1:], obuf.dtype)

        def zero_block(i, carry):
            c = o_copy(i, 0)
            c.start()
            c.wait()
            return carry
        lax.fori_loop(nact, nb, zero_block, 0)


def _emm(a, ws, block_e, nact, next_e, *, tm, tn, out_dtype):
    m, k = a.shape
    n = ws[0].shape[2]
    n_w = len(ws)
    any_spec = pl.BlockSpec(memory_space=pl.ANY)
    return pl.pallas_call(
        functools.partial(_emm_kernel, n_w=n_w, tm=tm, nb=m // tm),
        grid_spec=pltpu.PrefetchScalarGridSpec(
            num_scalar_prefetch=3,
            grid=(n // tn,),
            in_specs=[any_spec] * (1 + n_w),
            out_specs=any_spec,
            scratch_shapes=[pltpu.VMEM((2, tm, k), BF16), pltpu.VMEM((2, tm, tn), out_dtype),
                            pltpu.VMEM((n_w, k, tn), F32), pltpu.VMEM((n_w, k, tn), BF16),
                            pltpu.SemaphoreType.DMA((2,)), pltpu.SemaphoreType.DMA((2,)),
                            pltpu.SemaphoreType.DMA((n_w,))],
        ),
        out_shape=jax.ShapeDtypeStruct((m, n), out_dtype),
        compiler_params=_cparams(("arbitrary",)),
    )(block_e, nact, next_e, a, *ws)


ELT_TILE = 512
ROUTER_TILE = 256


def _rep_rows(s, tile):
    return s if tile == DEC_BATCH else jnp.concatenate([s] * (tile // DEC_BATCH), axis=0)


def _pick(i, p_ref, s_ref, tile=ROW_TILE):
    return jnp.where(i < N_PROMPT // tile, p_ref[0], _rep_rows(s_ref[...], tile))


def _prenorm_kernel(xp_ref, xs_ref, g_ref, scp_ref, scs_ref, shp_ref, shs_ref, h_ref):
    i = pl.program_id(0)
    tile = h_ref.shape[0]
    x = jnp.where(i < N_PROMPT // tile, xp_ref[...], _rep_rows(xs_ref[...], tile))
    sc = _pick(i, scp_ref, scs_ref, tile)
    sh = _pick(i, shp_ref, shs_ref, tile)
    h_ref[...] = (_rms(x, g_ref[...]) * (1.0 + sc) + sh).astype(h_ref.dtype)


def _resid_kernel(*refs, x_split, y_split, with_next, with_router, h_dtype):
    it = iter(refs)
    if x_split:
        xp_ref, xs_ref = next(it), next(it)
    else:
        x_ref = next(it)
    f_ref, gpost_ref, gap_ref, gas_ref = [next(it) for _ in range(4)]
    if with_next:
        gpre_ref, scp_ref, scs_ref, shp_ref, shs_ref = [next(it) for _ in range(5)]
    if with_router:
        rw_ref, rb_ref = next(it), next(it)
    if y_split:
        yp_ref, ys_ref = next(it), next(it)
    else:
        xo_ref = next(it)
    if with_next:
        h_ref = next(it)
    if with_router:
        ri_ref, gate_ref, cnt_ref, carry_ref = [next(it) for _ in range(4)]
    i = pl.program_id(0)
    tile = f_ref.shape[0]
    prompt_tiles = N_PROMPT // tile
    ga = _pick(i, gap_ref, gas_ref, tile)
    if x_split:
        x = jnp.where(i < prompt_tiles, xp_ref[...], _rep_rows(xs_ref[...], tile))
    else:
        x = x_ref[...]
    x = x + ga * _rms(f_ref[...], gpost_ref[...])
    if y_split:
        @pl.when(i < prompt_tiles)
        def _():
            yp_ref[...] = x

        @pl.when(i == prompt_tiles)
        def _():
            ys_ref[...] = x[:DEC_BATCH]
    else:
        xo_ref[...] = x
    if with_next:
        sc = _pick(i, scp_ref, scs_ref, tile)
        sh = _pick(i, shp_ref, shs_ref, tile)
        h = _rms(x, gpre_ref[...]) * (1.0 + sc) + sh
        h_ref[...] = h.astype(h_dtype)
        if with_router:
            logits = jnp.dot(h, rw_ref[...], precision=HIGHEST, preferred_element_type=F32) + rb_ref[...]
            _route_tile(logits, ri_ref, gate_ref, cnt_ref, carry_ref)


def _n_tiles(tile):
    return N_PROMPT // tile + 1


def _row_spec(width, tile=ROW_TILE):
    return pl.BlockSpec((tile, width), lambda i, *_: (i, 0))


def _vec_spec():
    return pl.BlockSpec((1, D_MODEL), lambda i, *_: (0, 0))


def _modp_spec(tile=ROW_TILE):
    return pl.BlockSpec((1, 1, D_MODEL), lambda i, *_: (jnp.minimum(i // (SEQ // tile), BATCH - 1), 0, 0))


def _mods_spec():
    return pl.BlockSpec((DEC_BATCH, D_MODEL), lambda i, *_: (0, 0))


def _prompt_rows_spec(tile=ROW_TILE):
    return pl.BlockSpec((tile, D_MODEL), lambda i, *_: (jnp.minimum(i, N_PROMPT // tile - 1), 0))


def _prenorm(xp, xs, g, sc, sh):
    t = ELT_TILE
    return pl.pallas_call(
        _prenorm_kernel,
        grid=(_n_tiles(t),),
        in_specs=[_prompt_rows_spec(t), _mods_spec(), _vec_spec(), _modp_spec(t), _mods_spec(), _modp_spec(t),
                  _mods_spec()],
        out_specs=_row_spec(D_MODEL, t),
        out_shape=jax.ShapeDtypeStruct((N_ROWS, D_MODEL), BF16),
        compiler_params=_cparams(("arbitrary",)),
    )(xp, xs, g, sc[0], sc[1], sh[0], sh[1])


def _resid(x, f, gpost, ga, nxt=None, router=None, h_dtype=BF16, y_split=False):
    t = ELT_TILE if router is None else ROUTER_TILE
    x_split = isinstance(x, tuple)
    if x_split:
        args = [x[0], x[1]]
        in_specs = [_prompt_rows_spec(t), _mods_spec()]
    else:
        args = [x]
        in_specs = [_row_spec(D_MODEL, t)]
    args += [f, gpost, ga[0], ga[1]]
    in_specs += [_row_spec(D_MODEL, t), _vec_spec(), _modp_spec(t), _mods_spec()]
    if y_split:
        out_shape = [jax.ShapeDtypeStruct((N_PROMPT, D_MODEL), F32), jax.ShapeDtypeStruct((DEC_BATCH, D_MODEL), F32)]
        out_specs = [_prompt_rows_spec(t), _mods_spec()]
    else:
        out_shape = [jax.ShapeDtypeStruct((N_ROWS, D_MODEL), F32)]
        out_specs = [_row_spec(D_MODEL, t)]
    if nxt is not None:
        gpre, sc, sh = nxt
        args += [gpre, sc[0], sc[1], sh[0], sh[1]]
        in_specs += [_vec_spec(), _modp_spec(t), _mods_spec(), _modp_spec(t), _mods_spec()]
        out_shape.append(jax.ShapeDtypeStruct((N_ROWS, D_MODEL), h_dtype))
        out_specs.append(_row_spec(D_MODEL, t))
    if router is not None:
        rw, rb = router
        args += [rw, rb]
        in_specs += [pl.BlockSpec((D_MODEL, LANES), lambda i: (0, 0)), pl.BlockSpec((1, LANES), lambda i: (0, 0))]
        out_shape += [jax.ShapeDtypeStruct((N_ROWS, LANES), I32), jax.ShapeDtypeStruct((N_ROWS, LANES), F32),
                      jax.ShapeDtypeStruct((1, LANES), F32)]
        out_specs += [_row_spec(LANES, t), _row_spec(LANES, t), pl.BlockSpec((1, LANES), lambda i: (0, 0))]
        scratch = [pltpu.VMEM((1, LANES), F32)]
    else:
        scratch = []
    return pl.pallas_call(
        functools.partial(_resid_kernel, x_split=x_split, y_split=y_split, with_next=nxt is not None,
                          with_router=router is not None, h_dtype=h_dtype),
        grid=(_n_tiles(t),),
        in_specs=in_specs,
        out_specs=out_specs,
        out_shape=out_shape,
        scratch_shapes=scratch,
        compiler_params=_cparams(("arbitrary",)),
    )(*args)


def _group_tables():
    grp = np.arange(CONV_DIM)[:, None] // (CONV_DIM // CONV_GROUPS) == np.arange(LANES)[None, :]
    return jnp.asarray(grp, BF16), jnp.asarray(grp.T, BF16)


def _split2(x):
    hi = x.astype(BF16)
    return hi, (x - hi.astype(F32)).astype(BF16)


def _group_rms(y, gnorm, sel, sel_t):
    width = CONV_DIM // CONV_GROUPS
    hi, lo = _split2(y * y)
    ss = jnp.dot(hi, sel, preferred_element_type=F32) + jnp.dot(lo, sel, preferred_element_type=F32)
    hi, lo = _split2(lax.rsqrt(ss * (1.0 / width) + EPS))
    scale = jnp.dot(hi, sel_t, preferred_element_type=F32) + jnp.dot(lo, sel_t, preferred_element_type=F32)
    return y * scale * gnorm


def _conv_prompt_kernel(hc_ref, bg_ref, cg_ref, hch_ref, cgh_ref, cw_ref, gn_ref, sel_ref, selt_ref, mix_ref,
                        y_ref, nc_ref, *, tt):
    del mix_ref
    t = pl.program_id(1)
    u = cg_ref[...] * hc_ref[...]
    halo = jnp.where(t == 0, 0.0, cgh_ref[...] * hch_ref[...])
    h1 = halo[SUBLANES - 1:SUBLANES]
    h2 = halo[SUBLANES - 2:SUBLANES - 1]
    row = lax.broadcasted_iota(I32, u.shape, 0)
    u1 = jnp.where(row == 0, h1, pltpu.roll(u, 1, 0))
    u2 = jnp.where(row == 0, h2, jnp.where(row == 1, h1, pltpu.roll(u, 2, 0)))
    conv = cw_ref[0:1] * u2 + cw_ref[1:2] * u1 + cw_ref[2:3] * u
    y_ref[...] = _group_rms(bg_ref[...] * conv, gn_ref[...], sel_ref[...], selt_ref[...]).astype(y_ref.dtype)

    @pl.when(t == pl.num_programs(1) - 1)
    def _():
        nc_ref[0] = u[tt - (CONV_W - 1):]


def _conv_prompt(proj, conv_w, conv_norm, mix, *, tt=512):
    nt = SEQ // tt
    cblk = lambda c: pl.BlockSpec((tt, CONV_DIM), lambda b, t: (b * nt + t, c))
    hblk = lambda c: pl.BlockSpec(
        (SUBLANES, CONV_DIM), lambda b, t: (jnp.maximum((b * nt + t) * (tt // SUBLANES) - 1, 0), c))
    return pl.pallas_call(
        functools.partial(_conv_prompt_kernel, tt=tt),
        grid=(BATCH, nt),
        in_specs=[cblk(0), cblk(1), cblk(2), hblk(0), hblk(2),
                  pl.BlockSpec((CONV_W, CONV_DIM), lambda b, t: (0, 0)),
                  pl.BlockSpec((1, CONV_DIM), lambda b, t: (0, 0)),
                  pl.BlockSpec((CONV_DIM, LANES), lambda b, t: (0, 0)),
                  pl.BlockSpec((LANES, CONV_DIM), lambda b, t: (0, 0)),
                  pl.BlockSpec(memory_space=pl.ANY)],
        out_specs=[pl.BlockSpec((tt, CONV_DIM), lambda b, t: (b * nt + t, 0)),
                   pl.BlockSpec((1, CONV_W - 1, CONV_DIM), lambda b, t: (b, 0, 0))],
        out_shape=[jax.ShapeDtypeStruct(mix.shape, mix.dtype),
                   jax.ShapeDtypeStruct((BATCH, CONV_W - 1, CONV_DIM), F32)],
        input_output_aliases={9: 0},
        compiler_params=_cparams(("arbitrary", "arbitrary")),
    )(proj, proj, proj, proj, proj, conv_w, conv_norm, *_group_tables(), mix)


HG_LEVELS = tuple(1 << i for i in range(ROW_TILE.bit_length() - 1))
LOG2_E = 1.4426950408889634


def _gates(z, lb):
    sg = jax.nn.sigmoid(z)
    f = lb + (1.0 - lb) * sg
    return jnp.log(jnp.maximum(f, F_MIN)), (1.0 - lb) * (1.0 - sg)


def _level_table():
    t = np.arange(ROW_TILE)[:, None]
    s = np.arange(ROW_TILE)[None, :]
    x = np.maximum(t ^ s, 1)
    lvl = np.floor(np.log2(x)).astype(np.int32)
    lvl = np.where(t == s, len(HG_LEVELS), np.where(s < t, lvl, -1))
    return jnp.asarray(lvl, I32)


def _nt_dot(x, y):
    return lax.dot_general(x, y, (((1,), (1,)), ((), ())), preferred_element_type=F32)


def _hgrn_tile(q, z, v, lb, st, tri, lvl):
    logf, k = _gates(z, lb)
    lf2 = logf * LOG2_E
    hi = lf2.astype(BF16)
    rem = lf2 - hi.astype(F32)
    mid = rem.astype(BF16)
    lo = (rem - mid.astype(F32)).astype(BF16)
    parts = jnp.dot(tri, jnp.concatenate([hi, mid, lo], axis=1), preferred_element_type=F32)
    a = parts[:, :HG_DK] + parts[:, HG_DK:2 * HG_DK] + parts[:, 2 * HG_DK:]
    row = lax.broadcasted_iota(I32, (ROW_TILE, HG_DK), 0)
    a8 = a.reshape(ROW_TILE // SUBLANES, SUBLANES, HG_DK)
    sub8 = lax.broadcasted_iota(I32, a8.shape, 1)
    scores = jnp.where(lvl == len(HG_LEVELS), _nt_dot(q.astype(BF16), k.astype(BF16)), 0.0)
    for li, c in enumerate(HG_LEVELS):
        if c < SUBLANES:
            if c == 1:
                d = jnp.where((row & 1) == 1, lf2, 0.0)
            elif c == 2:
                anchor = jnp.where(sub8 < 4, a8[:, 1:2, :], a8[:, 5:6, :])
                d = (a8 - anchor).reshape(ROW_TILE, HG_DK)
            else:
                d = (a8 - a8[:, c - 1:c, :]).reshape(ROW_TILE, HG_DK)
            src = jnp.where((row & c) != 0, q, k)
            neg = -jnp.abs(d)
        else:
            shape3 = (ROW_TILE // (2 * c), 2 * c, HG_DK)
            ab, qb, kb = a.reshape(shape3), q.reshape(shape3), k.reshape(shape3)
            anchor = ab[:, c - 1:c, :]
            neg = jnp.concatenate([anchor - ab[:, :c, :], ab[:, c:, :] - anchor], axis=1).reshape(ROW_TILE, HG_DK)
            src = jnp.concatenate([kb[:, :c, :], qb[:, c:, :]], axis=1).reshape(ROW_TILE, HG_DK)
        x = (src * jnp.exp2(neg)).astype(BF16)
        scores = jnp.where(lvl == li, _nt_dot(x, x), scores)
    vb = v.astype(BF16)
    a_last = a[ROW_TILE - 1:ROW_TILE, :]
    o = jnp.dot(scores.astype(BF16), vb, preferred_element_type=F32)
    o = o + _nt_dot((q * jnp.exp2(a)).astype(BF16), st.astype(BF16))
    kt = (k * jnp.exp2(a_last - a)).astype(BF16)
    st = st * jnp.exp2(a_last) + lax.dot_general(vb, kt, (((0,), (0,)), ((), ())),
                                                 preferred_element_type=F32)
    return o, st


def _hgrn_prompt_kernel(q_ref, z_ref, v_ref, og_ref, lb_ref, gn_ref, tri_ref, lvl_ref, mix_ref, o_ref, s_ref,
                        st_ref, *, tt):
    del mix_ref
    t = pl.program_id(1)

    @pl.when(t == 0)
    def _():
        st_ref[...] = jnp.zeros_like(st_ref)

    def body(j, carry):
        rows = pl.ds(pl.multiple_of(j * ROW_TILE, ROW_TILE), ROW_TILE)
        for h in range(HG_HEADS):
            cols = slice(h * LANES, (h + 1) * LANES)
            o, st = _hgrn_tile(q_ref[rows, cols], z_ref[rows, cols], v_ref[rows, cols], lb_ref[:, cols],
                               st_ref[h], tri_ref[...], lvl_ref[...])
            st_ref[h] = st
            o_ref[rows, cols] = (_rms(o, gn_ref[:, cols]) * _silu(og_ref[rows, cols])).astype(o_ref.dtype)
        return carry

    lax.fori_loop(0, tt // ROW_TILE, body, 0)

    @pl.when(t == pl.num_programs(1) - 1)
    def _():
        for h in range(HG_HEADS):
            s_ref[0, h] = st_ref[h].T


def _hgrn_prompt(proj, lb, hgrn_norm, mix, *, tt=512):
    nt = SEQ // tt
    width = HG_HEADS * LANES
    col0 = 3 * CONV_DIM // width
    blk = lambda part: pl.BlockSpec((tt, width), lambda b, t: (b * nt + t, col0 + part))
    hvec = pl.BlockSpec((1, width), lambda b, t: (0, 0))
    const = pl.BlockSpec((ROW_TILE, ROW_TILE), lambda b, t: (0, 0))
    tri = jnp.asarray(np.tril(np.ones((ROW_TILE, ROW_TILE), np.float32)), BF16)
    return pl.pallas_call(
        functools.partial(_hgrn_prompt_kernel, tt=tt),
        grid=(BATCH, nt),
        in_specs=[blk(0), blk(1), blk(2), blk(3), hvec, hvec, const, const, pl.BlockSpec(memory_space=pl.ANY)],
        out_specs=[pl.BlockSpec((tt, width), lambda b, t: (b * nt + t, 1)),
                   pl.BlockSpec((1, HG_HEADS, HG_DK, HG_DV), lambda b, t: (b, 0, 0, 0))],
        out_shape=[jax.ShapeDtypeStruct(mix.shape, mix.dtype),
                   jax.ShapeDtypeStruct((BATCH, HG_HEADS, HG_DK, HG_DV), F32)],
        scratch_shapes=[pltpu.VMEM((HG_HEADS, HG_DV, HG_DK), F32)],
        input_output_aliases={8: 0},
        compiler_params=_cparams(("arbitrary", "arbitrary")),
    )(proj, proj, proj, proj, lb.reshape(1, width), hgrn_norm.reshape(1, width), tri, _level_table(), mix)


def _conv_sample_kernel(hc_ref, bg_ref, cg_ref, cs_ref, cw_ref, gn_ref, sel_ref, selt_ref, mix_ref, y_ref, nc_ref):
    del mix_ref
    u = cg_ref[...] * hc_ref[...]
    s0 = cs_ref[:, 0, :]
    s1 = cs_ref[:, 1, :]
    conv = cw_ref[0:1] * s0 + cw_ref[1:2] * s1 + cw_ref[2:3] * u
    y_ref[...] = _group_rms(bg_ref[...] * conv, gn_ref[...], sel_ref[...], selt_ref[...]).astype(y_ref.dtype)
    nc_ref[:, 0, :] = s1
    nc_ref[:, 1, :] = u


def _conv_sample(proj, conv_state, conv_w, conv_norm, mix):
    rb = PROMPT_TILES
    cblk = lambda c: pl.BlockSpec((DEC_BATCH, CONV_DIM), lambda i: (rb, c))
    return pl.pallas_call(
        _conv_sample_kernel,
        grid=(1,),
        in_specs=[cblk(0), cblk(1), cblk(2),
                  pl.BlockSpec((DEC_BATCH, CONV_W - 1, CONV_DIM), lambda i: (0, 0, 0)),
                  pl.BlockSpec((CONV_W, CONV_DIM), lambda i: (0, 0)),
                  pl.BlockSpec((1, CONV_DIM), lambda i: (0, 0)),
                  pl.BlockSpec((CONV_DIM, LANES), lambda i: (0, 0)),
                  pl.BlockSpec((LANES, CONV_DIM), lambda i: (0, 0)),
                  pl.BlockSpec(memory_space=pl.ANY)],
        out_specs=[pl.BlockSpec((DEC_BATCH, CONV_DIM), lambda i: (rb, 0)),
                   pl.BlockSpec((DEC_BATCH, CONV_W - 1, CONV_DIM), lambda i: (0, 0, 0))],
        out_shape=[jax.ShapeDtypeStruct(mix.shape, mix.dtype),
                   jax.ShapeDtypeStruct((DEC_BATCH, CONV_W - 1, CONV_DIM), F32)],
        input_output_aliases={8: 0},
        compiler_params=_cparams(("arbitrary",)),
    )(proj, proj, proj, conv_state, conv_w, conv_norm, *_group_tables(), mix)


def _hgrn_sample_kernel(q_ref, z_ref, v_ref, og_ref, lb_ref, gn_ref, s_ref, *rest, bg, slab):
    o_ref, so_ref, osc_ref = rest[-3:]
    s_ref = s_ref.at[0]
    for other in range(so_ref.shape[0]):
        if other != slab:
            so_ref[other] = jnp.zeros(so_ref.shape[1:], so_ref.dtype)
    so_ref = so_ref.at[slab]
    g = pl.program_id(1)
    lb = lb_ref[0]
    logf, k = _gates(z_ref[...], lb)
    f = jnp.exp(logf)
    shift = (DEC_BATCH - g * bg) % DEC_BATCH
    ft = pltpu.roll(f.T, shift, 1)
    kt = pltpu.roll(k.T, shift, 1)
    rows = pl.ds(pl.multiple_of(g * bg, bg), bg)
    v = v_ref[rows, :]
    q = q_ref[rows, :]
    for j in range(bg):
        s_new = ft[:, j:j + 1] * s_ref[j, 0] + kt[:, j:j + 1] * v[j:j + 1, :]
        so_ref[j, 0] = s_new
        osc_ref[j:j + 1, :] = jnp.dot(q[j:j + 1, :].astype(BF16), s_new.astype(BF16),
                                      preferred_element_type=F32)
    o = osc_ref[...]
    o_ref[...] = (_rms(o, gn_ref[0]) * _silu(og_ref[rows, :])).astype(o_ref.dtype)


def _hgrn_sample(proj, state_all, l, lb, hgrn_norm, mix, new_state_all=None, *, bg=32):
    rb = PROMPT_TILES
    col0 = 3 * CONV_DIM // LANES
    blk = lambda part: pl.BlockSpec((DEC_BATCH, LANES), lambda h, g: (rb, col0 + part * HG_HEADS + h))
    hvec = pl.BlockSpec((1, 1, LANES), lambda h, g: (h, 0, 0))
    sblk = pl.BlockSpec((1, bg, 1, HG_DK, HG_DV), lambda h, g: (l, g, h, 0, 0))
    any_spec = pl.BlockSpec(memory_space=pl.ANY)
    args = [proj, proj, proj, proj, lb.reshape(HG_HEADS, 1, HG_DK), hgrn_norm.reshape(HG_HEADS, 1, HG_DV),
            state_all, mix]
    in_specs = [blk(0), blk(1), blk(2), blk(3), hvec, hvec, sblk, any_spec]
    aliases = {7: 0}
    if new_state_all is not None:
        args.append(new_state_all)
        in_specs.append(any_spec)
        aliases[8] = 1
        so_blk, slab = sblk, 0
    else:
        n_slabs = state_all.shape[0]
        so_blk = pl.BlockSpec((n_slabs, bg, 1, HG_DK, HG_DV), lambda h, g: (0, g, h, 0, 0))
        slab = l
    return pl.pallas_call(
        functools.partial(_hgrn_sample_kernel, bg=bg, slab=slab),
        grid=(HG_HEADS, DEC_BATCH // bg),
        in_specs=in_specs,
        out_specs=[pl.BlockSpec((bg, LANES), lambda h, g: (N_PROMPT // bg + g, CONV_DIM // LANES + h)), so_blk],
        out_shape=[jax.ShapeDtypeStruct(mix.shape, mix.dtype),
                   jax.ShapeDtypeStruct(state_all.shape, F32)],
        scratch_shapes=[pltpu.VMEM((bg, HG_DV), F32)],
        input_output_aliases=aliases,
        compiler_params=_cparams(("arbitrary", "arbitrary")),
    )(*args)


def _route_tile(logits, ri_ref, gate_ref, cnt_ref, carry_ref):
    i = pl.program_id(0)

    @pl.when(i == 0)
    def _():
        carry_ref[...] = jnp.zeros_like(carry_ref)

    tile = logits.shape[0]
    lane = lax.broadcasted_iota(I32, (tile, LANES), 1)
    lanef = lane.astype(F32)
    valid = i * tile + lax.broadcasted_iota(I32, (tile, LANES), 0) < N_ROWS
    lg = jnp.where(valid & (lane < N_EXPERTS), logits, jnp.where(valid, -jnp.inf, 0.0))
    m1 = jnp.max(lg, axis=-1, keepdims=True)
    i1 = jnp.min(jnp.where(lg == m1, lanef, float(LANES)), axis=-1, keepdims=True).astype(I32)
    lg2 = jnp.where(lane == i1, -jnp.inf, lg)
    m2 = jnp.max(lg2, axis=-1, keepdims=True)
    i2 = jnp.min(jnp.where(lg2 == m2, lanef, float(LANES)), axis=-1, keepdims=True).astype(I32)
    e = jnp.exp(m2 - m1)
    g1 = 1.0 / (1.0 + e)
    g2 = e / (1.0 + e)
    hot1 = lane == i1
    hot2 = lane == i2
    hot = jnp.where(valid, (hot1 | hot2).astype(F32), 0.0).astype(BF16)
    r = lax.broadcasted_iota(I32, (tile, tile), 0)
    c = lax.broadcasted_iota(I32, (tile, tile), 1)
    before = (c < r).astype(BF16)
    tot = jnp.dot(before, hot, preferred_element_type=F32) + carry_ref[...]
    r1 = jnp.sum(jnp.where(hot1, tot, 0.0), axis=-1, keepdims=True).astype(I32)
    r2 = jnp.sum(jnp.where(hot2, tot, 0.0), axis=-1, keepdims=True).astype(I32)
    ri_ref[...] = jnp.where(lane == 0, i1, jnp.where(lane == 1, i2, jnp.where(lane == 2, r1,
                            jnp.where(lane == 3, r2, 0))))
    gate_ref[...] = jnp.where(lane == 0, g1, jnp.where(lane == 1, g2, 0.0))
    carry_ref[...] += jnp.sum(hot.astype(F32), axis=0, keepdims=True)
    cnt_ref[...] = carry_ref[...]


DMA_GROUP = 8


def _gather_kernel(d1_ref, d2_ref, nact_ref, h_ref, xb_ref, tok_ref, buf_ref, sem):
    i = pl.program_id(0)

    nact = nact_ref[0]

    @pl.when(i == 0)
    def _():
        def clear(p, carry):
            tok_ref[p] = 0
            return carry
        lax.fori_loop(0, P_ROWS, clear, 0, unroll=8)

        def scatter(t, carry):
            tok_ref[d1_ref[t]] = t
            tok_ref[d2_ref[t]] = t
            return carry
        lax.fori_loop(0, N_ROWS, scatter, 0, unroll=4)

    def start_block(b):
        slot = b % 2

        def start(g, carry):
            for k in range(DMA_GROUP):
                r = g * DMA_GROUP + k
                tok = tok_ref[b * MOE_BLOCK + r]
                pltpu.make_async_copy(h_ref.at[pl.ds(tok, 1), :], buf_ref.at[slot, pl.ds(r, 1), :],
                                      sem.at[slot]).start()
            return carry
        lax.fori_loop(0, MOE_BLOCK // DMA_GROUP, start, 0)

    @pl.when(i == 0)
    def _():
        start_block(i)

    @pl.when(i + 1 < nact)
    def _():
        start_block(i + 1)

    @pl.when(i < nact)
    def _():
        slot = i % 2
        pltpu.make_async_copy(h_ref.at[pl.ds(0, MOE_BLOCK), :], buf_ref.at[slot], sem.at[slot]).wait()
        xb_ref[...] = buf_ref[slot].astype(xb_ref.dtype)

    @pl.when(i >= nact)
    def _():
        xb_ref[...] = jnp.zeros_like(xb_ref)


def _gather(h, dest1, dest2, nact):
    return pl.pallas_call(
        _gather_kernel,
        grid_spec=pltpu.PrefetchScalarGridSpec(
            num_scalar_prefetch=3,
            grid=(N_BLOCKS,),
            in_specs=[pl.BlockSpec(memory_space=pl.ANY)],
            out_specs=pl.BlockSpec((MOE_BLOCK, D_MODEL), lambda i, d1, d2, na: (i, 0)),
            scratch_shapes=[pltpu.SMEM((P_ROWS,), I32), pltpu.VMEM((2, MOE_BLOCK, D_MODEL), F32),
                            pltpu.SemaphoreType.DMA((2,))],
        ),
        out_shape=jax.ShapeDtypeStruct((P_ROWS, D_MODEL), BF16),
        compiler_params=_cparams(("arbitrary",)),
    )(dest1, dest2, nact, h)


def _combine_resid_kernel(d1_ref, d2_ref, x_ref, gate_ref, gpost_ref, gap_ref, gas_ref, yb_ref, yp_ref, ys_ref,
                          b1_ref, b2_ref, sem):
    i = pl.program_id(0)

    def start_tile(t):
        slot = t % 2

        def start(g, carry):
            for k in range(DMA_GROUP):
                r = g * DMA_GROUP + k
                tok = t * ROW_TILE + r
                pltpu.make_async_copy(yb_ref.at[pl.ds(d1_ref[tok], 1), :], b1_ref.at[slot, pl.ds(r, 1), :],
                                      sem.at[0, slot]).start()
                pltpu.make_async_copy(yb_ref.at[pl.ds(d2_ref[tok], 1), :], b2_ref.at[slot, pl.ds(r, 1), :],
                                      sem.at[1, slot]).start()
            return carry
        lax.fori_loop(0, ROW_TILE // DMA_GROUP, start, 0)

    @pl.when(i == 0)
    def _():
        start_tile(i)

    @pl.when(i + 1 < pl.num_programs(0))
    def _():
        start_tile(i + 1)

    slot = i % 2
    pltpu.make_async_copy(yb_ref.at[pl.ds(0, ROW_TILE), :], b1_ref.at[slot], sem.at[0, slot]).wait()
    pltpu.make_async_copy(yb_ref.at[pl.ds(0, ROW_TILE), :], b2_ref.at[slot], sem.at[1, slot]).wait()
    gate = gate_ref[...]
    f = gate[:, 0:1] * b1_ref[slot] + gate[:, 1:2] * b2_ref[slot]
    x = x_ref[...] + _pick(i, gap_ref, gas_ref) * _rms(f, gpost_ref[...])

    @pl.when(i < PROMPT_TILES)
    def _():
        yp_ref[...] = x

    @pl.when(i == PROMPT_TILES)
    def _():
        ys_ref[...] = x


def _combine_resid(x, yb, gates, dest1, dest2, gpost, ga):
    return pl.pallas_call(
        _combine_resid_kernel,
        grid_spec=pltpu.PrefetchScalarGridSpec(
            num_scalar_prefetch=2,
            grid=(N_TILES,),
            in_specs=[_row_spec(D_MODEL), _row_spec(LANES), _vec_spec(), _modp_spec(), _mods_spec(),
                      pl.BlockSpec(memory_space=pl.ANY)],
            out_specs=[_prompt_rows_spec(), _mods_spec()],
            scratch_shapes=[pltpu.VMEM((2, ROW_TILE, D_MODEL), F32), pltpu.VMEM((2, ROW_TILE, D_MODEL), F32),
                            pltpu.SemaphoreType.DMA((2, 2))],
        ),
        out_shape=[jax.ShapeDtypeStruct((N_PROMPT, D_MODEL), F32), jax.ShapeDtypeStruct((DEC_BATCH, D_MODEL), F32)],
        compiler_params=_cparams(("arbitrary",)),
    )(dest1, dest2, x, gates, gpost, ga[0], ga[1], yb)


def _moe_experts(h, ri, gates, counts, w1, w3, w2):
    counts = counts[0, :N_EXPERTS].astype(I32)
    padded = (counts + MOE_BLOCK - 1) // MOE_BLOCK * MOE_BLOCK
    pends = jnp.cumsum(padded)
    pstarts = pends - padded
    dest1 = pstarts[ri[:, 0]] + ri[:, 2]
    dest2 = pstarts[ri[:, 1]] + ri[:, 3]
    nact = (pends[-1:] // MOE_BLOCK).astype(I32)
    block_start = jnp.arange(N_BLOCKS, dtype=I32) * MOE_BLOCK
    block_e = jnp.minimum(jnp.sum((pends[None, :] <= block_start[:, None]).astype(I32), axis=1), N_EXPERTS - 1)
    run_end = (pends // MOE_BLOCK)[block_e]
    next_e = jnp.where(run_end < nact[0], block_e[jnp.minimum(run_end, N_BLOCKS - 1)], -1).astype(I32)
    xb = _gather(h, dest1, dest2, nact)
    gb = _emm(xb, (w1, w3), block_e, nact, next_e, tm=MOE_BLOCK, tn=1792, out_dtype=BF16)
    yb = _emm(gb, (w2,), block_e, nact, next_e, tm=MOE_BLOCK, tn=1024, out_dtype=F32)
    return yb, gates, dest1, dest2


def _dense(a, ws, e, *, tm, tn, out_dtype):
    nb = a.shape[0] // tm
    return _emm(a, ws, jnp.full((nb,), e, I32), jnp.full((1,), nb, I32), jnp.full((nb,), -1, I32),
                tm=tm, tn=tn, out_dtype=out_dtype)


def kernel(x_prompt, x_sample, state_conv, state_hgrn, c_prompt, c_sample, norm_pre, norm_post, w_mod, b_mod, w_in, conv_w, conv_norm, lb_logits, hgrn_norm, w_out, ffn_w1, ffn_w3, ffn_w2, router_w, router_b, moe_w1, moe_w3, moe_w2):
    p = jax.nn.softmax(lb_logits.astype(F32), axis=0)
    lb_all = jnp.cumsum(p, axis=0) - p[0:1]

    n_cond = BATCH + DEC_BATCH
    cond_rows = (n_cond + SUBLANES - 1) // SUBLANES * SUBLANES
    c_all = jnp.concatenate([c_prompt, c_sample, jnp.zeros((cond_rows - n_cond, D_MODEL), F32)], axis=0)
    mod = _gmm(jnp.concatenate([c_all] * DEPTH, axis=0), w_mod, jnp.arange(DEPTH, dtype=I32),
               jnp.full((1,), DEPTH, I32), tm=cond_rows, tn=1024, out_dtype=F32,
               bias=b_mod.reshape(DEPTH, 1, 6 * D_MODEL), lhs_silu=True)

    def mod_vec(l, j):
        rows = mod[l * cond_rows:l * cond_rows + n_cond, j * D_MODEL:(j + 1) * D_MODEL]
        return rows[:BATCH].reshape(BATCH, 1, D_MODEL), rows[BATCH:]

    x = (x_prompt.reshape(N_PROMPT, D_MODEL), x_sample.reshape(DEC_BATCH, D_MODEL))
    vec = lambda a: a.reshape(1, -1)

    h = _prenorm(x[0], x[1], vec(norm_pre[0, 0]), mod_vec(0, 1), mod_vec(0, 0))
    new_conv_p, new_hgrn_p, new_conv_s, new_hgrn_s = [], [], [], None
    for l in range(DEPTH):
        sh_f, sc_f, ga_f = mod_vec(l, 3), mod_vec(l, 4), mod_vec(l, 5)
        ga_a = mod_vec(l, 2)
        proj = _dense(h, (w_in,), l, tm=1664, tn=1024, out_dtype=F32)
        mix = jnp.zeros((N_ROWS, 2 * CONV_DIM), BF16)
        mix, nc_p = _conv_prompt(proj, conv_w[l], vec(conv_norm[l]), mix)
        mix, ns_p = _hgrn_prompt(proj, lb_all[l], hgrn_norm[l], mix)
        mix, nc_s = _conv_sample(proj, state_conv[l], conv_w[l], vec(conv_norm[l]), mix)
        mix, new_hgrn_s = _hgrn_sample(proj, state_hgrn, l, lb_all[l], hgrn_norm[l], mix, new_hgrn_s)
        mixed = _dense(mix, (w_out,), l, tm=1664, tn=1024, out_dtype=F32)
        new_conv_p.append(nc_p)
        new_hgrn_p.append(ns_p)
        new_conv_s.append(nc_s)
        nxt = (vec(norm_pre[l, 1]), sc_f, sh_f)
        if l % 2 == 0:
            j = l // 2
            x, h2 = _resid(x, mixed, vec(norm_post[l, 0]), ga_a, nxt=nxt)
            g = _dense(h2, (ffn_w1, ffn_w3), j, tm=1664, tn=512, out_dtype=BF16)
            f = _dense(g, (ffn_w2,), j, tm=1040, tn=512, out_dtype=F32)
        else:
            j = l // 2
            rw = jnp.pad(router_w[j].astype(F32), ((0, 0), (0, LANES - N_EXPERTS)))
            rb = jnp.pad(router_b[j].astype(F32), (0, LANES - N_EXPERTS)).reshape(1, LANES)
            x, h2, ri, gates, counts = _resid(x, mixed, vec(norm_post[l, 0]), ga_a, nxt=nxt, router=(rw, rb),
                                              h_dtype=F32)
            f = _moe_experts(h2, ri, gates, counts, moe_w1[j], moe_w3[j], moe_w2[j])
        if l + 1 < DEPTH:
            assert l % 2 == 0, "the expert combine is fused with the trunk's last residual step only"
            nxt = (vec(norm_pre[l + 1, 0]), mod_vec(l + 1, 1), mod_vec(l + 1, 0))
            x, h = _resid(x, f, vec(norm_post[l, 1]), ga_f, nxt=nxt)
        elif l % 2 == 0:
            y_prompt, y_sample = _resid(x, f, vec(norm_post[l, 1]), ga_f, y_split=True)
        else:
            y_prompt, y_sample = _combine_resid(x, *f, vec(norm_post[l, 1]), ga_f)

    y_prompt = y_prompt.reshape(BATCH, SEQ, D_MODEL)
    y_sample = y_sample.reshape(DEC_BATCH, 1, D_MODEL)
    return (y_prompt, y_sample, jnp.stack(new_conv_p), jnp.stack(new_hgrn_p),
            jnp.stack(new_conv_s), new_hgrn_s)
```
